```python
import math
import jax, jax.numpy as jnp
from jax import lax
import numpy as np

D_MODEL = 2048
BATCH = 4
SEQ = 4096
DEPTH = 2
DEC_BATCH = 16
DEC_SEQ = 2048
PAST_LEN = 128

N_META = 16
SSD_HEADS = 16
SSD_HEAD_DIM = 64
SSD_INNER = SSD_HEADS * SSD_HEAD_DIM
SSD_GROUPS = 4
SSD_STATE = 128
SSD_CONV = 5
SSD_CHUNK = 128
SSD_CONV_CH = SSD_INNER + 2 * SSD_GROUPS * SSD_STATE
ATT_HEADS = 8
ATT_HEAD_DIM = 64
ATT_V_DIM = 2 * ATT_HEAD_DIM
ATT_QK = ATT_HEADS * 2 * ATT_HEAD_DIM
ATT_INNER = ATT_HEADS * ATT_V_DIM
Q_BLOCK = 128
REL_BUCKETS = 32
REL_MAX_DIST = 128
IN_SIZES = (SSD_INNER, SSD_CONV_CH, 2 * SSD_HEADS, ATT_QK, ATT_QK, ATT_INNER)
IN_DIM = sum(IN_SIZES)
MIX_DIM = SSD_INNER + ATT_INNER
POOL_WINDOWS = (2, 4, 8, 16)
POOL_GROUP = D_MODEL // len(POOL_WINDOWS)
N_EXPERTS = 16
EXPERT_FF = 2048
CAPACITY_FACTOR = 2
ALPHA = (2 * DEPTH) ** 0.25
BETA = (8 * DEPTH) ** -0.25
N_EVEN = (DEPTH + 1) // 2
N_ODD = DEPTH // 2
LN_EPS = 1e-5

kernel_name = 'hybrid_ssd_diffattn_pool_ec_moe_encoder'


def layer_norm(x, g, b):
    xf = x.astype(jnp.float32)
    mu = jnp.mean(xf, axis=-1, keepdims=True)
    var = jnp.mean(jnp.square(xf - mu), axis=-1, keepdims=True)
    return ((xf - mu) * lax.rsqrt(var + LN_EPS) * g + b).astype(x.dtype)


def rms_norm(x, g):
    xf = x.astype(jnp.float32)
    y = xf * lax.rsqrt(jnp.mean(jnp.square(xf), axis=-1, keepdims=True) + LN_EPS)
    return (y * g).astype(x.dtype)


def rel_bucket(rel):
    half = REL_BUCKETS // 2
    max_exact = half // 2
    ret = jnp.where(rel > 0, half, 0)
    n = jnp.abs(rel)
    nf = jnp.maximum(n, 1).astype(jnp.float32)
    large = max_exact + (jnp.log(nf / max_exact) / math.log(REL_MAX_DIST / max_exact)
                         * (half - max_exact)).astype(jnp.int32)
    large = jnp.minimum(large, half - 1)
    return ret + jnp.where(n < max_exact, n, large)


def ssd_scan(x, dt, a_head, bm, cm):
    b, P = x.shape[0], x.shape[1]
    nc = P // SSD_CHUNK
    R = SSD_HEADS // SSD_GROUPS
    xs = (x.astype(jnp.float32) * dt[..., None]).reshape(b, nc, SSD_CHUNK, SSD_GROUPS, R, SSD_HEAD_DIM)
    a = (dt * a_head).reshape(b, nc, SSD_CHUNK, SSD_GROUPS, R).transpose(0, 3, 4, 1, 2)
    bm = bm.astype(jnp.float32).reshape(b, nc, SSD_CHUNK, SSD_GROUPS, SSD_STATE)
    cm = cm.astype(jnp.float32).reshape(b, nc, SSD_CHUNK, SSD_GROUPS, SSD_STATE)
    a_cs = jnp.cumsum(a, axis=-1)
    seg = a_cs[..., :, None] - a_cs[..., None, :]
    lower = jnp.tril(jnp.ones((SSD_CHUNK, SSD_CHUNK), dtype=bool))
    decay = jnp.exp(jnp.where(lower, seg, -jnp.inf))
    cb = jnp.einsum('bclgn,bcsgn->bgcls', cm, bm)
    y_diag = jnp.einsum('bgcls,bgrcls,bcsgrp->bclgrp', cb, decay, xs)
    decay_states = jnp.exp(a_cs[..., -1:] - a_cs)
    states = jnp.einsum('bcsgn,bgrcs,bcsgrp->cbgrpn', bm, decay_states, xs)
    chunk_decay = jnp.exp(a_cs[..., -1]).transpose(3, 0, 1, 2)

    def step(h, inp):
        s, d = inp
        return d[..., None, None] * h + s, h

    h0 = jnp.zeros((b, SSD_GROUPS, R, SSD_HEAD_DIM, SSD_STATE), jnp.float32)
    _, h_in = lax.scan(step, h0, (states, chunk_decay))
    y_off = jnp.einsum('bclgn,cbgrpn,bgrcl->bclgrp', cm, h_in, jnp.exp(a_cs))
    return (y_diag + y_off).reshape(b, P, SSD_HEADS, SSD_HEAD_DIM)


def diff_attention(q, k, v, lam, rel_bias, subln_w, lambda_init):
    b, L = q.shape[0], q.shape[1]
    q = q.reshape(b, L, ATT_HEADS, 2, ATT_HEAD_DIM).transpose(3, 0, 2, 1, 4)
    k = k.reshape(b, L, ATT_HEADS, 2, ATT_HEAD_DIM).transpose(3, 0, 2, 1, 4)
    v = v.reshape(b, L, ATT_HEADS, ATT_V_DIM).transpose(0, 2, 1, 3)
    k1, k2 = k[0], k[1]
    nblk = -(-L // Q_BLOCK)
    Lq = nblk * Q_BLOCK
    qs = jnp.pad(q, ((0, 0), (0, 0), (0, 0), (0, Lq - L), (0, 0)))
    qs = qs.reshape(2, b, ATT_HEADS, nblk, Q_BLOCK, ATT_HEAD_DIM).transpose(3, 0, 1, 2, 4, 5)
    kpos = jnp.arange(L)
    scale = ATT_HEAD_DIM ** -0.5

    def block(args):
        qb, i = args
        qpos = i * Q_BLOCK + jnp.arange(Q_BLOCK)
        bias = rel_bias[rel_bucket(kpos[None, :] - qpos[:, None])]
        bias = bias.transpose(2, 0, 1).astype(jnp.float32)[None]
        s1 = jnp.einsum('bhqd,bhkd->bhqk', qb[0], k1).astype(jnp.float32) * scale + bias
        s2 = jnp.einsum('bhqd,bhkd->bhqk', qb[1], k2).astype(jnp.float32) * scale + bias
        w = jax.nn.softmax(s1, axis=-1) - lam * jax.nn.softmax(s2, axis=-1)
        return jnp.einsum('bhqk,bhkv->bhqv', w.astype(v.dtype), v)

    o = lax.map(block, (qs, jnp.arange(nblk)))
    o = o.transpose(1, 2, 0, 3, 4).reshape(b, ATT_HEADS, Lq, ATT_V_DIM)[:, :, :L]
    o = rms_norm(o, subln_w) * (1.0 - lambda_init)
    return o.transpose(0, 2, 1, 3).reshape(b, L, ATT_INNER)


def hybrid_mixer(x, w_in, conv_w, conv_b, a_log_f, a_log_b, dt_bias_f, dt_bias_b, d_skip,
                 ssd_norm_w, lam_q1, lam_k1, lam_q2, lam_k2, subln_w, rel_bias, w_out, lambda_init):
    b, L, _ = x.shape
    proj = jnp.einsum('bld,de->ble', x, w_in)
    splits = [int(s) for s in np.cumsum(IN_SIZES)[:-1]]
    z, xbc, dt_raw, q, k, v = jnp.split(proj, splits, axis=-1)
    xbc = lax.conv_general_dilated(xbc, conv_w, window_strides=(1,),
                                   padding=[(SSD_CONV // 2, SSD_CONV // 2)],
                                   dimension_numbers=('NWC', 'WIO', 'NWC'),
                                   feature_group_count=SSD_CONV_CH)
    xbc = jax.nn.silu(xbc + conv_b)
    xs, bm, cm = jnp.split(xbc, [SSD_INNER, SSD_INNER + SSD_GROUPS * SSD_STATE], axis=-1)
    xs_h = xs.reshape(b, L, SSD_HEADS, SSD_HEAD_DIM)
    bm = bm.reshape(b, L, SSD_GROUPS, SSD_STATE)
    cm = cm.reshape(b, L, SSD_GROUPS, SSD_STATE)
    pad = (-L) % SSD_CHUNK

    def padf(t):
        return jnp.pad(t, ((0, 0), (pad, 0)) + ((0, 0),) * (t.ndim - 2))

    valid = (jnp.arange(L + pad) >= pad)[None, :, None]
    dt_p = padf(dt_raw).astype(jnp.float32)
    dt_f = jnp.where(valid, jax.nn.softplus(dt_p[..., :SSD_HEADS] + dt_bias_f), 0.0)
    dt_b = jnp.where(valid, jax.nn.softplus(dt_p[..., SSD_HEADS:] + dt_bias_b), 0.0)
    x_p, b_p, c_p = padf(xs_h), padf(bm), padf(cm)
    y_f = ssd_scan(x_p, dt_f, -jnp.exp(a_log_f.astype(jnp.float32)), b_p, c_p)

    def flip(t):
        return jnp.flip(t, axis=1)

    y_b = flip(ssd_scan(flip(x_p), flip(dt_b), -jnp.exp(a_log_b.astype(jnp.float32)), flip(b_p), flip(c_p)))
    y = (y_f + y_b)[:, pad:] + d_skip[:, None].astype(jnp.float32) * xs_h.astype(jnp.float32)
    y = y.reshape(b, L, SSD_INNER) * jax.nn.silu(z.astype(jnp.float32))
    y = rms_norm(y.reshape(b, L, SSD_GROUPS, SSD_INNER // SSD_GROUPS), 1.0).reshape(b, L, SSD_INNER)
    y_ssd = (y * ssd_norm_w).astype(x.dtype)
    lam = (jnp.exp(jnp.sum(lam_q1.astype(jnp.float32) * lam_k1))
           - jnp.exp(jnp.sum(lam_q2.astype(jnp.float32) * lam_k2)) + lambda_init)
    y_att = diff_attention(q, k, v, lam, rel_bias, subln_w, lambda_init).astype(x.dtype)
    return jnp.einsum('ble,ed->bld', jnp.concatenate([y_ssd, y_att], axis=-1), w_out)


def pool_mixer(x, pool_w, pool_b, pool_scale):
    b, L, _ = x.shape
    t = jnp.arange(L)
    outs = []
    for gi, w in enumerate(POOL_WINDOWS):
        xg = x[..., gi * POOL_GROUP:(gi + 1) * POOL_GROUP].astype(jnp.float32)
        cs = jnp.pad(jnp.cumsum(xg, axis=1), ((0, 0), (1, 0), (0, 0)))
        lo = jnp.maximum(t - w // 2, 0)
        hi = jnp.minimum(t + w // 2 - 1, L - 1)
        s = jnp.take(cs, hi + 1, axis=1) - jnp.take(cs, lo, axis=1)
        mean = s / (hi - lo + 1).astype(jnp.float32)[None, :, None]
        outs.append(jnp.einsum('bld,de->ble', (mean - xg).astype(x.dtype), pool_w[gi]))
    return ((jnp.concatenate(outs, axis=-1) + pool_b) * pool_scale).astype(x.dtype)


def expert_choice_ffn(x, w_router, w1, w3, w2):
    b, L, D = x.shape
    n = b * L
    xf = x.reshape(n, D)
    cap = max(1, CAPACITY_FACTOR * n // N_EXPERTS)
    aff = jax.nn.softmax(jnp.einsum('nd,de->ne', xf, w_router).astype(jnp.float32), axis=-1)
    g, idx = lax.top_k(aff.T, cap)
    xe = jnp.take(xf, idx, axis=0)
    h = jax.nn.silu(jnp.einsum('ecd,edf->ecf', xe, w1)) * jnp.einsum('ecd,edf->ecf', xe, w3)
    ye = jnp.einsum('ecf,efd->ecd', h, w2) * g[..., None].astype(x.dtype)
    out = jnp.zeros_like(xf).at[idx.reshape(-1)].add(ye.reshape(-1, D).astype(x.dtype))
    return out.reshape(b, L, D)


def trunk(x, p):
    b = x.shape[0]
    meta = jnp.broadcast_to(p['meta'][None].astype(x.dtype), (b, N_META, D_MODEL))
    h = jnp.concatenate([meta, x], axis=1)
    for layer in range(DEPTH):
        if layer % 2 == 0:
            e = layer // 2
            lambda_init = 0.8 - 0.6 * math.exp(-0.3 * layer)
            mix = hybrid_mixer(h, p['w_in'][e], p['conv_w'][e], p['conv_b'][e], p['a_log_f'][e],
                               p['a_log_b'][e], p['dt_bias_f'][e], p['dt_bias_b'][e], p['d_skip'][e],
                               p['ssd_norm_w'][e], p['lam_q1'][e], p['lam_k1'][e], p['lam_q2'][e],
                               p['lam_k2'][e], p['subln_w'][e], p['rel_bias'], p['w_out'][e], lambda_init)
        else:
            o = layer // 2
            mix = pool_mixer(h, p['pool_w'][o], p['pool_b'][o], p['pool_scale'][o])
        h = layer_norm(ALPHA * h + mix, p['ln1_g'][layer], p['ln1_b'][layer])
        ffn = expert_choice_ffn(h, p['w_router'][layer], p['w1'][layer], p['w3'][layer], p['w2'][layer])
        h = layer_norm(ALPHA * h + ffn, p['ln2_g'][layer], p['ln2_b'][layer])
    return h[:, N_META:]


def _dt_bias(k):
    dt = jnp.exp(jax.random.uniform(k, (N_EVEN, SSD_HEADS), jnp.float32, math.log(1e-3), math.log(1e-1)))
    return dt + jnp.log(-jnp.expm1(-dt))


def setup_inputs(seed: int = 0) -> dict:
    key = jax.random.key(seed)
    ks = jax.random.split(key, 32)
    f32 = jnp.float32

    def nrm(k, shape, s):
        return jax.random.normal(k, shape, f32) * s

    return {
        'x_prompt': nrm(ks[0], (BATCH, SEQ, D_MODEL), 1.0),
        'x_sample': nrm(ks[1], (DEC_BATCH, DEC_SEQ, D_MODEL), 1.0),
        'meta': nrm(ks[2], (N_META, D_MODEL), 1.0),
        'w_in': nrm(ks[3], (N_EVEN, D_MODEL, IN_DIM), D_MODEL ** -0.5),
        'conv_w': nrm(ks[4], (N_EVEN, SSD_CONV, 1, SSD_CONV_CH), SSD_CONV ** -0.5),
        'conv_b': nrm(ks[5], (N_EVEN, SSD_CONV_CH), 0.02),
        'a_log_f': jnp.log(jax.random.uniform(ks[6], (N_EVEN, SSD_HEADS), f32, 1.0, 16.0)),
        'a_log_b': jnp.log(jax.random.uniform(ks[7], (N_EVEN, SSD_HEADS), f32, 1.0, 16.0)),
        'dt_bias_f': _dt_bias(ks[8]),
        'dt_bias_b': _dt_bias(ks[9]),
        'd_skip': 1.0 + nrm(ks[10], (N_EVEN, SSD_HEADS), 0.02),
        'ssd_norm_w': 1.0 + nrm(ks[11], (N_EVEN, SSD_INNER), 0.02),
        'lam_q1': nrm(ks[12], (N_EVEN, ATT_HEAD_DIM), 0.1),
        'lam_k1': nrm(ks[13], (N_EVEN, ATT_HEAD_DIM), 0.1),
        'lam_q2': nrm(ks[14], (N_EVEN, ATT_HEAD_DIM), 0.1),
        'lam_k2': nrm(ks[15], (N_EVEN, ATT_HEAD_DIM), 0.1),
        'subln_w': 1.0 + nrm(ks[16], (N_EVEN, ATT_V_DIM), 0.02),
        'rel_bias': nrm(ks[17], (REL_BUCKETS, ATT_HEADS), 0.5),
        'w_out': nrm(ks[18], (N_EVEN, MIX_DIM, D_MODEL), MIX_DIM ** -0.5 * BETA),
        'pool_w': nrm(ks[19], (N_ODD, len(POOL_WINDOWS), POOL_GROUP, POOL_GROUP), POOL_GROUP ** -0.5 * BETA),
        'pool_b': nrm(ks[20], (N_ODD, D_MODEL), 0.02),
        'pool_scale': 1.0 + nrm(ks[21], (N_ODD, D_MODEL), 0.02),
        'ln1_g': 1.0 + nrm(ks[22], (DEPTH, D_MODEL), 0.02),
        'ln1_b': nrm(ks[23], (DEPTH, D_MODEL), 0.02),
        'ln2_g': 1.0 + nrm(ks[24], (DEPTH, D_MODEL), 0.02),
        'ln2_b': nrm(ks[25], (DEPTH, D_MODEL), 0.02),
        'w_router': nrm(ks[26], (DEPTH, D_MODEL, N_EXPERTS), D_MODEL ** -0.5),
        'w1': nrm(ks[27], (DEPTH, N_EXPERTS, D_MODEL, EXPERT_FF), D_MODEL ** -0.5),
        'w3': nrm(ks[28], (DEPTH, N_EXPERTS, D_MODEL, EXPERT_FF), D_MODEL ** -0.5),
        'w2': nrm(ks[29], (DEPTH, N_EXPERTS, EXPERT_FF, D_MODEL), EXPERT_FF ** -0.5 * BETA),
    }


def reference(x_prompt, x_sample, meta, w_in, conv_w, conv_b, a_log_f, a_log_b, dt_bias_f, dt_bias_b,
              d_skip, ssd_norm_w, lam_q1, lam_k1, lam_q2, lam_k2, subln_w, rel_bias, w_out,
              pool_w, pool_b, pool_scale, ln1_g, ln1_b, ln2_g, ln2_b, w_router, w1, w3, w2):
    p = dict(meta=meta, w_in=w_in, conv_w=conv_w, conv_b=conv_b, a_log_f=a_log_f, a_log_b=a_log_b,
             dt_bias_f=dt_bias_f, dt_bias_b=dt_bias_b, d_skip=d_skip, ssd_norm_w=ssd_norm_w,
             lam_q1=lam_q1, lam_k1=lam_k1, lam_q2=lam_q2, lam_k2=lam_k2, subln_w=subln_w,
             rel_bias=rel_bias, w_out=w_out, pool_w=pool_w, pool_b=pool_b, pool_scale=pool_scale,
             ln1_g=ln1_g, ln1_b=ln1_b, ln2_g=ln2_g, ln2_b=ln2_b, w_router=w_router,
             w1=w1, w3=w3, w2=w2)
    y_prompt = trunk(x_prompt, p)
    y_sample = trunk(x_sample, p)
    return (y_prompt, y_sample)
```

```python
import functools
import math

import jax
import jax.numpy as jnp
from jax import lax
from jax.experimental import pallas as pl
from jax.experimental.pallas import tpu as pltpu

D_MODEL = 2048
DEPTH = 2
N_META = 16
SSD_HEADS = 16
SSD_HEAD_DIM = 64
SSD_INNER = SSD_HEADS * SSD_HEAD_DIM
SSD_GROUPS = 4
SSD_STATE = 128
SSD_CONV = 5
SSD_CHUNK = 128
SSD_CONV_CH = SSD_INNER + 2 * SSD_GROUPS * SSD_STATE
ATT_HEADS = 8
ATT_HEAD_DIM = 64
ATT_V_DIM = 2 * ATT_HEAD_DIM
ATT_QK = ATT_HEADS * 2 * ATT_HEAD_DIM
ATT_INNER = ATT_HEADS * ATT_V_DIM
REL_BUCKETS = 32
POOL_WINDOWS = (2, 4, 8, 16)
POOL_GROUP = D_MODEL // len(POOL_WINDOWS)
N_EXPERTS = 16
EXPERT_FF = 2048
CAPACITY_FACTOR = 2
ALPHA = (2 * DEPTH) ** 0.25
LN_EPS = 1e-5

LANES = 128
BF16_ROWS = 16
SEQ_ALIGN = 256
TOK_TILE = 256
ROW_TILE = 512
DMA_ROWS = 32
FFN_ROWS = 704
FFN_COLS = 512
VMEM_LIMIT = 56 * 1024 * 1024

_BUCKET_STARTS = (1, 2, 3, 4, 5, 6, 7, 8, 12, 16, 23, 32, 46, 64, 91)
_COL_Z, _COL_XBC, _COL_K, _COL_DT, _PROJ_COLS = 0, 1024, 3072, 4096, 4608


def _cparams(*sem):
    return pltpu.CompilerParams(dimension_semantics=sem, vmem_limit_bytes=VMEM_LIMIT)


def _sigmoid(x):
    return 1.0 / (1.0 + jnp.exp(-x))


def _softplus(x):
    return jnp.maximum(x, 0.0) + jnp.log(1.0 + jnp.exp(-jnp.abs(x)))


def _split3(x):
    p1 = x.astype(jnp.bfloat16)
    r1 = x - p1.astype(jnp.float32)
    p2 = r1.astype(jnp.bfloat16)
    p3 = (r1 - p2.astype(jnp.float32)).astype(jnp.bfloat16)
    return p1, p2, p3


def _mm_nn_kernel(x_ref, w_ref, o_ref):
    o_ref[...] = jnp.dot(x_ref[...], w_ref[...], preferred_element_type=jnp.float32).astype(o_ref.dtype)


def mm_nn(x, w, out_dtype, tm, tn):
    n, k = x.shape
    nn = w.shape[1]
    return pl.pallas_call(
        _mm_nn_kernel,
        grid=(n // tm, nn // tn),
        in_specs=[pl.BlockSpec((tm, k), lambda i, j: (i, 0)), pl.BlockSpec((k, tn), lambda i, j: (0, j))],
        out_specs=pl.BlockSpec((tm, tn), lambda i, j: (i, j)),
        out_shape=jax.ShapeDtypeStruct((n, nn), out_dtype),
        compiler_params=_cparams("parallel", "arbitrary"),
        name="mm_nn",
    )(x, w)


def _mm_nt_kernel(w_ref, x_ref, o_ref):
    r = lax.dot_general(w_ref[...], x_ref[...], (((1,), (1,)), ((), ())), preferred_element_type=jnp.float32)
    for s in range(o_ref.shape[0]):
        o_ref[s] = r[:, s * SEQ_ALIGN:(s + 1) * SEQ_ALIGN].astype(o_ref.dtype)


def mm_nt_blocked(wt, x, out_dtype, tm, tn):
    n, k = x.shape
    nn = wt.shape[0]
    per = tm // SEQ_ALIGN
    return pl.pallas_call(
        _mm_nt_kernel,
        grid=(n // tm, nn // tn),
        in_specs=[pl.BlockSpec((tn, k), lambda i, j: (j, 0)), pl.BlockSpec((tm, k), lambda i, j: (i, 0))],
        out_specs=pl.BlockSpec((per, tn, SEQ_ALIGN), lambda i, j: (i, j, 0)),
        out_shape=jax.ShapeDtypeStruct((n // SEQ_ALIGN, nn, SEQ_ALIGN), out_dtype),
        compiler_params=_cparams("parallel", "arbitrary"),
        name="mm_nt",
    )(wt, x)


def _conv_kernel(x_ref, w_ref, b_ref, o_ref, pad_ref, *, rows):
    lp = x_ref.shape[1]
    halo = 8
    pad_ref[0:halo, :] = jnp.zeros((halo, LANES), jnp.float32)
    pad_ref[lp + halo:lp + 2 * halo, :] = jnp.zeros((halo, LANES), jnp.float32)
    pad_ref[halo:lp + halo, :] = x_ref[0]
    for t0 in range(0, lp, rows):
        acc = jnp.broadcast_to(b_ref[...], (rows, LANES))
        for k in range(SSD_CONV):
            off = t0 + halo - SSD_CONV // 2 + k
            acc = acc + pad_ref[off:off + rows, :] * w_ref[k:k + 1, :]
        o_ref[0, t0:t0 + rows, :] = acc * _sigmoid(acc)


def conv_silu(proj3, conv_w, conv_b):
    b, lp, _ = proj3.shape
    c0 = _COL_XBC // LANES
    return pl.pallas_call(
        functools.partial(_conv_kernel, rows=SEQ_ALIGN),
        grid=(b, SSD_CONV_CH // LANES),
        in_specs=[
            pl.BlockSpec((1, lp, LANES), lambda i, j: (i, 0, c0 + j)),
            pl.BlockSpec((SSD_CONV, LANES), lambda i, j: (0, j)),
            pl.BlockSpec((1, LANES), lambda i, j: (0, j)),
        ],
        out_specs=pl.BlockSpec((1, lp, LANES), lambda i, j: (i, 0, j)),
        out_shape=jax.ShapeDtypeStruct((b, lp, SSD_CONV_CH), jnp.float32),
        scratch_shapes=[pltpu.VMEM((lp + 16, LANES), jnp.float32)],
        compiler_params=_cparams("parallel", "parallel"),
        name="conv_silu",
    )(proj3, conv_w, conv_b)


def _ssd_kernel(x_ref, b_ref, c_ref, dt_ref, par_ref, y_ref, h_ref, *, seq_len, reverse):
    cs = SSD_CHUNK
    step = pl.program_id(1)
    chunk = (pl.num_programs(1) - 1 - step) if reverse else step
    lane0 = SSD_HEADS if reverse else 0

    @pl.when(step == 0)
    def _():
        h_ref[...] = jnp.zeros(h_ref.shape, jnp.float32)

    sub = lax.broadcasted_iota(jnp.int32, (cs, cs), 0)
    lan = lax.broadcasted_iota(jnp.int32, (cs, cs), 1)
    valid = (chunk * cs + sub[:, 0:1]) < seq_len
    dt = jnp.where(valid, _softplus(dt_ref[...] + par_ref[0:1, :]), 0.0)
    a = dt * -jnp.exp(par_ref[1:2, :])
    a_t = a.T[0:2 * SSD_HEADS, :]
    if reverse:
        incl, rest = (sub >= lan), (sub < lan)
    else:
        incl, rest = (sub <= lan), (sub > lan)
    tri = jnp.concatenate([incl.astype(jnp.bfloat16), rest.astype(jnp.bfloat16),
                           jnp.ones((cs, cs), jnp.bfloat16)], axis=1)
    pieces = jnp.concatenate(_split3(a_t), axis=0)
    sums = jnp.dot(pieces, tri, preferred_element_type=jnp.float32)
    nh2 = 2 * SSD_HEADS
    sums = sums[0:nh2] + sums[nh2:2 * nh2] + sums[2 * nh2:3 * nh2]
    cum_row, rest_row, tot_row = sums[:, 0:cs], sums[:, cs:2 * cs], sums[:, 2 * cs:3 * cs]
    cum_col = jnp.concatenate([cum_row, jnp.zeros((cs - nh2, cs), jnp.float32)], axis=0).T
    mask = (sub <= lan) if reverse else (sub >= lan)
    left = lan[0:1, :] < SSD_HEAD_DIM

    for g in range(SSD_GROUPS):
        bg = b_ref[:, g * SSD_STATE:(g + 1) * SSD_STATE]
        cg = c_ref[:, g * SSD_STATE:(g + 1) * SSD_STATE]
        cb = lax.dot_general(cg.astype(jnp.bfloat16), bg.astype(jnp.bfloat16), (((1,), (1,)), ((), ())),
                             preferred_element_type=jnp.float32)
        bg_t = bg.T
        for pp in range(2):
            p = 2 * g + pp
            r0, r1 = lane0 + 2 * p, lane0 + 2 * p + 1
            dtp = jnp.where(left, dt[:, r0:r0 + 1], dt[:, r1:r1 + 1])
            xs = (x_ref[:, p * LANES:(p + 1) * LANES] * dtp).astype(jnp.bfloat16)
            h_t = h_ref[p]
            rhs = jnp.concatenate([xs, h_t.astype(jnp.bfloat16)], axis=0)
            ys, ss = [], []
            for r in (r0, r1):
                cc = cum_col[:, r:r + 1]
                dec = jnp.exp(jnp.where(mask, cc - cum_row[r:r + 1, :], -jnp.inf))
                lhs = jnp.concatenate([(cb * dec).astype(jnp.bfloat16),
                                       (cg * jnp.exp(cc)).astype(jnp.bfloat16)], axis=1)
                ys.append(jnp.dot(lhs, rhs, preferred_element_type=jnp.float32))
                bw = (bg_t * jnp.exp(rest_row[r:r + 1, :])).astype(jnp.bfloat16)
                ss.append(jnp.dot(bw, xs, preferred_element_type=jnp.float32))
            y_ref[:, p * LANES:(p + 1) * LANES] = jnp.where(left, ys[0], ys[1])
            decay = jnp.where(left, jnp.exp(tot_row[r0:r0 + 1, :]), jnp.exp(tot_row[r1:r1 + 1, :]))
            h_ref[p] = decay * h_t + jnp.where(left, ss[0], ss[1])


def ssd_scan(xbc, proj, par, b, lp, seq_len, reverse):
    n = xbc.shape[0]
    nc = lp // SSD_CHUNK
    cdt = _COL_DT // LANES

    def row(i, c):
        return i * nc + ((nc - 1 - c) if reverse else c)

    return pl.pallas_call(
        functools.partial(_ssd_kernel, seq_len=seq_len, reverse=reverse),
        grid=(b, nc),
        in_specs=[
            pl.BlockSpec((SSD_CHUNK, SSD_INNER), lambda i, c: (row(i, c), 0)),
            pl.BlockSpec((SSD_CHUNK, SSD_GROUPS * SSD_STATE), lambda i, c: (row(i, c), 2)),
            pl.BlockSpec((SSD_CHUNK, SSD_GROUPS * SSD_STATE), lambda i, c: (row(i, c), 3)),
            pl.BlockSpec((SSD_CHUNK, LANES), lambda i, c: (row(i, c), cdt)),
            pl.BlockSpec((8, LANES), lambda i, c: (0, 0)),
        ],
        out_specs=pl.BlockSpec((SSD_CHUNK, SSD_INNER), lambda i, c: (row(i, c), 0)),
        out_shape=jax.ShapeDtypeStruct((n, SSD_INNER), jnp.float32),
        scratch_shapes=[pltpu.VMEM((SSD_HEADS // 2, SSD_STATE, LANES), jnp.float32)],
        compiler_params=_cparams("parallel", "arbitrary"),
        name="ssd_bwd" if reverse else "ssd_fwd",
    )(xbc, xbc, xbc, proj, par)


def _gate_kernel(yf_ref, yb_ref, x_ref, z_ref, dsk_ref, nw_ref, o_ref):
    z = z_ref[...]
    y = (yf_ref[...] + yb_ref[...] + dsk_ref[...] * x_ref[...]) * (z * _sigmoid(z))
    gw = SSD_INNER // SSD_GROUPS
    for g in range(SSD_GROUPS):
        seg = y[:, g * gw:(g + 1) * gw]
        ms = jnp.mean(seg * seg, axis=-1, keepdims=True)
        o_ref[:, g * gw:(g + 1) * gw] = (seg * lax.rsqrt(ms + LN_EPS) * nw_ref[:, g * gw:(g + 1) * gw]).astype(o_ref.dtype)


def ssd_gate(yf, yb, xbc, proj, dsk, nw, tm):
    n = yf.shape[0]
    blk = lambda i: (i, 0)
    return pl.pallas_call(
        _gate_kernel,
        grid=(n // tm,),
        in_specs=[pl.BlockSpec((tm, SSD_INNER), blk)] * 4 + [pl.BlockSpec((1, SSD_INNER), lambda i: (0, 0))] * 2,
        out_specs=pl.BlockSpec((tm, SSD_INNER), blk),
        out_shape=jax.ShapeDtypeStruct((n, SSD_INNER), jnp.bfloat16),
        compiler_params=_cparams("parallel"),
        name="ssd_gate",
    )(yf, yb, xbc, proj, dsk, nw)


def _bias_kernel(rb_ref, o_ref):
    h = pl.program_id(0)
    d = pl.program_id(1) - 2
    t = SEQ_ALIGN
    kk = lax.broadcasted_iota(jnp.int32, (t, t), 0)
    qq = lax.broadcasted_iota(jnp.int32, (t, t), 1)
    rel = d * t + kk - qq
    n = jnp.abs(rel)
    half = REL_BUCKETS // 2
    vneg = jnp.full((t, t), rb_ref[0, h], jnp.float32)
    vpos = jnp.full((t, t), rb_ref[half, h], jnp.float32)
    for j, start in enumerate(_BUCKET_STARTS, start=1):
        ge = n >= start
        vneg = jnp.where(ge, rb_ref[j, h], vneg)
        vpos = jnp.where(ge, rb_ref[half + j, h], vpos)
    o_ref[0, 0] = jnp.where(rel > 0, vpos, vneg)


def bias_tiles(rel_bias):
    return pl.pallas_call(
        _bias_kernel,
        grid=(ATT_HEADS, 5),
        in_specs=[pl.BlockSpec(memory_space=pltpu.SMEM)],
        out_specs=pl.BlockSpec((1, 1, SEQ_ALIGN, SEQ_ALIGN), lambda h, d: (h, d, 0, 0)),
        out_shape=jax.ShapeDtypeStruct((ATT_HEADS, 5, SEQ_ALIGN, SEQ_ALIGN), jnp.float32),
        compiler_params=_cparams("parallel", "parallel"),
        name="bias_tiles",
    )(rel_bias)


def _attn_kernel(q_ref, v_ref, k_ref, bias_ref, lam_ref, sub_ref, o_ref, acc1_ref, acc2_ref, *, seq_len, lambda_init):
    t = SEQ_ALIGN
    qb = pl.program_id(2)
    q_t = q_ref[0]
    sub = lax.broadcasted_iota(jnp.int32, q_t.shape, 0)
    zero = jnp.zeros_like(q_t)
    q1 = jnp.where(sub < ATT_HEAD_DIM, q_t, zero)
    q2 = jnp.where(sub >= ATT_HEAD_DIM, q_t, zero)
    acc1_ref[...] = jnp.zeros(acc1_ref.shape, jnp.float32)
    acc2_ref[...] = jnp.zeros(acc2_ref.shape, jnp.float32)

    def block(kb, carry, masked):
        kblk = k_ref[0, pl.ds(pl.multiple_of(kb * t, t), t), :].astype(jnp.bfloat16)
        bias = bias_ref[0, jnp.clip(kb - qb, -2, 2) + 2]
        vblk = v_ref[kb]
        out = []
        for q_m, acc_ref, (m, l) in ((q1, acc1_ref, carry[0:2]), (q2, acc2_ref, carry[2:4])):
            s = jnp.dot(kblk, q_m, preferred_element_type=jnp.float32) + bias
            if masked:
                kidx = kb * t + lax.broadcasted_iota(jnp.int32, (t, 1), 0)
                s = jnp.where(kidx < seq_len, s, -jnp.inf)
            m_new = jnp.maximum(m, jnp.max(s, axis=0, keepdims=True))
            alpha = jnp.exp(m - m_new)
            p = jnp.exp(s - m_new)
            l = alpha * l + jnp.sum(p, axis=0, keepdims=True)
            acc_ref[...] = alpha * acc_ref[...] + jnp.dot(vblk, p.astype(jnp.bfloat16),
                                                          preferred_element_type=jnp.float32)
            out += [m_new, l]
        return tuple(out)

    neg = jnp.full((1, t), -jnp.inf, jnp.float32)
    zer = jnp.zeros((1, t), jnp.float32)
    n_full = seq_len // t
    carry = lax.fori_loop(0, n_full, lambda kb, c: block(kb, c, False), (neg, zer, neg, zer))
    if seq_len % t:
        carry = block(n_full, carry, True)
    _, l1, _, l2 = carry
    lv = lam_ref[...]
    lam = (jnp.exp(jnp.sum(lv[0:1] * lv[1:2], axis=-1, keepdims=True))
           - jnp.exp(jnp.sum(lv[2:3] * lv[3:4], axis=-1, keepdims=True)) + lambda_init)
    o = acc1_ref[...] * (1.0 / l1) - lam * (acc2_ref[...] * (1.0 / l2))
    ms = jnp.mean(o * o, axis=0, keepdims=True)
    o = o * lax.rsqrt(ms + LN_EPS) * sub_ref[...] * (1.0 - lambda_init)
    o_ref[...] = o.T.astype(o_ref.dtype)


def diff_attention(qv, proj3, bias5, lam4, subw, b, lp, seq_len, lambda_init):
    nq = lp // SEQ_ALIGN
    ck = _COL_K // LANES
    return pl.pallas_call(
        functools.partial(_attn_kernel, seq_len=seq_len, lambda_init=lambda_init),
        grid=(b, ATT_HEADS, nq),
        in_specs=[
            pl.BlockSpec((1, ATT_V_DIM, SEQ_ALIGN), lambda i, h, q: (i * nq + q, h, 0)),
            pl.BlockSpec((nq, ATT_V_DIM, SEQ_ALIGN), lambda i, h, q: (i, ATT_HEADS + h, 0)),
            pl.BlockSpec((1, lp, LANES), lambda i, h, q: (i, 0, ck + h)),
            pl.BlockSpec((1, 5, SEQ_ALIGN, SEQ_ALIGN), lambda i, h, q: (h, 0, 0, 0)),
            pl.BlockSpec((8, LANES), lambda i, h, q: (0, 0)),
            pl.BlockSpec((ATT_V_DIM, SEQ_ALIGN), lambda i, h, q: (0, 0)),
        ],
        out_specs=pl.BlockSpec((SEQ_ALIGN, ATT_V_DIM), lambda i, h, q: (i * nq + q, h)),
        out_shape=jax.ShapeDtypeStruct((b * lp, ATT_INNER), jnp.bfloat16),
        scratch_shapes=[pltpu.VMEM((ATT_V_DIM, SEQ_ALIGN), jnp.float32)] * 2,
        compiler_params=_cparams("parallel", "parallel", "arbitrary"),
        name="diff_attn",
    )(qv, qv, proj3, bias5, lam4, subw)


def _norm_route(t, g_ref, b_ref, wr_ref, tile_row0, lp, seq_len, h_ref, hb_ref, aff_ref, afft_ref):
    rows = t.shape[0]
    mu = jnp.mean(t, axis=-1, keepdims=True)
    tc = t - mu
    var = jnp.mean(tc * tc, axis=-1, keepdims=True)
    y = tc * lax.rsqrt(var + LN_EPS) * g_ref[...] + b_ref[...]
    pos = tile_row0 % lp + lax.broadcasted_iota(jnp.int32, (rows, 1), 0)
    valid = pos < seq_len
    y = jnp.where(valid, y, 0.0)
    h_ref[...] = y
    yb = y.astype(jnp.bfloat16)
    hb_ref[...] = yb
    if aff_ref is not None:
        logits = jnp.dot(yb, wr_ref[...], preferred_element_type=jnp.float32)
        lane = lax.broadcasted_iota(jnp.int32, logits.shape, 1)
        is_e = lane < N_EXPERTS
        logits = jnp.where(is_e, logits, -jnp.inf)
        e = jnp.exp(logits - jnp.max(logits, axis=-1, keepdims=True))
        aff = e / jnp.sum(e, axis=-1, keepdims=True)
        aff = jnp.where(valid, aff, jnp.where(is_e, -1.0, 0.0))
        aff_ref[...] = aff
        afft_ref[0] = aff.T[0:N_EXPERTS, :]


def _outproj_kernel(ys_ref, ya_ref, w_ref, h_ref, g_ref, b_ref, wr_ref, ho_ref, hb_ref, aff_ref, afft_ref, *, lp, seq_len):
    mix = (jnp.dot(ys_ref[...], w_ref[0:SSD_INNER, :], preferred_element_type=jnp.float32)
           + jnp.dot(ya_ref[...], w_ref[SSD_INNER:, :], preferred_element_type=jnp.float32))
    t = ALPHA * h_ref[...] + mix
    _norm_route(t, g_ref, b_ref, wr_ref, pl.program_id(0) * h_ref.shape[0], lp, seq_len, ho_ref, hb_ref, aff_ref, afft_ref)


def _poolproj_kernel(d_ref, w_ref, pb_ref, ps_ref, h_ref, g_ref, b_ref, wr_ref, ho_ref, hb_ref, aff_ref, afft_ref, *, lp, seq_len):
    outs = []
    for gi in range(len(POOL_WINDOWS)):
        outs.append(jnp.dot(d_ref[:, gi * POOL_GROUP:(gi + 1) * POOL_GROUP], w_ref[gi], preferred_element_type=jnp.float32))
    mix = (jnp.concatenate(outs, axis=-1) + pb_ref[...]) * ps_ref[...]
    t = ALPHA * h_ref[...] + mix
    _norm_route(t, g_ref, b_ref, wr_ref, pl.program_id(0) * h_ref.shape[0], lp, seq_len, ho_ref, hb_ref, aff_ref, afft_ref)


def _mix_outputs(n, tm):
    row = lambda i: (i, 0)
    specs = [pl.BlockSpec((tm, D_MODEL), row), pl.BlockSpec((tm, D_MODEL), row), pl.BlockSpec((tm, LANES), row),
             pl.BlockSpec((1, N_EXPERTS, tm), lambda i: (i, 0, 0))]
    shapes = [jax.ShapeDtypeStruct((n, D_MODEL), jnp.float32), jax.ShapeDtypeStruct((n, D_MODEL), jnp.bfloat16),
              jax.ShapeDtypeStruct((n, LANES), jnp.float32), jax.ShapeDtypeStruct((n // tm, N_EXPERTS, tm), jnp.float32)]
    return specs, shapes


def out_proj_norm_route(y_ssd, y_att, w_out, h, g, bb, w_r, lp, seq_len):
    n = h.shape[0]
    tm = TOK_TILE
    row = lambda i: (i, 0)
    fix = lambda i: (0, 0)
    specs, shapes = _mix_outputs(n, tm)
    return pl.pallas_call(
        functools.partial(_outproj_kernel, lp=lp, seq_len=seq_len),
        grid=(n // tm,),
        in_specs=[pl.BlockSpec((tm, SSD_INNER), row), pl.BlockSpec((tm, ATT_INNER), row),
                  pl.BlockSpec((D_MODEL, D_MODEL), fix), pl.BlockSpec((tm, D_MODEL), row),
                  pl.BlockSpec((1, D_MODEL), fix), pl.BlockSpec((1, D_MODEL), fix), pl.BlockSpec((D_MODEL, LANES), fix)],
        out_specs=specs, out_shape=shapes,
        compiler_params=_cparams("parallel"),
        name="outproj_ln_route",
    )(y_ssd, y_att, w_out, h, g, bb, w_r)


def pool_proj_norm_route(d, pool_w, pool_b, pool_scale, h, g, bb, w_r, lp, seq_len):
    n = h.shape[0]
    tm = TOK_TILE
    row = lambda i: (i, 0)
    fix = lambda i: (0, 0)
    specs, shapes = _mix_outputs(n, tm)
    return pl.pallas_call(
        functools.partial(_poolproj_kernel, lp=lp, seq_len=seq_len),
        grid=(n // tm,),
        in_specs=[pl.BlockSpec((tm, D_MODEL), row),
                  pl.BlockSpec((len(POOL_WINDOWS), POOL_GROUP, POOL_GROUP), lambda i: (0, 0, 0)),
                  pl.BlockSpec((1, D_MODEL), fix), pl.BlockSpec((1, D_MODEL), fix), pl.BlockSpec((tm, D_MODEL), row),
                  pl.BlockSpec((1, D_MODEL), fix), pl.BlockSpec((1, D_MODEL), fix), pl.BlockSpec((D_MODEL, LANES), fix)],
        out_specs=specs, out_shape=shapes,
        compiler_params=_cparams("parallel"),
        name="poolproj_ln_route",
    )(d, pool_w, pool_b, pool_scale, h, g, bb, w_r)


def _pool_kernel(x_ref, o_ref, s0_ref, s1_ref, *, seq_len):
    lp = x_ref.shape[1]
    halo = 16
    gi = pl.program_id(1) // (POOL_GROUP // LANES)
    zeros = jnp.zeros((halo, LANES), jnp.float32)
    for ref in (s0_ref, s1_ref):
        ref[0:halo, :] = zeros
        ref[lp + halo:lp + 2 * halo, :] = zeros
    s0_ref[halo:lp + halo, :] = x_ref[0]
    x = x_ref[0]
    t = lax.broadcasted_iota(jnp.int32, (lp, 1), 0)
    src, dst = s0_ref, s1_ref
    d = jnp.zeros((lp, LANES), jnp.float32)
    width = 1
    for wi, w in enumerate(POOL_WINDOWS):
        while width < w:
            dst[halo:lp + halo, :] = src[halo:lp + halo, :] + src[halo + width:lp + halo + width, :]
            dst[0:halo, :] = src[0:halo, :] + src[width:halo + width, :]
            src, dst = dst, src
            width *= 2
        win = src[halo - w // 2:lp + halo - w // 2, :]
        cnt = jnp.minimum(t + w // 2 - 1, seq_len - 1) - jnp.maximum(t - w // 2, 0) + 1
        mean = win / cnt.astype(jnp.float32)
        d = jnp.where(gi == wi, mean - x, d)
    o_ref[0] = jnp.where(t < seq_len, d, 0.0).astype(o_ref.dtype)


def pool_diff(h3, seq_len):
    b, lp, _ = h3.shape
    return pl.pallas_call(
        functools.partial(_pool_kernel, seq_len=seq_len),
        grid=(b, D_MODEL // LANES),
        in_specs=[pl.BlockSpec((1, lp, LANES), lambda i, j: (i, 0, j))],
        out_specs=pl.BlockSpec((1, lp, LANES), lambda i, j: (i, 0, j)),
        out_shape=jax.ShapeDtypeStruct((b, lp, D_MODEL), jnp.bfloat16),
        scratch_shapes=[pltpu.VMEM((lp + 32, LANES), jnp.float32)] * 2,
        compiler_params=_cparams("parallel", "parallel"),
        name="pool_diff",
    )(h3)


def _route_kernel(aff_ref, pos_ref, ptok_ref, start_ref, *, cap):
    nt = aff_ref.shape[0]
    t = TOK_TILE

    def bits_of(v):
        return lax.bitcast_convert_type(v, jnp.int32)

    def count(mask):
        return jnp.sum(jnp.sum(jnp.where(mask, 1.0, 0.0), axis=0), axis=1, keepdims=True)

    def search(i, thr):
        cand = thr | jnp.left_shift(jnp.int32(1), 30 - i)
        cnt = count(bits_of(aff_ref[...]) >= cand[None])
        return jnp.where(cnt >= cap, cand, thr)

    thr = lax.fori_loop(0, 31, search, jnp.zeros((N_EXPERTS, 1), jnp.int32))
    need = cap - count(bits_of(aff_ref[...]) > thr[None])

    before = (lax.broadcasted_iota(jnp.int32, (t, t), 0) < lax.broadcasted_iota(jnp.int32, (t, t), 1)).astype(jnp.bfloat16)
    ones8 = jnp.ones((8, t), jnp.bfloat16)
    fill = jnp.full((LANES - N_EXPERTS, t), -1.0, jnp.float32)
    zpad = jnp.zeros((LANES - N_EXPERTS, t), jnp.float32)

    def tile(ti, carry):
        c_tie, c_sel, c_lane = carry
        b = bits_of(aff_ref[ti])
        tie = jnp.where(b == thr, 1.0, 0.0)
        rank = jnp.dot(tie.astype(jnp.bfloat16), before, preferred_element_type=jnp.float32) + c_tie
        sel = jnp.where(b > thr, 1.0, jnp.where(rank < need, tie, 0.0))
        slot = jnp.dot(sel.astype(jnp.bfloat16), before, preferred_element_type=jnp.float32) + c_sel
        posv = jnp.where(sel > 0.0, slot, -1.0)
        pos_ref[ti] = posv.astype(jnp.int32)
        ptok_ref[pl.ds(pl.multiple_of(ti * t, t), t), :] = jnp.concatenate([posv, fill], axis=0).T.astype(jnp.int32)
        start_ref[ti] = c_lane.astype(jnp.int32)
        selp = jnp.concatenate([sel, zpad], axis=0).astype(jnp.bfloat16)
        cnt_lane = lax.dot_general(ones8, selp, (((1,), (1,)), ((), ())), preferred_element_type=jnp.float32)
        return (c_tie + jnp.sum(tie, axis=1, keepdims=True), c_sel + jnp.sum(sel, axis=1, keepdims=True),
                c_lane + cnt_lane[0:1])

    zc = jnp.zeros((N_EXPERTS, 1), jnp.float32)
    _, _, c_lane = lax.fori_loop(0, nt, tile, (zc, zc, jnp.zeros((1, LANES), jnp.float32)))
    start_ref[nt] = c_lane.astype(jnp.int32)


def route(afft, cap):
    nt = afft.shape[0]
    n = nt * TOK_TILE
    return pl.pallas_call(
        functools.partial(_route_kernel, cap=cap),
        out_shape=[jax.ShapeDtypeStruct((nt, N_EXPERTS, TOK_TILE), jnp.int32),
                   jax.ShapeDtypeStruct((n, LANES), jnp.int32),
                   jax.ShapeDtypeStruct((nt + 1, 1, LANES), jnp.int32)],
        compiler_params=pltpu.CompilerParams(vmem_limit_bytes=VMEM_LIMIT),
        name="route",
    )(afft)


def _tile_plan(st_ref, ti):
    plan = []
    off = jnp.int32(0)
    for e in range(N_EXPERTS):
        s = st_ref[ti * N_EXPERTS + e]
        end = st_ref[(ti + 1) * N_EXPERTS + e]
        s16 = s - (s & (BF16_ROWS - 1))
        nch = jnp.where(end > s, lax.shift_right_logical(end - s16 + DMA_ROWS - 1, DMA_ROWS.bit_length() - 1), 0)
        plan.append((s16, nch, off, end))
        off = off + nch * DMA_ROWS
    return plan, off


def _stage_max():
    per = -(-(TOK_TILE + BF16_ROWS - 1) // DMA_ROWS) * DMA_ROWS
    return N_EXPERTS * per + SEQ_ALIGN


def _dispatch_kernel(st_ref, x_ref, pos_ref, aff_ref, xe_ref, ge_ref, sx_ref, sg_ref, cx_ref, cg_ref, n_ref, sem_ref,
                     *, cap, cap_pad):
    ti = pl.program_id(0)
    nt = pl.num_programs(0)
    t = TOK_TILE

    def chunk_copies(e, s16, off, j):
        src = pl.multiple_of(off + j * DMA_ROWS, DMA_ROWS)
        dst = pl.multiple_of(s16 + j * DMA_ROWS, BF16_ROWS)
        return (pltpu.make_async_copy(sx_ref.at[pl.ds(src, DMA_ROWS)], xe_ref.at[e, pl.ds(dst, DMA_ROWS)], sem_ref.at[0]),
                pltpu.make_async_copy(sg_ref.at[pl.ds(src, DMA_ROWS)], ge_ref.at[e, pl.ds(dst, DMA_ROWS)], sem_ref.at[1]))

    def wait_chunks(count):
        def body(_, c):
            for cp in chunk_copies(0, 0, 0, 0):
                cp.wait()
            return c
        lax.fori_loop(0, count, body, 0)

    tail0 = (cap // DMA_ROWS) * DMA_ROWS
    n_tail = (cap_pad - tail0) // DMA_ROWS

    @pl.when(ti == 0)
    def _():
        cx_ref[...] = jnp.zeros(cx_ref.shape, cx_ref.dtype)
        cg_ref[...] = jnp.zeros(cg_ref.shape, cg_ref.dtype)
        sx_ref[0:DMA_ROWS, :] = jnp.zeros((DMA_ROWS, D_MODEL), sx_ref.dtype)
        sg_ref[0:DMA_ROWS, :] = jnp.zeros((DMA_ROWS, LANES), sg_ref.dtype)
        for e in range(N_EXPERTS):
            for j in range(n_tail):
                pltpu.make_async_copy(sx_ref.at[pl.ds(0, DMA_ROWS)], xe_ref.at[e, pl.ds(tail0 + j * DMA_ROWS, DMA_ROWS)], sem_ref.at[0]).start()
                pltpu.make_async_copy(sg_ref.at[pl.ds(0, DMA_ROWS)], ge_ref.at[e, pl.ds(tail0 + j * DMA_ROWS, DMA_ROWS)], sem_ref.at[1]).start()
        wait_chunks(N_EXPERTS * n_tail)
        n_ref[0] = 0

    plan, k_tot = _tile_plan(st_ref, ti)

    sub = lax.broadcasted_iota(jnp.int32, (N_EXPERTS, 1), 0)
    delta = jnp.zeros((N_EXPERTS, 1), jnp.int32)
    for e, (s16, nch, off, end) in enumerate(plan):
        delta = jnp.where(sub == e, off - s16, delta)
    pos = pos_ref[0]
    srow = jnp.where(pos >= 0, pos + delta, -1)

    aff = aff_ref[...]
    a1, a2, a3 = _split3(aff)
    gate3 = (a1.astype(jnp.float32) + pltpu.roll(a2.astype(jnp.float32), N_EXPERTS, 1)
             + pltpu.roll(a3.astype(jnp.float32), 2 * N_EXPERTS, 1)).astype(jnp.bfloat16)
    xt = x_ref[...]

    @pl.when(ti > 0)
    def _():
        wait_chunks(n_ref[0])

    def permute(kc, c):
        k0 = pl.multiple_of(kc * SEQ_ALIGN, SEQ_ALIGN)
        k = k0 + lax.broadcasted_iota(jnp.int32, (SEQ_ALIGN, 1), 0)
        onehot = jnp.zeros((SEQ_ALIGN, t), jnp.float32)
        for e in range(N_EXPERTS):
            onehot = jnp.where(srow[e:e + 1, :] == k, 1.0, onehot)
        onehot = onehot.astype(jnp.bfloat16)
        sx_ref[pl.ds(k0, SEQ_ALIGN), :] = jnp.dot(onehot, xt, preferred_element_type=jnp.float32).astype(sx_ref.dtype)
        sg_ref[pl.ds(k0, SEQ_ALIGN), :] = jnp.dot(onehot, gate3, preferred_element_type=jnp.float32)
        return c

    lax.fori_loop(0, lax.shift_right_logical(k_tot + SEQ_ALIGN - 1, SEQ_ALIGN.bit_length() - 1), permute, 0)

    for e, (s16, nch, off, end) in enumerate(plan):
        @pl.when(nch > 0)
        def _(e=e, s16=s16, off=off, end=end):
            head = pl.ds(pl.multiple_of(off, BF16_ROWS), BF16_ROWS)
            sx_ref[head, :] = sx_ref[head, :] + cx_ref[e]
            sg_ref[head, :] = sg_ref[head, :] + cg_ref[e]
            part = end & (BF16_ROWS - 1)
            last = pl.ds(pl.multiple_of(off + (end - part) - s16, BF16_ROWS), BF16_ROWS)
            cx_ref[e] = jnp.where(part > 0, sx_ref[last, :], jnp.zeros((BF16_ROWS, D_MODEL), cx_ref.dtype))
            cg_ref[e] = jnp.where(part > 0, sg_ref[last, :], 0.0)

    total = jnp.int32(0)
    for e, (s16, nch, off, end) in enumerate(plan):
        def issue(j, c, e=e, s16=s16, off=off):
            for cp in chunk_copies(e, s16, off, j):
                cp.start()
            return c
        lax.fori_loop(0, nch, issue, 0)
        total = total + nch
    n_ref[0] = total

    @pl.when(ti == nt - 1)
    def _():
        wait_chunks(total)


def dispatch(starts, hb, pos, aff, cap, cap_pad):
    n = hb.shape[0]
    nt = n // TOK_TILE
    kmax = _stage_max()
    grid_spec = pltpu.PrefetchScalarGridSpec(
        num_scalar_prefetch=1,
        grid=(nt,),
        in_specs=[pl.BlockSpec((TOK_TILE, D_MODEL), lambda i, st: (i, 0)),
                  pl.BlockSpec((1, N_EXPERTS, TOK_TILE), lambda i, st: (i, 0, 0)),
                  pl.BlockSpec((TOK_TILE, LANES), lambda i, st: (i, 0))],
        out_specs=[pl.BlockSpec(memory_space=pl.ANY), pl.BlockSpec(memory_space=pl.ANY)],
        scratch_shapes=[pltpu.VMEM((kmax, D_MODEL), jnp.bfloat16), pltpu.VMEM((kmax, LANES), jnp.float32),
                        pltpu.VMEM((N_EXPERTS, BF16_ROWS, D_MODEL), jnp.bfloat16),
                        pltpu.VMEM((N_EXPERTS, BF16_ROWS, LANES), jnp.float32),
                        pltpu.SMEM((1,), jnp.int32), pltpu.SemaphoreType.DMA((2,))],
    )
    return pl.pallas_call(
        functools.partial(_dispatch_kernel, cap=cap, cap_pad=cap_pad),
        grid_spec=grid_spec,
        out_shape=[jax.ShapeDtypeStruct((N_EXPERTS, cap_pad, D_MODEL), jnp.bfloat16),
                   jax.ShapeDtypeStruct((N_EXPERTS, cap_pad, LANES), jnp.float32)],
        compiler_params=_cparams("arbitrary"),
        name="moe_dispatch",
    )(starts, hb, pos, aff)


def _ffn_kernel(x_ref, w1_ref, w3_ref, w2_ref, g_ref, o_ref, acc_ref, *, cap):
    e = pl.program_id(0)
    f = pl.program_id(2)
    x = x_ref[0]
    a = jnp.dot(x, w1_ref[0], preferred_element_type=jnp.float32)
    b = jnp.dot(x, w3_ref[0], preferred_element_type=jnp.float32)
    hid = (a * _sigmoid(a) * b).astype(jnp.bfloat16)
    part = jnp.dot(hid, w2_ref[0], preferred_element_type=jnp.float32)

    @pl.when(f == 0)
    def _():
        acc_ref[...] = part

    @pl.when(f > 0)
    def _():
        acc_ref[...] = acc_ref[...] + part

    @pl.when(f == pl.num_programs(2) - 1)
    def _():
        rows = o_ref.shape[1]
        gl = g_ref[0]
        lane = lax.broadcasted_iota(jnp.int32, gl.shape, 1)
        mine = ((lane & (N_EXPERTS - 1)) == e) & (lane < 3 * N_EXPERTS)
        gate = jnp.sum(jnp.where(mine, gl, 0.0), axis=-1, keepdims=True)
        slot = pl.program_id(1) * rows + lax.broadcasted_iota(jnp.int32, (rows, 1), 0)
        o_ref[0] = jnp.where(slot < cap, acc_ref[...] * gate, 0.0).astype(o_ref.dtype)


def expert_ffn(xe, ge, w1, w3, w2, cap):
    e, cap_pad, _ = xe.shape
    return pl.pallas_call(
        functools.partial(_ffn_kernel, cap=cap),
        grid=(e, cap_pad // FFN_ROWS, EXPERT_FF // FFN_COLS),
        in_specs=[pl.BlockSpec((1, FFN_ROWS, D_MODEL), lambda i, m, f: (i, m, 0)),
                  pl.BlockSpec((1, D_MODEL, FFN_COLS), lambda i, m, f: (i, 0, f)),
                  pl.BlockSpec((1, D_MODEL, FFN_COLS), lambda i, m, f: (i, 0, f)),
                  pl.BlockSpec((1, FFN_COLS, D_MODEL), lambda i, m, f: (i, f, 0)),
                  pl.BlockSpec((1, FFN_ROWS, LANES), lambda i, m, f: (i, m, 0))],
        out_specs=pl.BlockSpec((1, FFN_ROWS, D_MODEL), lambda i, m, f: (i, m, 0)),
        out_shape=jax.ShapeDtypeStruct((e, cap_pad, D_MODEL), jnp.bfloat16),
        scratch_shapes=[pltpu.VMEM((FFN_ROWS, D_MODEL), jnp.float32)],
        compiler_params=_cparams("parallel", "parallel", "arbitrary"),
        name="expert_ffn",
    )(xe, w1, w3, w2, ge)


def _combine_kernel(st_ref, ye_ref, ptok_ref, h_ref, g_ref, b_ref, ho_ref, hb_ref, sy_ref, acc_ref, sem_ref, *, lp, seq_len):
    ti = pl.program_id(0)
    t = TOK_TILE

    @pl.when(ti == 0)
    def _():
        sy_ref[...] = jnp.zeros(sy_ref.shape, sy_ref.dtype)

    plan, k_tot = _tile_plan(st_ref, ti)

    def chunk_copy(e, s16, off, j):
        src = pl.multiple_of(s16 + j * DMA_ROWS, BF16_ROWS)
        dst = pl.multiple_of(off + j * DMA_ROWS, DMA_ROWS)
        return pltpu.make_async_copy(ye_ref.at[e, pl.ds(src, DMA_ROWS)], sy_ref.at[pl.ds(dst, DMA_ROWS)], sem_ref.at[0])

    total = jnp.int32(0)
    for e, (s16, nch, off, end) in enumerate(plan):
        def issue(j, c, e=e, s16=s16, off=off):
            chunk_copy(e, s16, off, j).start()
            return c
        lax.fori_loop(0, nch, issue, 0)
        total = total + nch

    lane = lax.broadcasted_iota(jnp.int32, (1, LANES), 1)
    delta = jnp.zeros((1, LANES), jnp.int32)
    for e, (s16, nch, off, end) in enumerate(plan):
        delta = jnp.where(lane == e, off - s16, delta)
    p = ptok_ref[...]
    srow = jnp.where(p >= 0, p + delta, -1)
    acc_ref[...] = jnp.zeros(acc_ref.shape, jnp.float32)

    def wait_one(_, c):
        chunk_copy(0, 0, 0, 0).wait()
        return c
    lax.fori_loop(0, total, wait_one, 0)

    def gather(kc, c):
        k0 = pl.multiple_of(kc * SEQ_ALIGN, SEQ_ALIGN)
        k = k0 + lax.broadcasted_iota(jnp.int32, (1, SEQ_ALIGN), 1)
        onehot = jnp.zeros((t, SEQ_ALIGN), jnp.float32)
        for e in range(N_EXPERTS):
            onehot = jnp.where(srow[:, e:e + 1] == k, 1.0, onehot)
        acc_ref[...] = acc_ref[...] + jnp.dot(onehot.astype(jnp.bfloat16), sy_ref[pl.ds(k0, SEQ_ALIGN), :],
                                              preferred_element_type=jnp.float32)
        return c

    lax.fori_loop(0, lax.shift_right_logical(k_tot + SEQ_ALIGN - 1, SEQ_ALIGN.bit_length() - 1), gather, 0)
    tsum = ALPHA * h_ref[...] + acc_ref[...]
    _norm_route(tsum, g_ref, b_ref, None, ti * t, lp, seq_len, ho_ref, hb_ref, None, None)


def combine(starts, ye, ptok, h, g, bb, lp, seq_len):
    n = h.shape[0]
    nt = n // TOK_TILE
    row = lambda i, st: (i, 0)
    fix = lambda i, st: (0, 0)
    grid_spec = pltpu.PrefetchScalarGridSpec(
        num_scalar_prefetch=1,
        grid=(nt,),
        in_specs=[pl.BlockSpec(memory_space=pl.ANY), pl.BlockSpec((TOK_TILE, LANES), row),
                  pl.BlockSpec((TOK_TILE, D_MODEL), row), pl.BlockSpec((1, D_MODEL), fix), pl.BlockSpec((1, D_MODEL), fix)],
        out_specs=[pl.BlockSpec((TOK_TILE, D_MODEL), row), pl.BlockSpec((TOK_TILE, D_MODEL), row)],
        scratch_shapes=[pltpu.VMEM((_stage_max(), D_MODEL), jnp.bfloat16), pltpu.VMEM((TOK_TILE, D_MODEL), jnp.float32),
                        pltpu.SemaphoreType.DMA((1,))],
    )
    return pl.pallas_call(
        functools.partial(_combine_kernel, lp=lp, seq_len=seq_len),
        grid_spec=grid_spec,
        out_shape=[jax.ShapeDtypeStruct((n, D_MODEL), jnp.float32), jax.ShapeDtypeStruct((n, D_MODEL), jnp.bfloat16)],
        compiler_params=_cparams("arbitrary"),
        name="moe_combine",
    )(starts, ye, ptok, h, g, bb)


def moe_block(h, hb, aff, afft, w1, w3, w2, g, bb, n_real, lp, seq_len):
    cap = max(1, CAPACITY_FACTOR * n_real // N_EXPERTS)
    cap_pad = -(-(cap + DMA_ROWS) // FFN_ROWS) * FFN_ROWS
    pos, ptok, start3 = route(afft, cap)
    starts = start3[:, 0, :N_EXPERTS].reshape(-1)
    xe, ge = dispatch(starts, hb, pos, aff, cap, cap_pad)
    ye = expert_ffn(xe, ge, w1, w3, w2, cap)
    return combine(starts, ye, ptok, h, g, bb, lp, seq_len)


def _prep_weights(meta, w_in, conv_w, conv_b, a_log_f, a_log_b, dt_bias_f, dt_bias_b, d_skip, ssd_norm_w,
                  lam_q1, lam_k1, lam_q2, lam_k2, subln_w, rel_bias, w_out, pool_w, pool_b, pool_scale,
                  ln1_g, ln1_b, ln2_g, ln2_b, w_router, w1, w3, w2):
    bf = jnp.bfloat16
    f32 = jnp.float32
    w = w_in[0]
    o = [0, SSD_INNER, SSD_INNER + SSD_CONV_CH, SSD_INNER + SSD_CONV_CH + 2 * SSD_HEADS]
    wz, wxbc, wdt = w[:, o[0]:o[1]], w[:, o[1]:o[2]], w[:, o[2]:o[3]]
    wq, wk, wv = (w[:, o[3] + i * ATT_QK:o[3] + (i + 1) * ATT_QK] for i in range(3))
    pad = jnp.zeros((D_MODEL, _PROJ_COLS - _COL_DT - 2 * SSD_HEADS), f32)
    w_tok = jnp.concatenate([wz, wxbc, wk, wdt, pad], axis=1).astype(bf)
    w_qv = jnp.concatenate([wq * (ATT_HEAD_DIM ** -0.5), wv], axis=1).T.astype(bf)
    lane_pad = lambda v: jnp.pad(v.astype(f32), (0, LANES - v.shape[0]))
    par = jnp.stack([lane_pad(jnp.concatenate([dt_bias_f[0], dt_bias_b[0]])),
                     lane_pad(jnp.concatenate([a_log_f[0], a_log_b[0]]))] + [jnp.zeros((LANES,), f32)] * 6)
    lam4 = jnp.stack([lane_pad(lam_q1[0]), lane_pad(lam_k1[0]), lane_pad(lam_q2[0]), lane_pad(lam_k2[0])]
                     + [jnp.zeros((LANES,), f32)] * 4)
    return dict(
        meta=meta, w_tok=w_tok, w_qv=w_qv, conv_w=conv_w[0, :, 0, :], conv_b=conv_b[0][None], par=par,
        dsk=jnp.repeat(d_skip[0], SSD_HEAD_DIM)[None], nw=ssd_norm_w[0][None], lam4=lam4,
        subw=jnp.broadcast_to(subln_w[0][:, None], (ATT_V_DIM, SEQ_ALIGN)), rel_bias=rel_bias,
        w_out=w_out[0].astype(bf), pool_w=pool_w[0].astype(bf), pool_b=pool_b[0][None], pool_scale=pool_scale[0][None],
        ln1_g=ln1_g, ln1_b=ln1_b, ln2_g=ln2_g, ln2_b=ln2_b,
        w_r=jnp.pad(w_router, ((0, 0), (0, 0), (0, LANES - N_EXPERTS))).astype(bf),
        w1=w1.astype(bf), w3=w3.astype(bf), w2=w2.astype(bf))


def _trunk(x, p, bias5):
    b, s, _ = x.shape
    seq_len = s + N_META
    lp = -(-seq_len // SEQ_ALIGN) * SEQ_ALIGN
    n = b * lp
    n_real = b * seq_len
    meta = jnp.broadcast_to(p["meta"][None].astype(x.dtype), (b, N_META, D_MODEL))
    h = jnp.concatenate([meta, x, jnp.zeros((b, lp - seq_len, D_MODEL), x.dtype)], axis=1).reshape(n, D_MODEL)
    hb = h.astype(jnp.bfloat16)
    lambda_init = 0.8 - 0.6 * math.exp(-0.3 * 0)

    proj = mm_nn(hb, p["w_tok"], jnp.float32, ROW_TILE, 512)
    qv = mm_nt_blocked(p["w_qv"], hb, jnp.bfloat16, ROW_TILE, 512)
    proj3 = proj.reshape(b, lp, _PROJ_COLS)
    xbc = conv_silu(proj3, p["conv_w"], p["conv_b"]).reshape(n, SSD_CONV_CH)
    yf = ssd_scan(xbc, proj, p["par"], b, lp, seq_len, False)
    yb = ssd_scan(xbc, proj, p["par"], b, lp, seq_len, True)
    y_ssd = ssd_gate(yf, yb, xbc, proj, p["dsk"], p["nw"], ROW_TILE)
    y_att = diff_attention(qv, proj3, bias5, p["lam4"], p["subw"], b, lp, seq_len, lambda_init)
    h, hb, aff, afft = out_proj_norm_route(y_ssd, y_att, p["w_out"], h, p["ln1_g"][0][None], p["ln1_b"][0][None],
                                           p["w_r"][0], lp, seq_len)
    h, hb = moe_block(h, hb, aff, afft, p["w1"][0], p["w3"][0], p["w2"][0], p["ln2_g"][0][None], p["ln2_b"][0][None],
                      n_real, lp, seq_len)
    d = pool_diff(h.reshape(b, lp, D_MODEL), seq_len).reshape(n, D_MODEL)
    h, hb, aff, afft = pool_proj_norm_route(d, p["pool_w"], p["pool_b"], p["pool_scale"], h, p["ln1_g"][1][None],
                                            p["ln1_b"][1][None], p["w_r"][1], lp, seq_len)
    h, hb = moe_block(h, hb, aff, afft, p["w1"][1], p["w3"][1], p["w2"][1], p["ln2_g"][1][None], p["ln2_b"][1][None],
                      n_real, lp, seq_len)
    return h.reshape(b, lp, D_MODEL)[:, N_META:seq_len]


def kernel(x_prompt, x_sample, meta, w_in, conv_w, conv_b, a_log_f, a_log_b, dt_bias_f, dt_bias_b, d_skip, ssd_norm_w, lam_q1, lam_k1, lam_q2, lam_k2, subln_w, rel_bias, w_out, pool_w, pool_b, pool_scale, ln1_g, ln1_b, ln2_g, ln2_b, w_router, w1, w3, w2):
    p = _prep_weights(meta, w_in, conv_w, conv_b, a_log_f, a_log_b, dt_bias_f, dt_bias_b, d_skip, ssd_norm_w,
                      lam_q1, lam_k1, lam_q2, lam_k2, subln_w, rel_bias, w_out, pool_w, pool_b, pool_scale,
                      ln1_g, ln1_b, ln2_g, ln2_b, w_router, w1, w3, w2)
    bias5 = bias_tiles(rel_bias)
    return (_trunk(x_prompt, p, bias5), _trunk(x_sample, p, bias5))
```

```python
import functools
import math

import jax
import jax.numpy as jnp
from jax import lax
from jax.experimental import pallas as pl
from jax.experimental.pallas import tpu as pltpu

D_MODEL = 2048
DEPTH = 2
N_META = 16
SSD_HEADS = 16
SSD_HEAD_DIM = 64
SSD_INNER = SSD_HEADS * SSD_HEAD_DIM
SSD_GROUPS = 4
SSD_STATE = 128
SSD_CONV = 5
SSD_CHUNK = 128
SSD_CONV_CH = SSD_INNER + 2 * SSD_GROUPS * SSD_STATE
ATT_HEADS = 8
ATT_HEAD_DIM = 64
ATT_V_DIM = 2 * ATT_HEAD_DIM
ATT_QK = ATT_HEADS * 2 * ATT_HEAD_DIM
ATT_INNER = ATT_HEADS * ATT_V_DIM
REL_BUCKETS = 32
POOL_WINDOWS = (2, 4, 8, 16)
POOL_GROUP = D_MODEL // len(POOL_WINDOWS)
N_EXPERTS = 16
EXPERT_FF = 2048
CAPACITY_FACTOR = 2
ALPHA = (2 * DEPTH) ** 0.25
LN_EPS = 1e-5

LANES = 128
BF16_ROWS = 16
SEQ_ALIGN = 256
TOK_TILE = 256
ROW_TILE = 512
DMA_ROWS = 32
FFN_ROWS = 704
FFN_COLS = 512
VMEM_LIMIT = 56 * 1024 * 1024

_BUCKET_STARTS = (1, 2, 3, 4, 5, 6, 7, 8, 12, 16, 23, 32, 46, 64, 91)
_COL_Z, _COL_XBC, _COL_DT, _PROJ_COLS = 0, 1024, 3072, 3584
ATT_CHUNK = 4
LOG2E = math.log2(math.e)


def _cparams(*sem):
    return pltpu.CompilerParams(dimension_semantics=sem, vmem_limit_bytes=VMEM_LIMIT)


def _sigmoid(x):
    return 1.0 / (1.0 + jnp.exp(-x))


def _softplus(x):
    return jnp.maximum(x, 0.0) + jnp.log(1.0 + jnp.exp(-jnp.abs(x)))


def _split3(x):
    p1 = x.astype(jnp.bfloat16)
    r1 = x - p1.astype(jnp.float32)
    p2 = r1.astype(jnp.bfloat16)
    p3 = (r1 - p2.astype(jnp.float32)).astype(jnp.bfloat16)
    return p1, p2, p3


def _mm_nn_kernel(x_ref, w_ref, o_ref):
    o_ref[...] = jnp.dot(x_ref[...], w_ref[...], preferred_element_type=jnp.float32).astype(o_ref.dtype)


def mm_nn(x, w, out_dtype, tm, tn):
    n, k = x.shape
    nn = w.shape[1]
    return pl.pallas_call(
        _mm_nn_kernel,
        grid=(n // tm, nn // tn),
        in_specs=[pl.BlockSpec((tm, k), lambda i, j: (i, 0)), pl.BlockSpec((k, tn), lambda i, j: (0, j))],
        out_specs=pl.BlockSpec((tm, tn), lambda i, j: (i, j)),
        out_shape=jax.ShapeDtypeStruct((n, nn), out_dtype),
        compiler_params=_cparams("parallel", "arbitrary"),
        name="mm_nn",
    )(x, w)


def _mm_nt_kernel(w_ref, x_ref, o_ref):
    r = lax.dot_general(w_ref[...], x_ref[...], (((1,), (1,)), ((), ())), preferred_element_type=jnp.float32)
    for s in range(o_ref.shape[0]):
        o_ref[s] = r[:, s * SEQ_ALIGN:(s + 1) * SEQ_ALIGN].astype(o_ref.dtype)


def mm_nt_blocked(wt, x, out_dtype, tm, tn):
    n, k = x.shape
    nn = wt.shape[0]
    per = tm // SEQ_ALIGN
    return pl.pallas_call(
        _mm_nt_kernel,
        grid=(n // tm, nn // tn),
        in_specs=[pl.BlockSpec((tn, k), lambda i, j: (j, 0)), pl.BlockSpec((tm, k), lambda i, j: (i, 0))],
        out_specs=pl.BlockSpec((per, tn, SEQ_ALIGN), lambda i, j: (i, j, 0)),
        out_shape=jax.ShapeDtypeStruct((n // SEQ_ALIGN, nn, SEQ_ALIGN), out_dtype),
        compiler_params=_cparams("parallel", "arbitrary"),
        name="mm_nt",
    )(wt, x)


def _conv_kernel(x_ref, w_ref, b_ref, o_ref, pad_ref, *, rows):
    lp = x_ref.shape[1]
    halo = 8
    pad_ref[0:halo, :] = jnp.zeros((halo, LANES), jnp.float32)
    pad_ref[lp + halo:lp + 2 * halo, :] = jnp.zeros((halo, LANES), jnp.float32)
    pad_ref[halo:lp + halo, :] = x_ref[0]
    for t0 in range(0, lp, rows):
        acc = jnp.broadcast_to(b_ref[...], (rows, LANES))
        for k in range(SSD_CONV):
            off = t0 + halo - SSD_CONV // 2 + k
            acc = acc + pad_ref[off:off + rows, :] * w_ref[k:k + 1, :]
        o_ref[0, t0:t0 + rows, :] = acc * _sigmoid(acc)


def conv_silu(proj3, conv_w, conv_b):
    b, lp, _ = proj3.shape
    c0 = _COL_XBC // LANES
    return pl.pallas_call(
        functools.partial(_conv_kernel, rows=SEQ_ALIGN),
        grid=(b, SSD_CONV_CH // LANES),
        in_specs=[
            pl.BlockSpec((1, lp, LANES), lambda i, j: (i, 0, c0 + j)),
            pl.BlockSpec((SSD_CONV, LANES), lambda i, j: (0, j)),
            pl.BlockSpec((1, LANES), lambda i, j: (0, j)),
        ],
        out_specs=pl.BlockSpec((1, lp, LANES), lambda i, j: (i, 0, j)),
        out_shape=jax.ShapeDtypeStruct((b, lp, SSD_CONV_CH), jnp.float32),
        scratch_shapes=[pltpu.VMEM((lp + 16, LANES), jnp.float32)],
        compiler_params=_cparams("parallel", "parallel"),
        name="conv_silu",
    )(proj3, conv_w, conv_b)


def _ssd_kernel(x_ref, b_ref, c_ref, dt_ref, par_ref, y_ref, h_ref, *, seq_len, reverse):
    cs = SSD_CHUNK
    step = pl.program_id(1)
    chunk = (pl.num_programs(1) - 1 - step) if reverse else step
    lane0 = SSD_HEADS if reverse else 0

    @pl.when(step == 0)
    def _():
        h_ref[...] = jnp.zeros(h_ref.shape, jnp.float32)

    sub = lax.broadcasted_iota(jnp.int32, (cs, cs), 0)
    lan = lax.broadcasted_iota(jnp.int32, (cs, cs), 1)
    valid = (chunk * cs + sub[:, 0:1]) < seq_len
    dt = jnp.where(valid, _softplus(dt_ref[...] + par_ref[0:1, :]), 0.0)
    a = dt * -jnp.exp(par_ref[1:2, :])
    a_t = a.T[0:2 * SSD_HEADS, :]
    if reverse:
        incl, rest = (sub >= lan), (sub < lan)
    else:
        incl, rest = (sub <= lan), (sub > lan)
    tri = jnp.concatenate([incl.astype(jnp.bfloat16), rest.astype(jnp.bfloat16),
                           jnp.ones((cs, cs), jnp.bfloat16)], axis=1)
    pieces = jnp.concatenate(_split3(a_t), axis=0)
    sums = jnp.dot(pieces, tri, preferred_element_type=jnp.float32)
    nh2 = 2 * SSD_HEADS
    sums = sums[0:nh2] + sums[nh2:2 * nh2] + sums[2 * nh2:3 * nh2]
    cum_row, rest_row, tot_row = sums[:, 0:cs], sums[:, cs:2 * cs], sums[:, 2 * cs:3 * cs]
    cum_col = jnp.concatenate([cum_row, jnp.zeros((cs - nh2, cs), jnp.float32)], axis=0).T
    mask = (sub <= lan) if reverse else (sub >= lan)
    left = lan[0:1, :] < SSD_HEAD_DIM

    for g in range(SSD_GROUPS):
        bg = b_ref[:, g * SSD_STATE:(g + 1) * SSD_STATE]
        cg = c_ref[:, g * SSD_STATE:(g + 1) * SSD_STATE]
        cb = lax.dot_general(cg.astype(jnp.bfloat16), bg.astype(jnp.bfloat16), (((1,), (1,)), ((), ())),
                             preferred_element_type=jnp.float32)
        bg_t = bg.T
        for pp in range(2):
            p = 2 * g + pp
            r0, r1 = lane0 + 2 * p, lane0 + 2 * p + 1
            dtp = jnp.where(left, dt[:, r0:r0 + 1], dt[:, r1:r1 + 1])
            xs = (x_ref[:, p * LANES:(p + 1) * LANES] * dtp).astype(jnp.bfloat16)
            h_t = h_ref[p]
            rhs = jnp.concatenate([xs, h_t.astype(jnp.bfloat16)], axis=0)
            ys, ss = [], []
            for r in (r0, r1):
                cc = cum_col[:, r:r + 1]
                dec = jnp.exp(jnp.where(mask, cc - cum_row[r:r + 1, :], -jnp.inf))
                lhs = jnp.concatenate([(cb * dec).astype(jnp.bfloat16),
                                       (cg * jnp.exp(cc)).astype(jnp.bfloat16)], axis=1)
                ys.append(jnp.dot(lhs, rhs, preferred_element_type=jnp.float32))
                bw = (bg_t * jnp.exp(rest_row[r:r + 1, :])).astype(jnp.bfloat16)
                ss.append(jnp.dot(bw, xs, preferred_element_type=jnp.float32))
            y_ref[:, p * LANES:(p + 1) * LANES] = jnp.where(left, ys[0], ys[1])
            decay = jnp.where(left, jnp.exp(tot_row[r0:r0 + 1, :]), jnp.exp(tot_row[r1:r1 + 1, :]))
            h_ref[p] = decay * h_t + jnp.where(left, ss[0], ss[1])


def ssd_scan(xbc, proj, par, b, lp, seq_len, reverse):
    n = xbc.shape[0]
    nc = lp // SSD_CHUNK
    cdt = _COL_DT // LANES

    def row(i, c):
        return i * nc + ((nc - 1 - c) if reverse else c)

    return pl.pallas_call(
        functools.partial(_ssd_kernel, seq_len=seq_len, reverse=reverse),
        grid=(b, nc),
        in_specs=[
            pl.BlockSpec((SSD_CHUNK, SSD_INNER), lambda i, c: (row(i, c), 0)),
            pl.BlockSpec((SSD_CHUNK, SSD_GROUPS * SSD_STATE), lambda i, c: (row(i, c), 2)),
            pl.BlockSpec((SSD_CHUNK, SSD_GROUPS * SSD_STATE), lambda i, c: (row(i, c), 3)),
            pl.BlockSpec((SSD_CHUNK, LANES), lambda i, c: (row(i, c), cdt)),
            pl.BlockSpec((8, LANES), lambda i, c: (0, 0)),
        ],
        out_specs=pl.BlockSpec((SSD_CHUNK, SSD_INNER), lambda i, c: (row(i, c), 0)),
        out_shape=jax.ShapeDtypeStruct((n, SSD_INNER), jnp.float32),
        scratch_shapes=[pltpu.VMEM((SSD_HEADS // 2, SSD_STATE, LANES), jnp.float32)],
        compiler_params=_cparams("parallel", "arbitrary"),
        name="ssd_bwd" if reverse else "ssd_fwd",
    )(xbc, xbc, xbc, proj, par)


def _gate_kernel(yf_ref, yb_ref, x_ref, z_ref, dsk_ref, nw_ref, o_ref):
    z = z_ref[...]
    y = (yf_ref[...] + yb_ref[...] + dsk_ref[...] * x_ref[...]) * (z * _sigmoid(z))
    gw = SSD_INNER // SSD_GROUPS
    for g in range(SSD_GROUPS):
        seg = y[:, g * gw:(g + 1) * gw]
        ms = jnp.mean(seg * seg, axis=-1, keepdims=True)
        o_ref[:, g * gw:(g + 1) * gw] = (seg * lax.rsqrt(ms + LN_EPS) * nw_ref[:, g * gw:(g + 1) * gw]).astype(o_ref.dtype)


def ssd_gate(yf, yb, xbc, proj, dsk, nw, tm):
    n = yf.shape[0]
    blk = lambda i: (i, 0)
    return pl.pallas_call(
        _gate_kernel,
        grid=(n // tm,),
        in_specs=[pl.BlockSpec((tm, SSD_INNER), blk)] * 4 + [pl.BlockSpec((1, SSD_INNER), lambda i: (0, 0))] * 2,
        out_specs=pl.BlockSpec((tm, SSD_INNER), blk),
        out_shape=jax.ShapeDtypeStruct((n, SSD_INNER), jnp.bfloat16),
        compiler_params=_cparams("parallel"),
        name="ssd_gate",
    )(yf, yb, xbc, proj, dsk, nw)


def _bias_kernel(rb_ref, o_ref):
    h = pl.program_id(0)
    d = pl.program_id(1) - 2
    t = SEQ_ALIGN
    kk = lax.broadcasted_iota(jnp.int32, (t, t), 0)
    qq = lax.broadcasted_iota(jnp.int32, (t, t), 1)
    rel = d * t + kk - qq
    n = jnp.abs(rel)
    half = REL_BUCKETS // 2
    vneg = jnp.full((t, t), rb_ref[0, h], jnp.float32)
    vpos = jnp.full((t, t), rb_ref[half, h], jnp.float32)
    for j, start in enumerate(_BUCKET_STARTS, start=1):
        ge = n >= start
        vneg = jnp.where(ge, rb_ref[j, h], vneg)
        vpos = jnp.where(ge, rb_ref[half + j, h], vpos)
    o_ref[0, 0] = jnp.where(rel > 0, vpos, vneg) * LOG2E


def bias_tiles(rel_bias):
    return pl.pallas_call(
        _bias_kernel,
        grid=(ATT_HEADS, 5),
        in_specs=[pl.BlockSpec(memory_space=pltpu.SMEM)],
        out_specs=pl.BlockSpec((1, 1, SEQ_ALIGN, SEQ_ALIGN), lambda h, d: (h, d, 0, 0)),
        out_shape=jax.ShapeDtypeStruct((ATT_HEADS, 5, SEQ_ALIGN, SEQ_ALIGN), jnp.float32),
        compiler_params=_cparams("parallel", "parallel"),
        name="bias_tiles",
    )(rel_bias)


def _attn_kernel(q_ref, v_ref, k_ref, bias_ref, lam_ref, sub_ref, o_ref, s_ref, *, seq_len, lambda_init):
    t = SEQ_ALIGN
    group = ATT_CHUNK
    qb = pl.program_id(2)
    q_t = q_ref[0]
    sub = lax.broadcasted_iota(jnp.int32, q_t.shape, 0)
    zero = jnp.zeros_like(q_t)
    qs = (jnp.where(sub < ATT_HEAD_DIM, q_t, zero), jnp.where(sub >= ATT_HEAD_DIM, q_t, zero))
    n_full = seq_len // t
    n_grp = n_full // group
    tail = seq_len - n_full * t
    tail_rows = -(-tail // LANES) * LANES

    def fold(x, op):
        return op(x.reshape(x.shape[0] // 8, 8, t), axis=0)

    def score_block(kb, rows, mx):
        r0 = kb * t
        if not isinstance(kb, int):
            r0 = pl.multiple_of(r0, t)
        kblk = k_ref[0, pl.ds(r0, rows), :]
        bias = bias_ref[0, jnp.clip(kb - qb, -2, 2) + 2, 0:rows, :]
        out = []
        for mi in range(2):
            s = jnp.dot(kblk, qs[mi], preferred_element_type=jnp.float32) + bias
            if rows != t:
                s = jnp.where(lax.broadcasted_iota(jnp.int32, (rows, 1), 0) < tail, s, -jnp.inf)
            s_ref[mi, pl.ds(r0, rows), :] = s
            out.append(jnp.maximum(mx[mi], fold(s, jnp.max)))
        return tuple(out)

    neg = jnp.full((8, t), -jnp.inf, jnp.float32)

    def score_group(g, mx):
        for j in range(group):
            mx = score_block(g * group + j, t, mx)
        return mx

    mx = lax.fori_loop(0, n_grp, score_group, (neg, neg))
    for kb in range(n_grp * group, n_full):
        mx = score_block(kb, t, mx)
    if tail:
        mx = score_block(n_full, tail_rows, mx)
    ms_ = [jnp.max(m, axis=0, keepdims=True) for m in mx]

    def prob_blocks(kbs, rows, carry):
        ls, accs = list(carry[0:2]), list(carry[2:4])
        vcat = jnp.concatenate([v_ref[kb][:, 0:rows] for kb in kbs], axis=1) if len(kbs) > 1 else v_ref[kbs[0]][:, 0:rows]
        for mi in range(2):
            ps = []
            for kb in kbs:
                r0 = kb * t
                if not isinstance(kb, int):
                    r0 = pl.multiple_of(r0, t)
                p = jnp.exp2(s_ref[mi, pl.ds(r0, rows), :] - ms_[mi])
                ls[mi] = ls[mi] + fold(p, jnp.sum)
                ps.append(p.astype(jnp.bfloat16))
            pcat = jnp.concatenate(ps, axis=0) if len(ps) > 1 else ps[0]
            accs[mi] = accs[mi] + jnp.dot(vcat, pcat, preferred_element_type=jnp.float32)
        return tuple(ls) + tuple(accs)

    z8 = jnp.zeros((8, t), jnp.float32)
    zacc = jnp.zeros((ATT_V_DIM, t), jnp.float32)
    carry = lax.fori_loop(0, n_grp, lambda g, c: prob_blocks([g * group + j for j in range(group)], t, c),
                          (z8, z8, zacc, zacc))
    for kb in range(n_grp * group, n_full):
        carry = prob_blocks([kb], t, carry)
    if tail:
        carry = prob_blocks([n_full], tail_rows, carry)
    l1 = jnp.sum(carry[0], axis=0, keepdims=True)
    l2 = jnp.sum(carry[1], axis=0, keepdims=True)
    lv = lam_ref[...]
    lam = (jnp.exp(jnp.sum(lv[0:1] * lv[1:2], axis=-1, keepdims=True))
           - jnp.exp(jnp.sum(lv[2:3] * lv[3:4], axis=-1, keepdims=True)) + lambda_init)
    o = carry[2] * (1.0 / l1) - lam * (carry[3] * (1.0 / l2))
    ms = jnp.mean(o * o, axis=0, keepdims=True)
    o = o * lax.rsqrt(ms + LN_EPS) * sub_ref[...] * (1.0 - lambda_init)
    o_ref[...] = o.T.astype(o_ref.dtype)


def diff_attention(qv, k3, bias5, lam4, subw, b, lp, seq_len, lambda_init):
    nq = lp // SEQ_ALIGN
    return pl.pallas_call(
        functools.partial(_attn_kernel, seq_len=seq_len, lambda_init=lambda_init),
        grid=(b, ATT_HEADS, nq),
        in_specs=[
            pl.BlockSpec((1, ATT_V_DIM, SEQ_ALIGN), lambda i, h, q: (i * nq + q, h, 0)),
            pl.BlockSpec((nq, ATT_V_DIM, SEQ_ALIGN), lambda i, h, q: (i, ATT_HEADS + h, 0)),
            pl.BlockSpec((1, lp, LANES), lambda i, h, q: (i, 0, h)),
            pl.BlockSpec((1, 5, SEQ_ALIGN, SEQ_ALIGN), lambda i, h, q: (h, 0, 0, 0)),
            pl.BlockSpec((8, LANES), lambda i, h, q: (0, 0)),
            pl.BlockSpec((ATT_V_DIM, SEQ_ALIGN), lambda i, h, q: (0, 0)),
        ],
        out_specs=pl.BlockSpec((SEQ_ALIGN, ATT_V_DIM), lambda i, h, q: (i * nq + q, h)),
        out_shape=jax.ShapeDtypeStruct((b * lp, ATT_INNER), jnp.bfloat16),
        scratch_shapes=[pltpu.VMEM((2, lp, SEQ_ALIGN), jnp.float32)],
        compiler_params=_cparams("parallel", "parallel", "arbitrary"),
        name="diff_attn",
    )(qv, qv, k3, bias5, lam4, subw)


def _norm_route(t, g_ref, b_ref, wr_ref, tile_row0, lp, seq_len, h_ref, hb_ref, aff_ref, afft_ref):
    rows = t.shape[0]
    mu = jnp.mean(t, axis=-1, keepdims=True)
    tc = t - mu
    var = jnp.mean(tc * tc, axis=-1, keepdims=True)
    y = tc * lax.rsqrt(var + LN_EPS) * g_ref[...] + b_ref[...]
    pos = tile_row0 % lp + lax.broadcasted_iota(jnp.int32, (rows, 1), 0)
    valid = pos < seq_len
    y = jnp.where(valid, y, 0.0)
    h_ref[...] = y
    yb = y.astype(jnp.bfloat16)
    hb_ref[...] = yb
    if aff_ref is not None:
        logits = jnp.dot(yb, wr_ref[...], preferred_element_type=jnp.float32)
        lane = lax.broadcasted_iota(jnp.int32, logits.shape, 1)
        is_e = lane < N_EXPERTS
        logits = jnp.where(is_e, logits, -jnp.inf)
        e = jnp.exp(logits - jnp.max(logits, axis=-1, keepdims=True))
        aff = e / jnp.sum(e, axis=-1, keepdims=True)
        aff = jnp.where(valid, aff, jnp.where(is_e, -1.0, 0.0))
        aff_ref[...] = aff
        afft_ref[0] = aff.T[0:N_EXPERTS, :]


def _outproj_kernel(ys_ref, ya_ref, w_ref, h_ref, g_ref, b_ref, wr_ref, ho_ref, hb_ref, aff_ref, afft_ref, *, lp, seq_len):
    mix = (jnp.dot(ys_ref[...], w_ref[0:SSD_INNER, :], preferred_element_type=jnp.float32)
           + jnp.dot(ya_ref[...], w_ref[SSD_INNER:, :], preferred_element_type=jnp.float32))
    t = ALPHA * h_ref[...] + mix
    _norm_route(t, g_ref, b_ref, wr_ref, pl.program_id(0) * h_ref.shape[0], lp, seq_len, ho_ref, hb_ref, aff_ref, afft_ref)


def _poolproj_kernel(d_ref, w_ref, pb_ref, ps_ref, h_ref, g_ref, b_ref, wr_ref, ho_ref, hb_ref, aff_ref, afft_ref, *, lp, seq_len):
    outs = []
    for gi in range(len(POOL_WINDOWS)):
        outs.append(jnp.dot(d_ref[:, gi * POOL_GROUP:(gi + 1) * POOL_GROUP], w_ref[gi], preferred_element_type=jnp.float32))
    mix = (jnp.concatenate(outs, axis=-1) + pb_ref[...]) * ps_ref[...]
    t = ALPHA * h_ref[...] + mix
    _norm_route(t, g_ref, b_ref, wr_ref, pl.program_id(0) * h_ref.shape[0], lp, seq_len, ho_ref, hb_ref, aff_ref, afft_ref)


def _mix_outputs(n, tm):
    row = lambda i: (i, 0)
    specs = [pl.BlockSpec((tm, D_MODEL), row), pl.BlockSpec((tm, D_MODEL), row), pl.BlockSpec((tm, LANES), row),
             pl.BlockSpec((1, N_EXPERTS, tm), lambda i: (i, 0, 0))]
    shapes = [jax.ShapeDtypeStruct((n, D_MODEL), jnp.float32), jax.ShapeDtypeStruct((n, D_MODEL), jnp.bfloat16),
              jax.ShapeDtypeStruct((n, LANES), jnp.float32), jax.ShapeDtypeStruct((n // tm, N_EXPERTS, tm), jnp.float32)]
    return specs, shapes


def out_proj_norm_route(y_ssd, y_att, w_out, h, g, bb, w_r, lp, seq_len):
    n = h.shape[0]
    tm = TOK_TILE
    row = lambda i: (i, 0)
    fix = lambda i: (0, 0)
    specs, shapes = _mix_outputs(n, tm)
    return pl.pallas_call(
        functools.partial(_outproj_kernel, lp=lp, seq_len=seq_len),
        grid=(n // tm,),
        in_specs=[pl.BlockSpec((tm, SSD_INNER), row), pl.BlockSpec((tm, ATT_INNER), row),
                  pl.BlockSpec((D_MODEL, D_MODEL), fix), pl.BlockSpec((tm, D_MODEL), row),
                  pl.BlockSpec((1, D_MODEL), fix), pl.BlockSpec((1, D_MODEL), fix), pl.BlockSpec((D_MODEL, LANES), fix)],
        out_specs=specs, out_shape=shapes,
        compiler_params=_cparams("parallel"),
        name="outproj_ln_route",
    )(y_ssd, y_att, w_out, h, g, bb, w_r)


def pool_proj_norm_route(d, pool_w, pool_b, pool_scale, h, g, bb, w_r, lp, seq_len):
    n = h.shape[0]
    tm = TOK_TILE
    row = lambda i: (i, 0)
    fix = lambda i: (0, 0)
    specs, shapes = _mix_outputs(n, tm)
    return pl.pallas_call(
        functools.partial(_poolproj_kernel, lp=lp, seq_len=seq_len),
        grid=(n // tm,),
        in_specs=[pl.BlockSpec((tm, D_MODEL), row),
                  pl.BlockSpec((len(POOL_WINDOWS), POOL_GROUP, POOL_GROUP), lambda i: (0, 0, 0)),
                  pl.BlockSpec((1, D_MODEL), fix), pl.BlockSpec((1, D_MODEL), fix), pl.BlockSpec((tm, D_MODEL), row),
                  pl.BlockSpec((1, D_MODEL), fix), pl.BlockSpec((1, D_MODEL), fix), pl.BlockSpec((D_MODEL, LANES), fix)],
        out_specs=specs, out_shape=shapes,
        compiler_params=_cparams("parallel"),
        name="poolproj_ln_route",
    )(d, pool_w, pool_b, pool_scale, h, g, bb, w_r)


def _pool_kernel(x_ref, o_ref, s0_ref, s1_ref, *, seq_len):
    lp = x_ref.shape[1]
    halo = 16
    gi = pl.program_id(1) // (POOL_GROUP // LANES)
    zeros = jnp.zeros((halo, LANES), jnp.float32)
    for ref in (s0_ref, s1_ref):
        ref[0:halo, :] = zeros
        ref[lp + halo:lp + 2 * halo, :] = zeros
    s0_ref[halo:lp + halo, :] = x_ref[0]
    x = x_ref[0]
    t = lax.broadcasted_iota(jnp.int32, (lp, 1), 0)
    src, dst = s0_ref, s1_ref
    d = jnp.zeros((lp, LANES), jnp.float32)
    width = 1
    for wi, w in enumerate(POOL_WINDOWS):
        while width < w:
            dst[halo:lp + halo, :] = src[halo:lp + halo, :] + src[halo + width:lp + halo + width, :]
            dst[0:halo, :] = src[0:halo, :] + src[width:halo + width, :]
            src, dst = dst, src
            width *= 2
        win = src[halo - w // 2:lp + halo - w // 2, :]
        cnt = jnp.minimum(t + w // 2 - 1, seq_len - 1) - jnp.maximum(t - w // 2, 0) + 1
        mean = win / cnt.astype(jnp.float32)
        d = jnp.where(gi == wi, mean - x, d)
    o_ref[0] = jnp.where(t < seq_len, d, 0.0).astype(o_ref.dtype)


def pool_diff(h3, seq_len):
    b, lp, _ = h3.shape
    return pl.pallas_call(
        functools.partial(_pool_kernel, seq_len=seq_len),
        grid=(b, D_MODEL // LANES),
        in_specs=[pl.BlockSpec((1, lp, LANES), lambda i, j: (i, 0, j))],
        out_specs=pl.BlockSpec((1, lp, LANES), lambda i, j: (i, 0, j)),
        out_shape=jax.ShapeDtypeStruct((b, lp, D_MODEL), jnp.bfloat16),
        scratch_shapes=[pltpu.VMEM((lp + 32, LANES), jnp.float32)] * 2,
        compiler_params=_cparams("parallel", "parallel"),
        name="pool_diff",
    )(h3)


def _route_kernel(aff_ref, pos_ref, ptok_ref, start_ref, *, cap):
    nt = aff_ref.shape[0]
    t = TOK_TILE

    def bits_of(v):
        return lax.bitcast_convert_type(v, jnp.int32)

    def count(mask):
        return jnp.sum(jnp.sum(jnp.where(mask, 1.0, 0.0), axis=0), axis=1, keepdims=True)

    def search(i, thr):
        cand = thr | jnp.left_shift(jnp.int32(1), 30 - i)
        cnt = count(bits_of(aff_ref[...]) >= cand[None])
        return jnp.where(cnt >= cap, cand, thr)

    thr = lax.fori_loop(0, 31, search, jnp.zeros((N_EXPERTS, 1), jnp.int32))
    need = cap - count(bits_of(aff_ref[...]) > thr[None])

    before = (lax.broadcasted_iota(jnp.int32, (t, t), 0) < lax.broadcasted_iota(jnp.int32, (t, t), 1)).astype(jnp.bfloat16)
    ones8 = jnp.ones((8, t), jnp.bfloat16)
    fill = jnp.full((LANES - N_EXPERTS, t), -1.0, jnp.float32)
    zpad = jnp.zeros((LANES - N_EXPERTS, t), jnp.float32)

    def tile(ti, carry):
        c_tie, c_sel, c_lane = carry
        b = bits_of(aff_ref[ti])
        tie = jnp.where(b == thr, 1.0, 0.0)
        rank = jnp.dot(tie.astype(jnp.bfloat16), before, preferred_element_type=jnp.float32) + c_tie
        sel = jnp.where(b > thr, 1.0, jnp.where(rank < need, tie, 0.0))
        slot = jnp.dot(sel.astype(jnp.bfloat16), before, preferred_element_type=jnp.float32) + c_sel
        posv = jnp.where(sel > 0.0, slot, -1.0)
        pos_ref[ti] = posv.astype(jnp.int32)
        ptok_ref[pl.ds(pl.multiple_of(ti * t, t), t), :] = jnp.concatenate([posv, fill], axis=0).T.astype(jnp.int32)
        start_ref[ti] = c_lane.astype(jnp.int32)
        selp = jnp.concatenate([sel, zpad], axis=0).astype(jnp.bfloat16)
        cnt_lane = lax.dot_general(ones8, selp, (((1,), (1,)), ((), ())), preferred_element_type=jnp.float32)
        return (c_tie + jnp.sum(tie, axis=1, keepdims=True), c_sel + jnp.sum(sel, axis=1, keepdims=True),
                c_lane + cnt_lane[0:1])

    zc = jnp.zeros((N_EXPERTS, 1), jnp.float32)
    _, _, c_lane = lax.fori_loop(0, nt, tile, (zc, zc, jnp.zeros((1, LANES), jnp.float32)))
    start_ref[nt] = c_lane.astype(jnp.int32)


def route(afft, cap):
    nt = afft.shape[0]
    n = nt * TOK_TILE
    return pl.pallas_call(
        functools.partial(_route_kernel, cap=cap),
        out_shape=[jax.ShapeDtypeStruct((nt, N_EXPERTS, TOK_TILE), jnp.int32),
                   jax.ShapeDtypeStruct((n, LANES), jnp.int32),
                   jax.ShapeDtypeStruct((nt + 1, 1, LANES), jnp.int32)],
        compiler_params=pltpu.CompilerParams(vmem_limit_bytes=VMEM_LIMIT),
        name="route",
    )(afft)


def _tile_plan(st_ref, ti):
    plan = []
    off = jnp.int32(0)
    for e in range(N_EXPERTS):
        s = st_ref[ti * N_EXPERTS + e]
        end = st_ref[(ti + 1) * N_EXPERTS + e]
        s16 = s - (s & (BF16_ROWS - 1))
        nch = jnp.where(end > s, lax.shift_right_logical(end - s16 + DMA_ROWS - 1, DMA_ROWS.bit_length() - 1), 0)
        plan.append((s16, nch, off, end))
        off = off + nch * DMA_ROWS
    return plan, off


def _stage_max():
    per = -(-(TOK_TILE + BF16_ROWS - 1) // DMA_ROWS) * DMA_ROWS
    return N_EXPERTS * per + SEQ_ALIGN


def _dispatch_kernel(st_ref, x_ref, pos_ref, aff_ref, xe_ref, ge_ref, sx_ref, sg_ref, cx_ref, cg_ref, n_ref, sem_ref,
                     *, cap, cap_pad):
    ti = pl.program_id(0)
    nt = pl.num_programs(0)
    t = TOK_TILE

    def chunk_copies(e, s16, off, j):
        src = pl.multiple_of(off + j * DMA_ROWS, DMA_ROWS)
        dst = pl.multiple_of(s16 + j * DMA_ROWS, BF16_ROWS)
        return (pltpu.make_async_copy(sx_ref.at[pl.ds(src, DMA_ROWS)], xe_ref.at[e, pl.ds(dst, DMA_ROWS)], sem_ref.at[0]),
                pltpu.make_async_copy(sg_ref.at[pl.ds(src, DMA_ROWS)], ge_ref.at[e, pl.ds(dst, DMA_ROWS)], sem_ref.at[1]))

    def wait_chunks(count):
        def body(_, c):
            for cp in chunk_copies(0, 0, 0, 0):
                cp.wait()
            return c
        lax.fori_loop(0, count, body, 0)

    tail0 = (cap // DMA_ROWS) * DMA_ROWS
    n_tail = (cap_pad - tail0) // DMA_ROWS

    @pl.when(ti == 0)
    def _():
        cx_ref[...] = jnp.zeros(cx_ref.shape, cx_ref.dtype)
        cg_ref[...] = jnp.zeros(cg_ref.shape, cg_ref.dtype)
        sx_ref[0:DMA_ROWS, :] = jnp.zeros((DMA_ROWS, D_MODEL), sx_ref.dtype)
        sg_ref[0:DMA_ROWS, :] = jnp.zeros((DMA_ROWS, LANES), sg_ref.dtype)
        for e in range(N_EXPERTS):
            for j in range(n_tail):
                pltpu.make_async_copy(sx_ref.at[pl.ds(0, DMA_ROWS)], xe_ref.at[e, pl.ds(tail0 + j * DMA_ROWS, DMA_ROWS)], sem_ref.at[0]).start()
                pltpu.make_async_copy(sg_ref.at[pl.ds(0, DMA_ROWS)], ge_ref.at[e, pl.ds(tail0 + j * DMA_ROWS, DMA_ROWS)], sem_ref.at[1]).start()
        wait_chunks(N_EXPERTS * n_tail)
        n_ref[0] = 0

    plan, k_tot = _tile_plan(st_ref, ti)

    sub = lax.broadcasted_iota(jnp.int32, (N_EXPERTS, 1), 0)
    delta = jnp.zeros((N_EXPERTS, 1), jnp.int32)
    for e, (s16, nch, off, end) in enumerate(plan):
        delta = jnp.where(sub == e, off - s16, delta)
    pos = pos_ref[0]
    srow = jnp.where(pos >= 0, pos + delta, -1)

    aff = aff_ref[...]
    a1, a2, a3 = _split3(aff)
    gate3 = (a1.astype(jnp.float32) + pltpu.roll(a2.astype(jnp.float32), N_EXPERTS, 1)
             + pltpu.roll(a3.astype(jnp.float32), 2 * N_EXPERTS, 1)).astype(jnp.bfloat16)
    xt = x_ref[...]

    @pl.when(ti > 0)
    def _():
        wait_chunks(n_ref[0])

    def permute(kc, c):
        k0 = pl.multiple_of(kc * SEQ_ALIGN, SEQ_ALIGN)
        k = k0 + lax.broadcasted_iota(jnp.int32, (SEQ_ALIGN, 1), 0)
        onehot = jnp.zeros((SEQ_ALIGN, t), jnp.float32)
        for e in range(N_EXPERTS):
            onehot = jnp.where(srow[e:e + 1, :] == k, 1.0, onehot)
        onehot = onehot.astype(jnp.bfloat16)
        sx_ref[pl.ds(k0, SEQ_ALIGN), :] = jnp.dot(onehot, xt, preferred_element_type=jnp.float32).astype(sx_ref.dtype)
        sg_ref[pl.ds(k0, SEQ_ALIGN), :] = jnp.dot(onehot, gate3, preferred_element_type=jnp.float32)
        return c

    lax.fori_loop(0, lax.shift_right_logical(k_tot + SEQ_ALIGN - 1, SEQ_ALIGN.bit_length() - 1), permute, 0)

    for e, (s16, nch, off, end) in enumerate(plan):
        @pl.when(nch > 0)
        def _(e=e, s16=s16, off=off, end=end):
            head = pl.ds(pl.multiple_of(off, BF16_ROWS), BF16_ROWS)
            sx_ref[head, :] = sx_ref[head, :] + cx_ref[e]
            sg_ref[head, :] = sg_ref[head, :] + cg_ref[e]
            part = end & (BF16_ROWS - 1)
            last = pl.ds(pl.multiple_of(off + (end - part) - s16, BF16_ROWS), BF16_ROWS)
            cx_ref[e] = jnp.where(part > 0, sx_ref[last, :], jnp.zeros((BF16_ROWS, D_MODEL), cx_ref.dtype))
            cg_ref[e] = jnp.where(part > 0, sg_ref[last, :], 0.0)

    total = jnp.int32(0)
    for e, (s16, nch, off, end) in enumerate(plan):
        def issue(j, c, e=e, s16=s16, off=off):
            for cp in chunk_copies(e, s16, off, j):
                cp.start()
            return c
        lax.fori_loop(0, nch, issue, 0)
        total = total + nch
    n_ref[0] = total

    @pl.when(ti == nt - 1)
    def _():
        wait_chunks(total)


def dispatch(starts, hb, pos, aff, cap, cap_pad):
    n = hb.shape[0]
    nt = n // TOK_TILE
    kmax = _stage_max()
    grid_spec = pltpu.PrefetchScalarGridSpec(
        num_scalar_prefetch=1,
        grid=(nt,),
        in_specs=[pl.BlockSpec((TOK_TILE, D_MODEL), lambda i, st: (i, 0)),
                  pl.BlockSpec((1, N_EXPERTS, TOK_TILE), lambda i, st: (i, 0, 0)),
                  pl.BlockSpec((TOK_TILE, LANES), lambda i, st: (i, 0))],
        out_specs=[pl.BlockSpec(memory_space=pl.ANY), pl.BlockSpec(memory_space=pl.ANY)],
        scratch_shapes=[pltpu.VMEM((kmax, D_MODEL), jnp.bfloat16), pltpu.VMEM((kmax, LANES), jnp.float32),
                        pltpu.VMEM((N_EXPERTS, BF16_ROWS, D_MODEL), jnp.bfloat16),
                        pltpu.VMEM((N_EXPERTS, BF16_ROWS, LANES), jnp.float32),
                        pltpu.SMEM((1,), jnp.int32), pltpu.SemaphoreType.DMA((2,))],
    )
    return pl.pallas_call(
        functools.partial(_dispatch_kernel, cap=cap, cap_pad=cap_pad),
        grid_spec=grid_spec,
        out_shape=[jax.ShapeDtypeStruct((N_EXPERTS, cap_pad, D_MODEL), jnp.bfloat16),
                   jax.ShapeDtypeStruct((N_EXPERTS, cap_pad, LANES), jnp.float32)],
        compiler_params=_cparams("arbitrary"),
        name="moe_dispatch",
    )(starts, hb, pos, aff)


def _ffn_kernel(x_ref, w1_ref, w3_ref, w2_ref, g_ref, o_ref, acc_ref, *, cap):
    e = pl.program_id(0)
    f = pl.program_id(2)
    x = x_ref[0]
    a = jnp.dot(x, w1_ref[0], preferred_element_type=jnp.float32)
    b = jnp.dot(x, w3_ref[0], preferred_element_type=jnp.float32)
    hid = (a * _sigmoid(a) * b).astype(jnp.bfloat16)
    part = jnp.dot(hid, w2_ref[0], preferred_element_type=jnp.float32)

    @pl.when(f == 0)
    def _():
        acc_ref[...] = part

    @pl.when(f > 0)
    def _():
        acc_ref[...] = acc_ref[...] + part

    @pl.when(f == pl.num_programs(2) - 1)
    def _():
        rows = o_ref.shape[1]
        gl = g_ref[0]
        lane = lax.broadcasted_iota(jnp.int32, gl.shape, 1)
        mine = ((lane & (N_EXPERTS - 1)) == e) & (lane < 3 * N_EXPERTS)
        gate = jnp.sum(jnp.where(mine, gl, 0.0), axis=-1, keepdims=True)
        slot = pl.program_id(1) * rows + lax.broadcasted_iota(jnp.int32, (rows, 1), 0)
        o_ref[0] = jnp.where(slot < cap, acc_ref[...] * gate, 0.0).astype(o_ref.dtype)


def expert_ffn(xe, ge, w1, w3, w2, cap):
    e, cap_pad, _ = xe.shape
    return pl.pallas_call(
        functools.partial(_ffn_kernel, cap=cap),
        grid=(e, cap_pad // FFN_ROWS, EXPERT_FF // FFN_COLS),
        in_specs=[pl.BlockSpec((1, FFN_ROWS, D_MODEL), lambda i, m, f: (i, m, 0)),
                  pl.BlockSpec((1, D_MODEL, FFN_COLS), lambda i, m, f: (i, 0, f)),
                  pl.BlockSpec((1, D_MODEL, FFN_COLS), lambda i, m, f: (i, 0, f)),
                  pl.BlockSpec((1, FFN_COLS, D_MODEL), lambda i, m, f: (i, f, 0)),
                  pl.BlockSpec((1, FFN_ROWS, LANES), lambda i, m, f: (i, m, 0))],
        out_specs=pl.BlockSpec((1, FFN_ROWS, D_MODEL), lambda i, m, f: (i, m, 0)),
        out_shape=jax.ShapeDtypeStruct((e, cap_pad, D_MODEL), jnp.bfloat16),
        scratch_shapes=[pltpu.VMEM((FFN_ROWS, D_MODEL), jnp.float32)],
        compiler_params=_cparams("parallel", "parallel", "arbitrary"),
        name="expert_ffn",
    )(xe, w1, w3, w2, ge)


def _combine_kernel(st_ref, ye_ref, ptok_ref, h_ref, g_ref, b_ref, ho_ref, hb_ref, sy_ref, acc_ref, sem_ref, *, lp, seq_len):
    ti = pl.program_id(0)
    t = TOK_TILE

    @pl.when(ti == 0)
    def _():
        sy_ref[...] = jnp.zeros(sy_ref.shape, sy_ref.dtype)

    plan, k_tot = _tile_plan(st_ref, ti)

    def chunk_copy(e, s16, off, j):
        src = pl.multiple_of(s16 + j * DMA_ROWS, BF16_ROWS)
        dst = pl.multiple_of(off + j * DMA_ROWS, DMA_ROWS)
        return pltpu.make_async_copy(ye_ref.at[e, pl.ds(src, DMA_ROWS)], sy_ref.at[pl.ds(dst, DMA_ROWS)], sem_ref.at[0])

    total = jnp.int32(0)
    for e, (s16, nch, off, end) in enumerate(plan):
        def issue(j, c, e=e, s16=s16, off=off):
            chunk_copy(e, s16, off, j).start()
            return c
        lax.fori_loop(0, nch, issue, 0)
        total = total + nch

    lane = lax.broadcasted_iota(jnp.int32, (1, LANES), 1)
    delta = jnp.zeros((1, LANES), jnp.int32)
    for e, (s16, nch, off, end) in enumerate(plan):
        delta = jnp.where(lane == e, off - s16, delta)
    p = ptok_ref[...]
    srow = jnp.where(p >= 0, p + delta, -1)
    acc_ref[...] = jnp.zeros(acc_ref.shape, jnp.float32)

    def wait_one(_, c):
        chunk_copy(0, 0, 0, 0).wait()
        return c
    lax.fori_loop(0, total, wait_one, 0)

    def gather(kc, c):
        k0 = pl.multiple_of(kc * SEQ_ALIGN, SEQ_ALIGN)
        k = k0 + lax.broadcasted_iota(jnp.int32, (1, SEQ_ALIGN), 1)
        onehot = jnp.zeros((t, SEQ_ALIGN), jnp.float32)
        for e in range(N_EXPERTS):
            onehot = jnp.where(srow[:, e:e + 1] == k, 1.0, onehot)
        acc_ref[...] = acc_ref[...] + jnp.dot(onehot.astype(jnp.bfloat16), sy_ref[pl.ds(k0, SEQ_ALIGN), :],
                                              preferred_element_type=jnp.float32)
        return c

    lax.fori_loop(0, lax.shift_right_logical(k_tot + SEQ_ALIGN - 1, SEQ_ALIGN.bit_length() - 1), gather, 0)
    tsum = ALPHA * h_ref[...] + acc_ref[...]
    _norm_route(tsum, g_ref, b_ref, None, ti * t, lp, seq_len, ho_ref, hb_ref, None, None)


def combine(starts, ye, ptok, h, g, bb, lp, seq_len):
    n = h.shape[0]
    nt = n // TOK_TILE
    row = lambda i, st: (i, 0)
    fix = lambda i, st: (0, 0)
    grid_spec = pltpu.PrefetchScalarGridSpec(
        num_scalar_prefetch=1,
        grid=(nt,),
        in_specs=[pl.BlockSpec(memory_space=pl.ANY), pl.BlockSpec((TOK_TILE, LANES), row),
                  pl.BlockSpec((TOK_TILE, D_MODEL), row), pl.BlockSpec((1, D_MODEL), fix), pl.BlockSpec((1, D_MODEL), fix)],
        out_specs=[pl.BlockSpec((TOK_TILE, D_MODEL), row), pl.BlockSpec((TOK_TILE, D_MODEL), row)],
        scratch_shapes=[pltpu.VMEM((_stage_max(), D_MODEL), jnp.bfloat16), pltpu.VMEM((TOK_TILE, D_MODEL), jnp.float32),
                        pltpu.SemaphoreType.DMA((1,))],
    )
    return pl.pallas_call(
        functools.partial(_combine_kernel, lp=lp, seq_len=seq_len),
        grid_spec=grid_spec,
        out_shape=[jax.ShapeDtypeStruct((n, D_MODEL), jnp.float32), jax.ShapeDtypeStruct((n, D_MODEL), jnp.bfloat16)],
        compiler_params=_cparams("arbitrary"),
        name="moe_combine",
    )(starts, ye, ptok, h, g, bb)


def moe_block(h, hb, aff, afft, w1, w3, w2, g, bb, n_real, lp, seq_len):
    cap = max(1, CAPACITY_FACTOR * n_real // N_EXPERTS)
    cap_pad = -(-(cap + DMA_ROWS) // FFN_ROWS) * FFN_ROWS
    pos, ptok, start3 = route(afft, cap)
    starts = start3[:, 0, :N_EXPERTS].reshape(-1)
    xe, ge = dispatch(starts, hb, pos, aff, cap, cap_pad)
    ye = expert_ffn(xe, ge, w1, w3, w2, cap)
    return combine(starts, ye, ptok, h, g, bb, lp, seq_len)


def _prep_weights(meta, w_in, conv_w, conv_b, a_log_f, a_log_b, dt_bias_f, dt_bias_b, d_skip, ssd_norm_w,
                  lam_q1, lam_k1, lam_q2, lam_k2, subln_w, rel_bias, w_out, pool_w, pool_b, pool_scale,
                  ln1_g, ln1_b, ln2_g, ln2_b, w_router, w1, w3, w2):
    bf = jnp.bfloat16
    f32 = jnp.float32
    w = w_in[0]
    o = [0, SSD_INNER, SSD_INNER + SSD_CONV_CH, SSD_INNER + SSD_CONV_CH + 2 * SSD_HEADS]
    wz, wxbc, wdt = w[:, o[0]:o[1]], w[:, o[1]:o[2]], w[:, o[2]:o[3]]
    wq, wk, wv = (w[:, o[3] + i * ATT_QK:o[3] + (i + 1) * ATT_QK] for i in range(3))
    pad = jnp.zeros((D_MODEL, _PROJ_COLS - _COL_DT - 2 * SSD_HEADS), f32)
    w_tok = jnp.concatenate([wz, wxbc, wdt, pad], axis=1).astype(bf)
    w_qv = jnp.concatenate([wq * (ATT_HEAD_DIM ** -0.5 * LOG2E), wv], axis=1).T.astype(bf)
    lane_pad = lambda v: jnp.pad(v.astype(f32), (0, LANES - v.shape[0]))
    par = jnp.stack([lane_pad(jnp.concatenate([dt_bias_f[0], dt_bias_b[0]])),
                     lane_pad(jnp.concatenate([a_log_f[0], a_log_b[0]]))] + [jnp.zeros((LANES,), f32)] * 6)
    lam4 = jnp.stack([lane_pad(lam_q1[0]), lane_pad(lam_k1[0]), lane_pad(lam_q2[0]), lane_pad(lam_k2[0])]
                     + [jnp.zeros((LANES,), f32)] * 4)
    return dict(
        meta=meta, w_tok=w_tok, w_k=wk.astype(bf), w_qv=w_qv, conv_w=conv_w[0, :, 0, :], conv_b=conv_b[0][None], par=par,
        dsk=jnp.repeat(d_skip[0], SSD_HEAD_DIM)[None], nw=ssd_norm_w[0][None], lam4=lam4,
        subw=jnp.broadcast_to(subln_w[0][:, None], (ATT_V_DIM, SEQ_ALIGN)), rel_bias=rel_bias,
        w_out=w_out[0].astype(bf), pool_w=pool_w[0].astype(bf), pool_b=pool_b[0][None], pool_scale=pool_scale[0][None],
        ln1_g=ln1_g, ln1_b=ln1_b, ln2_g=ln2_g, ln2_b=ln2_b,
        w_r=jnp.pad(w_router, ((0, 0), (0, 0), (0, LANES - N_EXPERTS))).astype(bf),
        w1=w1.astype(bf), w3=w3.astype(bf), w2=w2.astype(bf))


def _trunk(x, p, bias5):
    b, s, _ = x.shape
    seq_len = s + N_META
    lp = -(-seq_len // SEQ_ALIGN) * SEQ_ALIGN
    n = b * lp
    n_real = b * seq_len
    meta = jnp.broadcast_to(p["meta"][None].astype(x.dtype), (b, N_META, D_MODEL))
    h = jnp.concatenate([meta, x, jnp.zeros((b, lp - seq_len, D_MODEL), x.dtype)], axis=1).reshape(n, D_MODEL)
    hb = h.astype(jnp.bfloat16)
    lambda_init = 0.8 - 0.6 * math.exp(-0.3 * 0)

    proj = mm_nn(hb, p["w_tok"], jnp.float32, ROW_TILE, 512)
    k3 = mm_nn(hb, p["w_k"], jnp.bfloat16, ROW_TILE, 512).reshape(b, lp, ATT_QK)
    qv = mm_nt_blocked(p["w_qv"], hb, jnp.bfloat16, ROW_TILE, 512)
    proj3 = proj.reshape(b, lp, _PROJ_COLS)
    xbc = conv_silu(proj3, p["conv_w"], p["conv_b"]).reshape(n, SSD_CONV_CH)
    yf = ssd_scan(xbc, proj, p["par"], b, lp, seq_len, False)
    yb = ssd_scan(xbc, proj, p["par"], b, lp, seq_len, True)
    y_ssd = ssd_gate(yf, yb, xbc, proj, p["dsk"], p["nw"], ROW_TILE)
    y_att = diff_attention(qv, k3, bias5, p["lam4"], p["subw"], b, lp, seq_len, lambda_init)
    h, hb, aff, afft = out_proj_norm_route(y_ssd, y_att, p["w_out"], h, p["ln1_g"][0][None], p["ln1_b"][0][None],
                                           p["w_r"][0], lp, seq_len)
    h, hb = moe_block(h, hb, aff, afft, p["w1"][0], p["w3"][0], p["w2"][0], p["ln2_g"][0][None], p["ln2_b"][0][None],
                      n_real, lp, seq_len)
    d = pool_diff(h.reshape(b, lp, D_MODEL), seq_len).reshape(n, D_MODEL)
    h, hb, aff, afft = pool_proj_norm_route(d, p["pool_w"], p["pool_b"], p["pool_scale"], h, p["ln1_g"][1][None],
                                            p["ln1_b"][1][None], p["w_r"][1], lp, seq_len)
    h, hb = moe_block(h, hb, aff, afft, p["w1"][1], p["w3"][1], p["w2"][1], p["ln2_g"][1][None], p["ln2_b"][1][None],
                      n_real, lp, seq_len)
    return h.reshape(b, lp, D_MODEL)[:, N_META:seq_len]


def kernel(x_prompt, x_sample, meta, w_in, conv_w, conv_b, a_log_f, a_log_b, dt_bias_f, dt_bias_b, d_skip, ssd_norm_w, lam_q1, lam_k1, lam_q2, lam_k2, subln_w, rel_bias, w_out, pool_w, pool_b, pool_scale, ln1_g, ln1_b, ln2_g, ln2_b, w_router, w1, w3, w2):
    p = _prep_weights(meta, w_in, conv_w, conv_b, a_log_f, a_log_b, dt_bias_f, dt_bias_b, d_skip, ssd_norm_w,
                      lam_q1, lam_k1, lam_q2, lam_k2, subln_w, rel_bias, w_out, pool_w, pool_b, pool_scale,
                      ln1_g, ln1_b, ln2_g, ln2_b, w_router, w1, w3, w2)
    bias5 = bias_tiles(rel_bias)
    return (_trunk(x_prompt, p, bias5), _trunk(x_sample, p, bias5))
```

```python
import functools
import math

import jax
import jax.numpy as jnp
from jax import lax
from jax.experimental import pallas as pl
from jax.experimental.pallas import tpu as pltpu

D_MODEL = 2048
DEPTH = 2
N_META = 16
SSD_HEADS = 16
SSD_HEAD_DIM = 64
SSD_INNER = SSD_HEADS * SSD_HEAD_DIM
SSD_GROUPS = 4
SSD_STATE = 128
SSD_CONV = 5
SSD_CHUNK = 128
SSD_CONV_CH = SSD_INNER + 2 * SSD_GROUPS * SSD_STATE
ATT_HEADS = 8
ATT_HEAD_DIM = 64
ATT_V_DIM = 2 * ATT_HEAD_DIM
ATT_QK = ATT_HEADS * 2 * ATT_HEAD_DIM
ATT_INNER = ATT_HEADS * ATT_V_DIM
REL_BUCKETS = 32
POOL_WINDOWS = (2, 4, 8, 16)
POOL_GROUP = D_MODEL // len(POOL_WINDOWS)
N_EXPERTS = 16
EXPERT_FF = 2048
CAPACITY_FACTOR = 2
ALPHA = (2 * DEPTH) ** 0.25
LN_EPS = 1e-5

LANES = 128
BF16_ROWS = 16
SEQ_ALIGN = 256
TOK_TILE = 256
ROW_TILE = 1024
DMA_ROWS = 32
FFN_ROWS = 704
FFN_COLS = 512
VMEM_LIMIT = 56 * 1024 * 1024

_BUCKET_STARTS = (1, 2, 3, 4, 5, 6, 7, 8, 12, 16, 23, 32, 46, 64, 91)
_COL_Z, _COL_XBC, _COL_DT, _PROJ_COLS = 0, 1024, 3072, 3584
ATT_CHUNK = 4
LOG2E = math.log2(math.e)


def _cparams(*sem):
    return pltpu.CompilerParams(dimension_semantics=sem, vmem_limit_bytes=VMEM_LIMIT)


def _sigmoid(x):
    return 1.0 / (1.0 + jnp.exp(-x))


def _softplus(x):
    return jnp.maximum(x, 0.0) + jnp.log(1.0 + jnp.exp(-jnp.abs(x)))


def _split3(x):
    p1 = x.astype(jnp.bfloat16)
    r1 = x - p1.astype(jnp.float32)
    p2 = r1.astype(jnp.bfloat16)
    p3 = (r1 - p2.astype(jnp.float32)).astype(jnp.bfloat16)
    return p1, p2, p3


def _mm_nn_kernel(x_ref, w_ref, o_ref):
    o_ref[...] = jnp.dot(x_ref[...], w_ref[...], preferred_element_type=jnp.float32).astype(o_ref.dtype)


def mm_nn(x, w, out_dtype, tm, tn):
    n, k = x.shape
    nn = w.shape[1]
    return pl.pallas_call(
        _mm_nn_kernel,
        grid=(n // tm, nn // tn),
        in_specs=[pl.BlockSpec((tm, k), lambda i, j: (i, 0)), pl.BlockSpec((k, tn), lambda i, j: (0, j))],
        out_specs=pl.BlockSpec((tm, tn), lambda i, j: (i, j)),
        out_shape=jax.ShapeDtypeStruct((n, nn), out_dtype),
        compiler_params=_cparams("parallel", "arbitrary"),
        name="mm_nn",
    )(x, w)


def _mm_nt_kernel(w_ref, x_ref, o_ref):
    r = lax.dot_general(w_ref[...], x_ref[...], (((1,), (1,)), ((), ())), preferred_element_type=jnp.float32)
    for s in range(o_ref.shape[0]):
        o_ref[s] = r[:, s * SEQ_ALIGN:(s + 1) * SEQ_ALIGN].astype(o_ref.dtype)


def mm_nt_blocked(wt, x, out_dtype, tm, tn):
    n, k = x.shape
    nn = wt.shape[0]
    per = tm // SEQ_ALIGN
    return pl.pallas_call(
        _mm_nt_kernel,
        grid=(n // tm, nn // tn),
        in_specs=[pl.BlockSpec((tn, k), lambda i, j: (j, 0)), pl.BlockSpec((tm, k), lambda i, j: (i, 0))],
        out_specs=pl.BlockSpec((per, tn, SEQ_ALIGN), lambda i, j: (i, j, 0)),
        out_shape=jax.ShapeDtypeStruct((n // SEQ_ALIGN, nn, SEQ_ALIGN), out_dtype),
        compiler_params=_cparams("parallel", "arbitrary"),
        name="mm_nt",
    )(wt, x)


def _conv_kernel(x_ref, w_ref, b_ref, o_ref, pad_ref, *, rows):
    lp = x_ref.shape[1]
    halo = 8
    pad_ref[0:halo, :] = jnp.zeros((halo, LANES), jnp.float32)
    pad_ref[lp + halo:lp + 2 * halo, :] = jnp.zeros((halo, LANES), jnp.float32)
    pad_ref[halo:lp + halo, :] = x_ref[0]
    for t0 in range(0, lp, rows):
        acc = jnp.broadcast_to(b_ref[...], (rows, LANES))
        for k in range(SSD_CONV):
            off = t0 + halo - SSD_CONV // 2 + k
            acc = acc + pad_ref[off:off + rows, :] * w_ref[k:k + 1, :]
        o_ref[0, t0:t0 + rows, :] = acc * _sigmoid(acc)


def conv_silu(proj3, conv_w, conv_b):
    b, lp, _ = proj3.shape
    c0 = _COL_XBC // LANES
    return pl.pallas_call(
        functools.partial(_conv_kernel, rows=SEQ_ALIGN),
        grid=(b, SSD_CONV_CH // LANES),
        in_specs=[
            pl.BlockSpec((1, lp, LANES), lambda i, j: (i, 0, c0 + j)),
            pl.BlockSpec((SSD_CONV, LANES), lambda i, j: (0, j)),
            pl.BlockSpec((1, LANES), lambda i, j: (0, j)),
        ],
        out_specs=pl.BlockSpec((1, lp, LANES), lambda i, j: (i, 0, j)),
        out_shape=jax.ShapeDtypeStruct((b, lp, SSD_CONV_CH), jnp.float32),
        scratch_shapes=[pltpu.VMEM((lp + 16, LANES), jnp.float32)],
        compiler_params=_cparams("parallel", "parallel"),
        name="conv_silu",
    )(proj3, conv_w, conv_b)


def _ssd_kernel(x_ref, b_ref, c_ref, dt_ref, par_ref, y_ref, h_ref, *, seq_len, reverse):
    cs = SSD_CHUNK
    step = pl.program_id(1)
    chunk = (pl.num_programs(1) - 1 - step) if reverse else step
    lane0 = SSD_HEADS if reverse else 0

    @pl.when(step == 0)
    def _():
        h_ref[...] = jnp.zeros(h_ref.shape, jnp.float32)

    sub = lax.broadcasted_iota(jnp.int32, (cs, cs), 0)
    lan = lax.broadcasted_iota(jnp.int32, (cs, cs), 1)
    valid = (chunk * cs + sub[:, 0:1]) < seq_len
    dt = jnp.where(valid, _softplus(dt_ref[...] + par_ref[0:1, :]), 0.0)
    a = dt * -jnp.exp(par_ref[1:2, :])
    a_t = a.T[0:2 * SSD_HEADS, :]
    if reverse:
        incl, rest = (sub >= lan), (sub < lan)
    else:
        incl, rest = (sub <= lan), (sub > lan)
    tri = jnp.concatenate([incl.astype(jnp.bfloat16), rest.astype(jnp.bfloat16),
                           jnp.ones((cs, cs), jnp.bfloat16)], axis=1)
    pieces = jnp.concatenate(_split3(a_t), axis=0)
    sums = jnp.dot(pieces, tri, preferred_element_type=jnp.float32)
    nh2 = 2 * SSD_HEADS
    sums = sums[0:nh2] + sums[nh2:2 * nh2] + sums[2 * nh2:3 * nh2]
    cum_row, rest_row, tot_row = sums[:, 0:cs], sums[:, cs:2 * cs], sums[:, 2 * cs:3 * cs]
    cum_col = jnp.concatenate([cum_row, jnp.zeros((cs - nh2, cs), jnp.float32)], axis=0).T
    mask = (sub <= lan) if reverse else (sub >= lan)
    left = lan[0:1, :] < SSD_HEAD_DIM

    for g in range(SSD_GROUPS):
        bg = b_ref[:, g * SSD_STATE:(g + 1) * SSD_STATE]
        cg = c_ref[:, g * SSD_STATE:(g + 1) * SSD_STATE]
        cb = lax.dot_general(cg.astype(jnp.bfloat16), bg.astype(jnp.bfloat16), (((1,), (1,)), ((), ())),
                             preferred_element_type=jnp.float32)
        bg_t = bg.T
        for pp in range(2):
            p = 2 * g + pp
            r0, r1 = lane0 + 2 * p, lane0 + 2 * p + 1
            dtp = jnp.where(left, dt[:, r0:r0 + 1], dt[:, r1:r1 + 1])
            xs = (x_ref[:, p * LANES:(p + 1) * LANES] * dtp).astype(jnp.bfloat16)
            h_t = h_ref[p]
            rhs = jnp.concatenate([xs, h_t.astype(jnp.bfloat16)], axis=0)
            ys, ss = [], []
            for r in (r0, r1):
                cc = cum_col[:, r:r + 1]
                dec = jnp.exp(jnp.where(mask, cc - cum_row[r:r + 1, :], -jnp.inf))
                lhs = jnp.concatenate([(cb * dec).astype(jnp.bfloat16),
                                       (cg * jnp.exp(cc)).astype(jnp.bfloat16)], axis=1)
                ys.append(jnp.dot(lhs, rhs, preferred_element_type=jnp.float32))
                bw = (bg_t * jnp.exp(rest_row[r:r + 1, :])).astype(jnp.bfloat16)
                ss.append(jnp.dot(bw, xs, preferred_element_type=jnp.float32))
            y_ref[:, p * LANES:(p + 1) * LANES] = jnp.where(left, ys[0], ys[1])
            decay = jnp.where(left, jnp.exp(tot_row[r0:r0 + 1, :]), jnp.exp(tot_row[r1:r1 + 1, :]))
            h_ref[p] = decay * h_t + jnp.where(left, ss[0], ss[1])


def ssd_scan(xbc, proj, par, b, lp, seq_len, reverse):
    n = xbc.shape[0]
    nc = lp // SSD_CHUNK
    cdt = _COL_DT // LANES

    def row(i, c):
        return i * nc + ((nc - 1 - c) if reverse else c)

    return pl.pallas_call(
        functools.partial(_ssd_kernel, seq_len=seq_len, reverse=reverse),
        grid=(b, nc),
        in_specs=[
            pl.BlockSpec((SSD_CHUNK, SSD_INNER), lambda i, c: (row(i, c), 0)),
            pl.BlockSpec((SSD_CHUNK, SSD_GROUPS * SSD_STATE), lambda i, c: (row(i, c), 2)),
            pl.BlockSpec((SSD_CHUNK, SSD_GROUPS * SSD_STATE), lambda i, c: (row(i, c), 3)),
            pl.BlockSpec((SSD_CHUNK, LANES), lambda i, c: (row(i, c), cdt)),
            pl.BlockSpec((8, LANES), lambda i, c: (0, 0)),
        ],
        out_specs=pl.BlockSpec((SSD_CHUNK, SSD_INNER), lambda i, c: (row(i, c), 0)),
        out_shape=jax.ShapeDtypeStruct((n, SSD_INNER), jnp.float32),
        scratch_shapes=[pltpu.VMEM((SSD_HEADS // 2, SSD_STATE, LANES), jnp.float32)],
        compiler_params=_cparams("parallel", "arbitrary"),
        name="ssd_bwd" if reverse else "ssd_fwd",
    )(xbc, xbc, xbc, proj, par)


def _gate_kernel(yf_ref, yb_ref, x_ref, z_ref, dsk_ref, nw_ref, o_ref):
    z = z_ref[...]
    y = (yf_ref[...] + yb_ref[...] + dsk_ref[...] * x_ref[...]) * (z * _sigmoid(z))
    gw = SSD_INNER // SSD_GROUPS
    for g in range(SSD_GROUPS):
        seg = y[:, g * gw:(g + 1) * gw]
        ms = jnp.mean(seg * seg, axis=-1, keepdims=True)
        o_ref[:, g * gw:(g + 1) * gw] = (seg * lax.rsqrt(ms + LN_EPS) * nw_ref[:, g * gw:(g + 1) * gw]).astype(o_ref.dtype)


def ssd_gate(yf, yb, xbc, proj, dsk, nw, tm):
    n = yf.shape[0]
    blk = lambda i: (i, 0)
    return pl.pallas_call(
        _gate_kernel,
        grid=(n // tm,),
        in_specs=[pl.BlockSpec((tm, SSD_INNER), blk)] * 4 + [pl.BlockSpec((1, SSD_INNER), lambda i: (0, 0))] * 2,
        out_specs=pl.BlockSpec((tm, SSD_INNER), blk),
        out_shape=jax.ShapeDtypeStruct((n, SSD_INNER), jnp.bfloat16),
        compiler_params=_cparams("parallel"),
        name="ssd_gate",
    )(yf, yb, xbc, proj, dsk, nw)


def _bias_kernel(rb_ref, o_ref):
    h = pl.program_id(0)
    d = pl.program_id(1) - 2
    t = SEQ_ALIGN
    kk = lax.broadcasted_iota(jnp.int32, (t, t), 0)
    qq = lax.broadcasted_iota(jnp.int32, (t, t), 1)
    rel = d * t + kk - qq
    n = jnp.abs(rel)
    half = REL_BUCKETS // 2
    vneg = jnp.full((t, t), rb_ref[0, h], jnp.float32)
    vpos = jnp.full((t, t), rb_ref[half, h], jnp.float32)
    for j, start in enumerate(_BUCKET_STARTS, start=1):
        ge = n >= start
        vneg = jnp.where(ge, rb_ref[j, h], vneg)
        vpos = jnp.where(ge, rb_ref[half + j, h], vpos)
    o_ref[0, 0] = jnp.where(rel > 0, vpos, vneg) * LOG2E


def bias_tiles(rel_bias):
    return pl.pallas_call(
        _bias_kernel,
        grid=(ATT_HEADS, 5),
        in_specs=[pl.BlockSpec(memory_space=pltpu.SMEM)],
        out_specs=pl.BlockSpec((1, 1, SEQ_ALIGN, SEQ_ALIGN), lambda h, d: (h, d, 0, 0)),
        out_shape=jax.ShapeDtypeStruct((ATT_HEADS, 5, SEQ_ALIGN, SEQ_ALIGN), jnp.float32),
        compiler_params=_cparams("parallel", "parallel"),
        name="bias_tiles",
    )(rel_bias)


def _attn_kernel(q_ref, v_ref, k_ref, bias_ref, lam_ref, sub_ref, o_ref, s_ref, *, seq_len, lambda_init):
    t = SEQ_ALIGN
    group = ATT_CHUNK
    qb = pl.program_id(2)
    q_t = q_ref[0]
    sub = lax.broadcasted_iota(jnp.int32, q_t.shape, 0)
    zero = jnp.zeros_like(q_t)
    qs = (jnp.where(sub < ATT_HEAD_DIM, q_t, zero), jnp.where(sub >= ATT_HEAD_DIM, q_t, zero))
    n_full = seq_len // t
    n_grp = n_full // group
    tail = seq_len - n_full * t
    tail_rows = -(-tail // LANES) * LANES

    def fold(x, op):
        return op(x.reshape(x.shape[0] // 8, 8, t), axis=0)

    def score_block(kb, rows, mx):
        r0 = kb * t
        if not isinstance(kb, int):
            r0 = pl.multiple_of(r0, t)
        kblk = k_ref[0, pl.ds(r0, rows), :]
        bias = bias_ref[0, jnp.clip(kb - qb, -2, 2) + 2, 0:rows, :]
        out = []
        for mi in range(2):
            s = jnp.dot(kblk, qs[mi], preferred_element_type=jnp.float32) + bias
            if rows != t:
                s = jnp.where(lax.broadcasted_iota(jnp.int32, (rows, 1), 0) < tail, s, -jnp.inf)
            s_ref[mi, pl.ds(r0, rows), :] = s
            out.append(jnp.maximum(mx[mi], fold(s, jnp.max)))
        return tuple(out)

    neg = jnp.full((8, t), -jnp.inf, jnp.float32)

    def score_group(g, mx):
        for j in range(group):
            mx = score_block(g * group + j, t, mx)
        return mx

    mx = lax.fori_loop(0, n_grp, score_group, (neg, neg))
    for kb in range(n_grp * group, n_full):
        mx = score_block(kb, t, mx)
    if tail:
        mx = score_block(n_full, tail_rows, mx)
    ms_ = [jnp.max(m, axis=0, keepdims=True) for m in mx]

    def prob_blocks(kbs, rows, carry):
        ls, accs = list(carry[0:2]), list(carry[2:4])
        vcat = jnp.concatenate([v_ref[kb][:, 0:rows] for kb in kbs], axis=1) if len(kbs) > 1 else v_ref[kbs[0]][:, 0:rows]
        for mi in range(2):
            ps = []
            for kb in kbs:
                r0 = kb * t
                if not isinstance(kb, int):
                    r0 = pl.multiple_of(r0, t)
                p = jnp.exp2(s_ref[mi, pl.ds(r0, rows), :] - ms_[mi])
                ls[mi] = ls[mi] + fold(p, jnp.sum)
                ps.append(p.astype(jnp.bfloat16))
            pcat = jnp.concatenate(ps, axis=0) if len(ps) > 1 else ps[0]
            accs[mi] = accs[mi] + jnp.dot(vcat, pcat, preferred_element_type=jnp.float32)
        return tuple(ls) + tuple(accs)

    z8 = jnp.zeros((8, t), jnp.float32)
    zacc = jnp.zeros((ATT_V_DIM, t), jnp.float32)
    carry = lax.fori_loop(0, n_grp, lambda g, c: prob_blocks([g * group + j for j in range(group)], t, c),
                          (z8, z8, zacc, zacc))
    for kb in range(n_grp * group, n_full):
        carry = prob_blocks([kb], t, carry)
    if tail:
        carry = prob_blocks([n_full], tail_rows, carry)
    l1 = jnp.sum(carry[0], axis=0, keepdims=True)
    l2 = jnp.sum(carry[1], axis=0, keepdims=True)
    lv = lam_ref[...]
    lam = (jnp.exp(jnp.sum(lv[0:1] * lv[1:2], axis=-1, keepdims=True))
           - jnp.exp(jnp.sum(lv[2:3] * lv[3:4], axis=-1, keepdims=True)) + lambda_init)
    o = carry[2] * (1.0 / l1) - lam * (carry[3] * (1.0 / l2))
    ms = jnp.mean(o * o, axis=0, keepdims=True)
    o = o * lax.rsqrt(ms + LN_EPS) * sub_ref[...] * (1.0 - lambda_init)
    o_ref[...] = o.T.astype(o_ref.dtype)


def diff_attention(qv, k3, bias5, lam4, subw, b, lp, seq_len, lambda_init):
    nq = lp // SEQ_ALIGN
    return pl.pallas_call(
        functools.partial(_attn_kernel, seq_len=seq_len, lambda_init=lambda_init),
        grid=(b, ATT_HEADS, nq),
        in_specs=[
            pl.BlockSpec((1, ATT_V_DIM, SEQ_ALIGN), lambda i, h, q: (i * nq + q, h, 0)),
            pl.BlockSpec((nq, ATT_V_DIM, SEQ_ALIGN), lambda i, h, q: (i, ATT_HEADS + h, 0)),
            pl.BlockSpec((1, lp, LANES), lambda i, h, q: (i, 0, h)),
            pl.BlockSpec((1, 5, SEQ_ALIGN, SEQ_ALIGN), lambda i, h, q: (h, 0, 0, 0)),
            pl.BlockSpec((8, LANES), lambda i, h, q: (0, 0)),
            pl.BlockSpec((ATT_V_DIM, SEQ_ALIGN), lambda i, h, q: (0, 0)),
        ],
        out_specs=pl.BlockSpec((SEQ_ALIGN, ATT_V_DIM), lambda i, h, q: (i * nq + q, h)),
        out_shape=jax.ShapeDtypeStruct((b * lp, ATT_INNER), jnp.bfloat16),
        scratch_shapes=[pltpu.VMEM((2, lp, SEQ_ALIGN), jnp.float32)],
        compiler_params=_cparams("parallel", "parallel", "arbitrary"),
        name="diff_attn",
    )(qv, qv, k3, bias5, lam4, subw)


def _norm_route(t, g_ref, b_ref, wr_ref, tile_row0, lp, seq_len, h_ref, hb_ref, aff_ref, afft_ref):
    rows = t.shape[0]
    mu = jnp.mean(t, axis=-1, keepdims=True)
    tc = t - mu
    var = jnp.mean(tc * tc, axis=-1, keepdims=True)
    y = tc * lax.rsqrt(var + LN_EPS) * g_ref[...] + b_ref[...]
    pos = tile_row0 % lp + lax.broadcasted_iota(jnp.int32, (rows, 1), 0)
    valid = pos < seq_len
    y = jnp.where(valid, y, 0.0)
    h_ref[...] = y
    yb = y.astype(jnp.bfloat16)
    hb_ref[...] = yb
    if aff_ref is not None:
        logits = jnp.dot(yb, wr_ref[...], preferred_element_type=jnp.float32)
        lane = lax.broadcasted_iota(jnp.int32, logits.shape, 1)
        is_e = lane < N_EXPERTS
        logits = jnp.where(is_e, logits, -jnp.inf)
        e = jnp.exp(logits - jnp.max(logits, axis=-1, keepdims=True))
        aff = e / jnp.sum(e, axis=-1, keepdims=True)
        aff = jnp.where(valid, aff, jnp.where(is_e, -1.0, 0.0))
        aff_ref[...] = aff
        afft_ref[0] = aff.T[0:N_EXPERTS, :]


def _outproj_kernel(ys_ref, ya_ref, w_ref, h_ref, g_ref, b_ref, wr_ref, ho_ref, hb_ref, aff_ref, afft_ref, *, lp, seq_len):
    mix = (jnp.dot(ys_ref[...], w_ref[0:SSD_INNER, :], preferred_element_type=jnp.float32)
           + jnp.dot(ya_ref[...], w_ref[SSD_INNER:, :], preferred_element_type=jnp.float32))
    t = ALPHA * h_ref[...] + mix
    _norm_route(t, g_ref, b_ref, wr_ref, pl.program_id(0) * h_ref.shape[0], lp, seq_len, ho_ref, hb_ref, aff_ref, afft_ref)


def _poolproj_kernel(d_ref, w_ref, pb_ref, ps_ref, h_ref, g_ref, b_ref, wr_ref, ho_ref, hb_ref, aff_ref, afft_ref, *, lp, seq_len):
    outs = []
    for gi in range(len(POOL_WINDOWS)):
        outs.append(jnp.dot(d_ref[:, gi * POOL_GROUP:(gi + 1) * POOL_GROUP], w_ref[gi], preferred_element_type=jnp.float32))
    mix = (jnp.concatenate(outs, axis=-1) + pb_ref[...]) * ps_ref[...]
    t = ALPHA * h_ref[...] + mix
    _norm_route(t, g_ref, b_ref, wr_ref, pl.program_id(0) * h_ref.shape[0], lp, seq_len, ho_ref, hb_ref, aff_ref, afft_ref)


def _mix_outputs(n, tm):
    row = lambda i: (i, 0)
    specs = [pl.BlockSpec((tm, D_MODEL), row), pl.BlockSpec((tm, D_MODEL), row), pl.BlockSpec((tm, LANES), row),
             pl.BlockSpec((1, N_EXPERTS, tm), lambda i: (i, 0, 0))]
    shapes = [jax.ShapeDtypeStruct((n, D_MODEL), jnp.float32), jax.ShapeDtypeStruct((n, D_MODEL), jnp.bfloat16),
              jax.ShapeDtypeStruct((n, LANES), jnp.float32), jax.ShapeDtypeStruct((n // tm, N_EXPERTS, tm), jnp.float32)]
    return specs, shapes


def out_proj_norm_route(y_ssd, y_att, w_out, h, g, bb, w_r, lp, seq_len):
    n = h.shape[0]
    tm = TOK_TILE
    row = lambda i: (i, 0)
    fix = lambda i: (0, 0)
    specs, shapes = _mix_outputs(n, tm)
    return pl.pallas_call(
        functools.partial(_outproj_kernel, lp=lp, seq_len=seq_len),
        grid=(n // tm,),
        in_specs=[pl.BlockSpec((tm, SSD_INNER), row), pl.BlockSpec((tm, ATT_INNER), row),
                  pl.BlockSpec((D_MODEL, D_MODEL), fix), pl.BlockSpec((tm, D_MODEL), row),
                  pl.BlockSpec((1, D_MODEL), fix), pl.BlockSpec((1, D_MODEL), fix), pl.BlockSpec((D_MODEL, LANES), fix)],
        out_specs=specs, out_shape=shapes,
        compiler_params=_cparams("parallel"),
        name="outproj_ln_route",
    )(y_ssd, y_att, w_out, h, g, bb, w_r)


def pool_proj_norm_route(d, pool_w, pool_b, pool_scale, h, g, bb, w_r, lp, seq_len):
    n = h.shape[0]
    tm = TOK_TILE
    row = lambda i: (i, 0)
    fix = lambda i: (0, 0)
    specs, shapes = _mix_outputs(n, tm)
    return pl.pallas_call(
        functools.partial(_poolproj_kernel, lp=lp, seq_len=seq_len),
        grid=(n // tm,),
        in_specs=[pl.BlockSpec((tm, D_MODEL), row),
                  pl.BlockSpec((len(POOL_WINDOWS), POOL_GROUP, POOL_GROUP), lambda i: (0, 0, 0)),
                  pl.BlockSpec((1, D_MODEL), fix), pl.BlockSpec((1, D_MODEL), fix), pl.BlockSpec((tm, D_MODEL), row),
                  pl.BlockSpec((1, D_MODEL), fix), pl.BlockSpec((1, D_MODEL), fix), pl.BlockSpec((D_MODEL, LANES), fix)],
        out_specs=specs, out_shape=shapes,
        compiler_params=_cparams("parallel"),
        name="poolproj_ln_route",
    )(d, pool_w, pool_b, pool_scale, h, g, bb, w_r)


def _pool_kernel(x_ref, o_ref, s0_ref, s1_ref, *, seq_len):
    lp = x_ref.shape[1]
    halo = 16
    gi = pl.program_id(1) // (POOL_GROUP // LANES)
    zeros = jnp.zeros((halo, LANES), jnp.float32)
    for ref in (s0_ref, s1_ref):
        ref[0:halo, :] = zeros
        ref[lp + halo:lp + 2 * halo, :] = zeros
    s0_ref[halo:lp + halo, :] = x_ref[0]
    x = x_ref[0]
    t = lax.broadcasted_iota(jnp.int32, (lp, 1), 0)
    src, dst = s0_ref, s1_ref
    d = jnp.zeros((lp, LANES), jnp.float32)
    width = 1
    for wi, w in enumerate(POOL_WINDOWS):
        while width < w:
            dst[halo:lp + halo, :] = src[halo:lp + halo, :] + src[halo + width:lp + halo + width, :]
            dst[0:halo, :] = src[0:halo, :] + src[width:halo + width, :]
            src, dst = dst, src
            width *= 2
        win = src[halo - w // 2:lp + halo - w // 2, :]
        cnt = jnp.minimum(t + w // 2 - 1, seq_len - 1) - jnp.maximum(t - w // 2, 0) + 1
        mean = win / cnt.astype(jnp.float32)
        d = jnp.where(gi == wi, mean - x, d)
    o_ref[0] = jnp.where(t < seq_len, d, 0.0).astype(o_ref.dtype)


def pool_diff(h3, seq_len):
    b, lp, _ = h3.shape
    return pl.pallas_call(
        functools.partial(_pool_kernel, seq_len=seq_len),
        grid=(b, D_MODEL // LANES),
        in_specs=[pl.BlockSpec((1, lp, LANES), lambda i, j: (i, 0, j))],
        out_specs=pl.BlockSpec((1, lp, LANES), lambda i, j: (i, 0, j)),
        out_shape=jax.ShapeDtypeStruct((b, lp, D_MODEL), jnp.bfloat16),
        scratch_shapes=[pltpu.VMEM((lp + 32, LANES), jnp.float32)] * 2,
        compiler_params=_cparams("parallel", "parallel"),
        name="pool_diff",
    )(h3)


def _route_kernel(aff_ref, pos_ref, start_ref, *, cap):
    nt = aff_ref.shape[0]
    t = TOK_TILE

    def bits_of(v):
        return lax.bitcast_convert_type(v, jnp.int32)

    def count(mask):
        return jnp.sum(jnp.sum(jnp.where(mask, 1.0, 0.0), axis=0), axis=1, keepdims=True)

    def search(i, thr):
        cand = thr | jnp.left_shift(jnp.int32(1), 30 - i)
        cnt = count(bits_of(aff_ref[...]) >= cand[None])
        return jnp.where(cnt >= cap, cand, thr)

    thr = lax.fori_loop(0, 31, search, jnp.zeros((N_EXPERTS, 1), jnp.int32))
    need = cap - count(bits_of(aff_ref[...]) > thr[None])

    before = (lax.broadcasted_iota(jnp.int32, (t, t), 0) < lax.broadcasted_iota(jnp.int32, (t, t), 1)).astype(jnp.bfloat16)
    ones8 = jnp.ones((8, t), jnp.bfloat16)
    zpad = jnp.zeros((LANES - N_EXPERTS, t), jnp.float32)

    def tile(ti, carry):
        c_tie, c_sel, c_lane = carry
        b = bits_of(aff_ref[ti])
        tie = jnp.where(b == thr, 1.0, 0.0)
        rank = jnp.dot(tie.astype(jnp.bfloat16), before, preferred_element_type=jnp.float32) + c_tie
        sel = jnp.where(b > thr, 1.0, jnp.where(rank < need, tie, 0.0))
        slot = jnp.dot(sel.astype(jnp.bfloat16), before, preferred_element_type=jnp.float32) + c_sel
        pos_ref[ti] = jnp.where(sel > 0.0, slot, -1.0).astype(jnp.int32)
        start_ref[ti] = c_lane.astype(jnp.int32)
        selp = jnp.concatenate([sel, zpad], axis=0).astype(jnp.bfloat16)
        cnt_lane = lax.dot_general(ones8, selp, (((1,), (1,)), ((), ())), preferred_element_type=jnp.float32)
        return (c_tie + jnp.sum(tie, axis=1, keepdims=True), c_sel + jnp.sum(sel, axis=1, keepdims=True),
                c_lane + cnt_lane[0:1])

    zc = jnp.zeros((N_EXPERTS, 1), jnp.float32)
    _, _, c_lane = lax.fori_loop(0, nt, tile, (zc, zc, jnp.zeros((1, LANES), jnp.float32)))
    start_ref[nt] = c_lane.astype(jnp.int32)


def route(afft, cap):
    nt = afft.shape[0]
    return pl.pallas_call(
        functools.partial(_route_kernel, cap=cap),
        out_shape=[jax.ShapeDtypeStruct((nt, N_EXPERTS, TOK_TILE), jnp.int32),
                   jax.ShapeDtypeStruct((nt + 1, 1, LANES), jnp.int32)],
        compiler_params=pltpu.CompilerParams(vmem_limit_bytes=VMEM_LIMIT),
        name="route",
    )(afft)


HALF_EXPERTS = N_EXPERTS // 2
_REGION_MAX = -(-(TOK_TILE + BF16_ROWS - 1) // DMA_ROWS) * DMA_ROWS
HALF_ROWS = HALF_EXPERTS * _REGION_MAX + SEQ_ALIGN
XE_COLS = D_MODEL + LANES


def _tile_plan(st_ref, ti, half):
    plan = []
    off = jnp.int32(0)
    for e in range(half * HALF_EXPERTS, (half + 1) * HALF_EXPERTS):
        s = st_ref[ti * N_EXPERTS + e]
        end = st_ref[(ti + 1) * N_EXPERTS + e]
        s16 = s - (s & (BF16_ROWS - 1))
        nch = jnp.where(end > s, lax.shift_right_logical(end - s16 + DMA_ROWS - 1, DMA_ROWS.bit_length() - 1), 0)
        plan.append((e, s16, nch, off, end))
        off = off + nch * DMA_ROWS
    return plan, off


def _stage_rows(pos_half, plan):
    sub = lax.broadcasted_iota(jnp.int32, (HALF_EXPERTS, 1), 0)
    delta = jnp.zeros((HALF_EXPERTS, 1), jnp.int32)
    for i, (e, s16, nch, off, end) in enumerate(plan):
        delta = jnp.where(sub == i, off - s16, delta)
    return jnp.where(pos_half >= 0, pos_half + delta, -1)


def _onehot_rows(srow, k0):
    k = k0 + lax.broadcasted_iota(jnp.int32, (SEQ_ALIGN, 1), 0)
    onehot = jnp.zeros((SEQ_ALIGN, srow.shape[1]), jnp.float32)
    for i in range(srow.shape[0]):
        onehot = jnp.where(srow[i:i + 1, :] == k, 1.0, onehot)
    return onehot.astype(jnp.bfloat16)


def _n_blocks(rows):
    return lax.shift_right_logical(rows + SEQ_ALIGN - 1, SEQ_ALIGN.bit_length() - 1)


def _dispatch_kernel(st_ref, x_ref, pos_ref, aff_ref, xe_ref, sx_ref, cx_ref, n_ref, sem_ref, *, cap, cap_pad):
    ti = pl.program_id(0)
    nt = pl.num_programs(0)

    def chunk_copy(half, e, s16, off, j):
        src = pl.multiple_of(off + j * DMA_ROWS, DMA_ROWS)
        dst = pl.multiple_of(s16 + j * DMA_ROWS, BF16_ROWS)
        return pltpu.make_async_copy(sx_ref.at[half, pl.ds(src, DMA_ROWS)], xe_ref.at[e, pl.ds(dst, DMA_ROWS)],
                                     sem_ref.at[half])

    def wait_chunks(half, count):
        def body(_, c):
            chunk_copy(half, 0, 0, 0, 0).wait()
            return c
        lax.fori_loop(0, count, body, 0)

    tail0 = (cap // DMA_ROWS) * DMA_ROWS
    n_tail = (cap_pad - tail0) // DMA_ROWS

    @pl.when(ti == 0)
    def _():
        cx_ref[...] = jnp.zeros(cx_ref.shape, cx_ref.dtype)
        sx_ref[0, 0:DMA_ROWS, :] = jnp.zeros((DMA_ROWS, XE_COLS), sx_ref.dtype)
        for e in range(N_EXPERTS):
            for j in range(n_tail):
                pltpu.make_async_copy(sx_ref.at[0, pl.ds(0, DMA_ROWS)],
                                      xe_ref.at[e, pl.ds(tail0 + j * DMA_ROWS, DMA_ROWS)], sem_ref.at[0]).start()
        wait_chunks(0, N_EXPERTS * n_tail)
        n_ref[0] = 0
        n_ref[1] = 0

    aff = aff_ref[...]
    a1, a2, a3 = _split3(aff)
    gate3 = (a1.astype(jnp.float32) + pltpu.roll(a2.astype(jnp.float32), N_EXPERTS, 1)
             + pltpu.roll(a3.astype(jnp.float32), 2 * N_EXPERTS, 1)).astype(jnp.bfloat16)
    xt = jnp.concatenate([x_ref[...], gate3], axis=1)

    for half in range(2):
        plan, k_tot = _tile_plan(st_ref, ti, half)
        srow = _stage_rows(pos_ref[0, half * HALF_EXPERTS:(half + 1) * HALF_EXPERTS, :], plan)
        wait_chunks(half, n_ref[half])

        def permute(kc, c, half=half, srow=srow):
            k0 = pl.multiple_of(kc * SEQ_ALIGN, SEQ_ALIGN)
            sx_ref[half, pl.ds(k0, SEQ_ALIGN), :] = jnp.dot(_onehot_rows(srow, k0), xt,
                                                            preferred_element_type=jnp.float32).astype(sx_ref.dtype)
            return c

        lax.fori_loop(0, _n_blocks(k_tot), permute, 0)

        total = jnp.int32(0)
        for e, s16, nch, off, end in plan:
            @pl.when(nch > 0)
            def _(half=half, e=e, s16=s16, off=off, end=end):
                head = pl.ds(pl.multiple_of(off, BF16_ROWS), BF16_ROWS)
                sx_ref[half, head, :] = sx_ref[half, head, :] + cx_ref[e]
                part = end & (BF16_ROWS - 1)
                last = pl.ds(pl.multiple_of(off + (end - part) - s16, BF16_ROWS), BF16_ROWS)
                cx_ref[e] = jnp.where(part > 0, sx_ref[half, last, :], jnp.zeros((BF16_ROWS, XE_COLS), cx_ref.dtype))

            def issue(j, c, half=half, e=e, s16=s16, off=off):
                chunk_copy(half, e, s16, off, j).start()
                return c
            lax.fori_loop(0, nch, issue, 0)
            total = total + nch
        n_ref[half] = total

    @pl.when(ti == nt - 1)
    def _():
        wait_chunks(0, n_ref[0])
        wait_chunks(1, n_ref[1])


def dispatch(starts, hb, pos, aff, cap, cap_pad):
    n = hb.shape[0]
    nt = n // TOK_TILE
    grid_spec = pltpu.PrefetchScalarGridSpec(
        num_scalar_prefetch=1,
        grid=(nt,),
        in_specs=[pl.BlockSpec((TOK_TILE, D_MODEL), lambda i, st: (i, 0)),
                  pl.BlockSpec((1, N_EXPERTS, TOK_TILE), lambda i, st: (i, 0, 0)),
                  pl.BlockSpec((TOK_TILE, LANES), lambda i, st: (i, 0))],
        out_specs=pl.BlockSpec(memory_space=pl.ANY),
        scratch_shapes=[pltpu.VMEM((2, HALF_ROWS, XE_COLS), jnp.bfloat16),
                        pltpu.VMEM((N_EXPERTS, BF16_ROWS, XE_COLS), jnp.bfloat16),
                        pltpu.SMEM((2,), jnp.int32), pltpu.SemaphoreType.DMA((2,))],
    )
    return pl.pallas_call(
        functools.partial(_dispatch_kernel, cap=cap, cap_pad=cap_pad),
        grid_spec=grid_spec,
        out_shape=jax.ShapeDtypeStruct((N_EXPERTS, cap_pad, XE_COLS), jnp.bfloat16),
        compiler_params=_cparams("arbitrary"),
        name="moe_dispatch",
    )(starts, hb, pos, aff)


def _ffn_kernel(x_ref, w1_ref, w3_ref, w2_ref, o_ref, acc_ref, *, cap):
    e = pl.program_id(0)
    f = pl.program_id(2)
    x = x_ref[0, :, 0:D_MODEL]
    a = jnp.dot(x, w1_ref[0], preferred_element_type=jnp.float32)
    b = jnp.dot(x, w3_ref[0], preferred_element_type=jnp.float32)
    hid = (a * _sigmoid(a) * b).astype(jnp.bfloat16)
    part = jnp.dot(hid, w2_ref[0], preferred_element_type=jnp.float32)

    @pl.when(f == 0)
    def _():
        acc_ref[...] = part

    @pl.when(f > 0)
    def _():
        acc_ref[...] = acc_ref[...] + part

    @pl.when(f == pl.num_programs(2) - 1)
    def _():
        rows = o_ref.shape[1]
        gl = x_ref[0, :, D_MODEL:XE_COLS].astype(jnp.float32)
        lane = lax.broadcasted_iota(jnp.int32, gl.shape, 1)
        mine = ((lane & (N_EXPERTS - 1)) == e) & (lane < 3 * N_EXPERTS)
        gate = jnp.sum(jnp.where(mine, gl, 0.0), axis=-1, keepdims=True)
        slot = pl.program_id(1) * rows + lax.broadcasted_iota(jnp.int32, (rows, 1), 0)
        o_ref[0] = jnp.where(slot < cap, acc_ref[...] * gate, 0.0).astype(o_ref.dtype)


def expert_ffn(xe, w1, w3, w2, cap):
    e, cap_pad, _ = xe.shape
    return pl.pallas_call(
        functools.partial(_ffn_kernel, cap=cap),
        grid=(e, cap_pad // FFN_ROWS, EXPERT_FF // FFN_COLS),
        in_specs=[pl.BlockSpec((1, FFN_ROWS, XE_COLS), lambda i, m, f: (i, m, 0)),
                  pl.BlockSpec((1, D_MODEL, FFN_COLS), lambda i, m, f: (i, 0, f)),
                  pl.BlockSpec((1, D_MODEL, FFN_COLS), lambda i, m, f: (i, 0, f)),
                  pl.BlockSpec((1, FFN_COLS, D_MODEL), lambda i, m, f: (i, f, 0))],
        out_specs=pl.BlockSpec((1, FFN_ROWS, D_MODEL), lambda i, m, f: (i, m, 0)),
        out_shape=jax.ShapeDtypeStruct((e, cap_pad, D_MODEL), jnp.bfloat16),
        scratch_shapes=[pltpu.VMEM((FFN_ROWS, D_MODEL), jnp.float32)],
        compiler_params=_cparams("parallel", "parallel", "arbitrary"),
        name="expert_ffn",
    )(xe, w1, w3, w2)


def _combine_kernel(st_ref, ye_ref, pos_ref, h_ref, g_ref, b_ref, ho_ref, hb_ref, sy_ref, acc_ref, n_ref, sem_ref,
                    *, lp, seq_len):
    ti = pl.program_id(0)
    nt = pl.num_programs(0)
    t = TOK_TILE

    def chunk_copy(half, e, s16, off, j):
        src = pl.multiple_of(s16 + j * DMA_ROWS, BF16_ROWS)
        dst = pl.multiple_of(off + j * DMA_ROWS, DMA_ROWS)
        return pltpu.make_async_copy(ye_ref.at[e, pl.ds(src, DMA_ROWS)], sy_ref.at[half, pl.ds(dst, DMA_ROWS)],
                                     sem_ref.at[half])

    def fetch(half, tile):
        plan, _ = _tile_plan(st_ref, tile, half)
        total = jnp.int32(0)
        for e, s16, nch, off, end in plan:
            def issue(j, c, e=e, s16=s16, off=off):
                chunk_copy(half, e, s16, off, j).start()
                return c
            lax.fori_loop(0, nch, issue, 0)
            total = total + nch
        n_ref[half] = total

    @pl.when(ti == 0)
    def _():
        sy_ref[...] = jnp.zeros(sy_ref.shape, sy_ref.dtype)
        fetch(0, 0)
        fetch(1, 0)

    acc_ref[...] = jnp.zeros(acc_ref.shape, jnp.float32)
    for half in range(2):
        plan, k_tot = _tile_plan(st_ref, ti, half)
        srow = _stage_rows(pos_ref[0, half * HALF_EXPERTS:(half + 1) * HALF_EXPERTS, :], plan)

        def wait_one(_, c, half=half):
            chunk_copy(half, 0, 0, 0, 0).wait()
            return c
        lax.fori_loop(0, n_ref[half], wait_one, 0)

        def gather(kc, c, half=half, srow=srow):
            k0 = pl.multiple_of(kc * SEQ_ALIGN, SEQ_ALIGN)
            acc_ref[...] = acc_ref[...] + lax.dot_general(
                _onehot_rows(srow, k0), sy_ref[half, pl.ds(k0, SEQ_ALIGN), :], (((0,), (0,)), ((), ())),
                preferred_element_type=jnp.float32)
            return c

        lax.fori_loop(0, _n_blocks(k_tot), gather, 0)

        @pl.when(ti + 1 < nt)
        def _(half=half):
            fetch(half, ti + 1)

    tsum = ALPHA * h_ref[...] + acc_ref[...]
    _norm_route(tsum, g_ref, b_ref, None, ti * t, lp, seq_len, ho_ref, hb_ref, None, None)


def combine(starts, ye, pos, h, g, bb, lp, seq_len):
    n = h.shape[0]
    nt = n // TOK_TILE
    row = lambda i, st: (i, 0)
    fix = lambda i, st: (0, 0)
    grid_spec = pltpu.PrefetchScalarGridSpec(
        num_scalar_prefetch=1,
        grid=(nt,),
        in_specs=[pl.BlockSpec(memory_space=pl.ANY), pl.BlockSpec((1, N_EXPERTS, TOK_TILE), lambda i, st: (i, 0, 0)),
                  pl.BlockSpec((TOK_TILE, D_MODEL), row), pl.BlockSpec((1, D_MODEL), fix), pl.BlockSpec((1, D_MODEL), fix)],
        out_specs=[pl.BlockSpec((TOK_TILE, D_MODEL), row), pl.BlockSpec((TOK_TILE, D_MODEL), row)],
        scratch_shapes=[pltpu.VMEM((2, HALF_ROWS, D_MODEL), jnp.bfloat16), pltpu.VMEM((TOK_TILE, D_MODEL), jnp.float32),
                        pltpu.SMEM((2,), jnp.int32), pltpu.SemaphoreType.DMA((2,))],
    )
    return pl.pallas_call(
        functools.partial(_combine_kernel, lp=lp, seq_len=seq_len),
        grid_spec=grid_spec,
        out_shape=[jax.ShapeDtypeStruct((n, D_MODEL), jnp.float32), jax.ShapeDtypeStruct((n, D_MODEL), jnp.bfloat16)],
        compiler_params=_cparams("arbitrary"),
        name="moe_combine",
    )(starts, ye, pos, h, g, bb)


def moe_block(h, hb, aff, afft, w1, w3, w2, g, bb, n_real, lp, seq_len):
    cap = max(1, CAPACITY_FACTOR * n_real // N_EXPERTS)
    cap_pad = -(-(cap + DMA_ROWS) // FFN_ROWS) * FFN_ROWS
    pos, start3 = route(afft, cap)
    starts = start3[:, 0, :N_EXPERTS].reshape(-1)
    xe = dispatch(starts, hb, pos, aff, cap, cap_pad)
    ye = expert_ffn(xe, w1, w3, w2, cap)
    return combine(starts, ye, pos, h, g, bb, lp, seq_len)


def _prep_weights(meta, w_in, conv_w, conv_b, a_log_f, a_log_b, dt_bias_f, dt_bias_b, d_skip, ssd_norm_w,
                  lam_q1, lam_k1, lam_q2, lam_k2, subln_w, rel_bias, w_out, pool_w, pool_b, pool_scale,
                  ln1_g, ln1_b, ln2_g, ln2_b, w_router, w1, w3, w2):
    bf = jnp.bfloat16
    f32 = jnp.float32
    w = w_in[0]
    o = [0, SSD_INNER, SSD_INNER + SSD_CONV_CH, SSD_INNER + SSD_CONV_CH + 2 * SSD_HEADS]
    wz, wxbc, wdt = w[:, o[0]:o[1]], w[:, o[1]:o[2]], w[:, o[2]:o[3]]
    wq, wk, wv = (w[:, o[3] + i * ATT_QK:o[3] + (i + 1) * ATT_QK] for i in range(3))
    pad = jnp.zeros((D_MODEL, _PROJ_COLS - _COL_DT - 2 * SSD_HEADS), f32)
    w_tok = jnp.concatenate([wz, wxbc, wdt, pad], axis=1).astype(bf)
    w_qv = jnp.concatenate([wq * (ATT_HEAD_DIM ** -0.5 * LOG2E), wv], axis=1).T.astype(bf)
    lane_pad = lambda v: jnp.pad(v.astype(f32), (0, LANES - v.shape[0]))
    par = jnp.stack([lane_pad(jnp.concatenate([dt_bias_f[0], dt_bias_b[0]])),
                     lane_pad(jnp.concatenate([a_log_f[0], a_log_b[0]]))] + [jnp.zeros((LANES,), f32)] * 6)
    lam4 = jnp.stack([lane_pad(lam_q1[0]), lane_pad(lam_k1[0]), lane_pad(lam_q2[0]), lane_pad(lam_k2[0])]
                     + [jnp.zeros((LANES,), f32)] * 4)
    return dict(
        meta=meta, w_tok=w_tok, w_k=wk.astype(bf), w_qv=w_qv, conv_w=conv_w[0, :, 0, :], conv_b=conv_b[0][None], par=par,
        dsk=jnp.repeat(d_skip[0], SSD_HEAD_DIM)[None], nw=ssd_norm_w[0][None], lam4=lam4,
        subw=jnp.broadcast_to(subln_w[0][:, None], (ATT_V_DIM, SEQ_ALIGN)), rel_bias=rel_bias,
        w_out=w_out[0].astype(bf), pool_w=pool_w[0].astype(bf), pool_b=pool_b[0][None], pool_scale=pool_scale[0][None],
        ln1_g=ln1_g, ln1_b=ln1_b, ln2_g=ln2_g, ln2_b=ln2_b,
        w_r=jnp.pad(w_router, ((0, 0), (0, 0), (0, LANES - N_EXPERTS))).astype(bf),
        w1=w1.astype(bf), w3=w3.astype(bf), w2=w2.astype(bf))


def _trunk(x, p, bias5):
    b, s, _ = x.shape
    seq_len = s + N_META
    lp = -(-seq_len // SEQ_ALIGN) * SEQ_ALIGN
    n = b * lp
    n_real = b * seq_len
    meta = jnp.broadcast_to(p["meta"][None].astype(x.dtype), (b, N_META, D_MODEL))
    h = jnp.concatenate([meta, x, jnp.zeros((b, lp - seq_len, D_MODEL), x.dtype)], axis=1).reshape(n, D_MODEL)
    hb = h.astype(jnp.bfloat16)
    lambda_init = 0.8 - 0.6 * math.exp(-0.3 * 0)
    tm = min(ROW_TILE, n)

    proj = mm_nn(hb, p["w_tok"], jnp.float32, tm, 512)
    k3 = mm_nn(hb, p["w_k"], jnp.bfloat16, tm, 512).reshape(b, lp, ATT_QK)
    qv = mm_nt_blocked(p["w_qv"], hb, jnp.bfloat16, tm, 512)
    proj3 = proj.reshape(b, lp, _PROJ_COLS)
    xbc = conv_silu(proj3, p["conv_w"], p["conv_b"]).reshape(n, SSD_CONV_CH)
    yf = ssd_scan(xbc, proj, p["par"], b, lp, seq_len, False)
    yb = ssd_scan(xbc, proj, p["par"], b, lp, seq_len, True)
    y_ssd = ssd_gate(yf, yb, xbc, proj, p["dsk"], p["nw"], min(512, n))
    y_att = diff_attention(qv, k3, bias5, p["lam4"], p["subw"], b, lp, seq_len, lambda_init)
    h, hb, aff, afft = out_proj_norm_route(y_ssd, y_att, p["w_out"], h, p["ln1_g"][0][None], p["ln1_b"][0][None],
                                           p["w_r"][0], lp, seq_len)
    h, hb = moe_block(h, hb, aff, afft, p["w1"][0], p["w3"][0], p["w2"][0], p["ln2_g"][0][None], p["ln2_b"][0][None],
                      n_real, lp, seq_len)
    d = pool_diff(h.reshape(b, lp, D_MODEL), seq_len).reshape(n, D_MODEL)
    h, hb, aff, afft = pool_proj_norm_route(d, p["pool_w"], p["pool_b"], p["pool_scale"], h, p["ln1_g"][1][None],
                                            p["ln1_b"][1][None], p["w_r"][1], lp, seq_len)
    h, hb = moe_block(h, hb, aff, afft, p["w1"][1], p["w3"][1], p["w2"][1], p["ln2_g"][1][None], p["ln2_b"][1][None],
                      n_real, lp, seq_len)
    return h.reshape(b, lp, D_MODEL)[:, N_META:seq_len]


def kernel(x_prompt, x_sample, meta, w_in, conv_w, conv_b, a_log_f, a_log_b, dt_bias_f, dt_bias_b, d_skip, ssd_norm_w, lam_q1, lam_k1, lam_q2, lam_k2, subln_w, rel_bias, w_out, pool_w, pool_b, pool_scale, ln1_g, ln1_b, ln2_g, ln2_b, w_router, w1, w3, w2):
    p = _prep_weights(meta, w_in, conv_w, conv_b, a_log_f, a_log_b, dt_bias_f, dt_bias_b, d_skip, ssd_norm_w,
                      lam_q1, lam_k1, lam_q2, lam_k2, subln_w, rel_bias, w_out, pool_w, pool_b, pool_scale,
                      ln1_g, ln1_b, ln2_g, ln2_b, w_router, w1, w3, w2)
    bias5 = bias_tiles(rel_bias)
    return (_trunk(x_prompt, p, bias5), _trunk(x_sample, p, bias5))
```

```python
import functools
import math

import jax
import jax.numpy as jnp
from jax import lax
from jax.experimental import pallas as pl
from jax.experimental.pallas import tpu as pltpu

D_MODEL = 2048
DEPTH = 2
N_META = 16
SSD_HEADS = 16
SSD_HEAD_DIM = 64
SSD_INNER = SSD_HEADS * SSD_HEAD_DIM
SSD_GROUPS = 4
SSD_STATE = 128
SSD_CONV = 5
SSD_CHUNK = 128
SSD_CONV_CH = SSD_INNER + 2 * SSD_GROUPS * SSD_STATE
ATT_HEADS = 8
ATT_HEAD_DIM = 64
ATT_V_DIM = 2 * ATT_HEAD_DIM
ATT_QK = ATT_HEADS * 2 * ATT_HEAD_DIM
ATT_INNER = ATT_HEADS * ATT_V_DIM
REL_BUCKETS = 32
POOL_WINDOWS = (2, 4, 8, 16)
POOL_GROUP = D_MODEL // len(POOL_WINDOWS)
N_EXPERTS = 16
EXPERT_FF = 2048
CAPACITY_FACTOR = 2
ALPHA = (2 * DEPTH) ** 0.25
LN_EPS = 1e-5

LANES = 128
BF16_ROWS = 16
SEQ_ALIGN = 256
TOK_TILE = 256
ROW_TILE = 1024
DMA_ROWS = 32
OUT_ROWS = 16
FFN_ROWS = 704
FFN_COLS = 512
VMEM_LIMIT = 56 * 1024 * 1024

_BUCKET_STARTS = (1, 2, 3, 4, 5, 6, 7, 8, 12, 16, 23, 32, 46, 64, 91)
_COL_Z, _COL_XBC, _COL_DT, _PROJ_COLS = 0, 1024, 3072, 3584
ATT_CHUNK = 4
ATT_HEADS_PER_STEP = 2
LOG2E = math.log2(math.e)


def _cparams(*sem):
    return pltpu.CompilerParams(dimension_semantics=sem, vmem_limit_bytes=VMEM_LIMIT)


def _sigmoid(x):
    return 1.0 / (1.0 + jnp.exp(-x))


def _softplus(x):
    return jnp.maximum(x, 0.0) + jnp.log(1.0 + jnp.exp(-jnp.abs(x)))


def _split3(x):
    p1 = x.astype(jnp.bfloat16)
    r1 = x - p1.astype(jnp.float32)
    p2 = r1.astype(jnp.bfloat16)
    p3 = (r1 - p2.astype(jnp.float32)).astype(jnp.bfloat16)
    return p1, p2, p3


def _mm_nn_kernel(x_ref, w_ref, o_ref):
    o_ref[...] = jnp.dot(x_ref[...], w_ref[...], preferred_element_type=jnp.float32).astype(o_ref.dtype)


def mm_nn(x, w, out_dtype, tm, tn):
    n, k = x.shape
    nn = w.shape[1]
    return pl.pallas_call(
        _mm_nn_kernel,
        grid=(n // tm, nn // tn),
        in_specs=[pl.BlockSpec((tm, k), lambda i, j: (i, 0)), pl.BlockSpec((k, tn), lambda i, j: (0, j))],
        out_specs=pl.BlockSpec((tm, tn), lambda i, j: (i, j)),
        out_shape=jax.ShapeDtypeStruct((n, nn), out_dtype),
        compiler_params=_cparams("parallel", "arbitrary"),
        name="mm_nn",
    )(x, w)


def _mm_nt_kernel(w_ref, x_ref, o_ref):
    r = lax.dot_general(w_ref[...], x_ref[...], (((1,), (1,)), ((), ())), preferred_element_type=jnp.float32)
    for s in range(o_ref.shape[0]):
        o_ref[s] = r[:, s * SEQ_ALIGN:(s + 1) * SEQ_ALIGN].astype(o_ref.dtype)


def mm_nt_blocked(wt, x, out_dtype, tm, tn):
    n, k = x.shape
    nn = wt.shape[0]
    per = tm // SEQ_ALIGN
    return pl.pallas_call(
        _mm_nt_kernel,
        grid=(n // tm, nn // tn),
        in_specs=[pl.BlockSpec((tn, k), lambda i, j: (j, 0)), pl.BlockSpec((tm, k), lambda i, j: (i, 0))],
        out_specs=pl.BlockSpec((per, tn, SEQ_ALIGN), lambda i, j: (i, j, 0)),
        out_shape=jax.ShapeDtypeStruct((n // SEQ_ALIGN, nn, SEQ_ALIGN), out_dtype),
        compiler_params=_cparams("parallel", "arbitrary"),
        name="mm_nt",
    )(wt, x)


def _conv_kernel(x_ref, w_ref, b_ref, o_ref, pad_ref, *, rows):
    lp = x_ref.shape[1]
    halo = 8
    pad_ref[0:halo, :] = jnp.zeros((halo, LANES), jnp.float32)
    pad_ref[lp + halo:lp + 2 * halo, :] = jnp.zeros((halo, LANES), jnp.float32)
    pad_ref[halo:lp + halo, :] = x_ref[0]
    for t0 in range(0, lp, rows):
        acc = jnp.broadcast_to(b_ref[...], (rows, LANES))
        for k in range(SSD_CONV):
            off = t0 + halo - SSD_CONV // 2 + k
            acc = acc + pad_ref[off:off + rows, :] * w_ref[k:k + 1, :]
        o_ref[0, t0:t0 + rows, :] = acc * _sigmoid(acc)


def conv_silu(proj3, conv_w, conv_b):
    b, lp, _ = proj3.shape
    c0 = _COL_XBC // LANES
    return pl.pallas_call(
        functools.partial(_conv_kernel, rows=SEQ_ALIGN),
        grid=(b, SSD_CONV_CH // LANES),
        in_specs=[
            pl.BlockSpec((1, lp, LANES), lambda i, j: (i, 0, c0 + j)),
            pl.BlockSpec((SSD_CONV, LANES), lambda i, j: (0, j)),
            pl.BlockSpec((1, LANES), lambda i, j: (0, j)),
        ],
        out_specs=pl.BlockSpec((1, lp, LANES), lambda i, j: (i, 0, j)),
        out_shape=jax.ShapeDtypeStruct((b, lp, SSD_CONV_CH), jnp.float32),
        scratch_shapes=[pltpu.VMEM((lp + 16, LANES), jnp.float32)],
        compiler_params=_cparams("parallel", "parallel"),
        name="conv_silu",
    )(proj3, conv_w, conv_b)


def _ssd_kernel(x_ref, b_ref, c_ref, dt_ref, par_ref, y_ref, h_ref, *, seq_len, reverse):
    cs = SSD_CHUNK
    step = pl.program_id(1)
    chunk = (pl.num_programs(1) - 1 - step) if reverse else step
    lane0 = SSD_HEADS if reverse else 0

    @pl.when(step == 0)
    def _():
        h_ref[...] = jnp.zeros(h_ref.shape, jnp.float32)

    sub = lax.broadcasted_iota(jnp.int32, (cs, cs), 0)
    lan = lax.broadcasted_iota(jnp.int32, (cs, cs), 1)
    valid = (chunk * cs + sub[:, 0:1]) < seq_len
    dt = jnp.where(valid, _softplus(dt_ref[...] + par_ref[0:1, :]), 0.0)
    a = dt * -jnp.exp(par_ref[1:2, :])
    a_t = a.T[0:2 * SSD_HEADS, :]
    if reverse:
        incl, rest = (sub >= lan), (sub < lan)
    else:
        incl, rest = (sub <= lan), (sub > lan)
    tri = jnp.concatenate([incl.astype(jnp.bfloat16), rest.astype(jnp.bfloat16),
                           jnp.ones((cs, cs), jnp.bfloat16)], axis=1)
    pieces = jnp.concatenate(_split3(a_t), axis=0)
    sums = jnp.dot(pieces, tri, preferred_element_type=jnp.float32)
    nh2 = 2 * SSD_HEADS
    sums = sums[0:nh2] + sums[nh2:2 * nh2] + sums[2 * nh2:3 * nh2]
    cum_row, rest_row, tot_row = sums[:, 0:cs], sums[:, cs:2 * cs], sums[:, 2 * cs:3 * cs]
    cum_col = jnp.concatenate([cum_row, jnp.zeros((cs - nh2, cs), jnp.float32)], axis=0).T
    mask = (sub <= lan) if reverse else (sub >= lan)
    left = lan[0:1, :] < SSD_HEAD_DIM

    for g in range(SSD_GROUPS):
        bg = b_ref[:, g * SSD_STATE:(g + 1) * SSD_STATE]
        cg = c_ref[:, g * SSD_STATE:(g + 1) * SSD_STATE]
        cb = lax.dot_general(cg.astype(jnp.bfloat16), bg.astype(jnp.bfloat16), (((1,), (1,)), ((), ())),
                             preferred_element_type=jnp.float32)
        bg_t = bg.T
        for pp in range(2):
            p = 2 * g + pp
            r0, r1 = lane0 + 2 * p, lane0 + 2 * p + 1
            dtp = jnp.where(left, dt[:, r0:r0 + 1], dt[:, r1:r1 + 1])
            xs = (x_ref[:, p * LANES:(p + 1) * LANES] * dtp).astype(jnp.bfloat16)
            h_t = h_ref[p]
            rhs = jnp.concatenate([xs, h_t.astype(jnp.bfloat16)], axis=0)
            ys, ss = [], []
            for r in (r0, r1):
                cc = cum_col[:, r:r + 1]
                dec = jnp.exp(jnp.where(mask, cc - cum_row[r:r + 1, :], -jnp.inf))
                lhs = jnp.concatenate([(cb * dec).astype(jnp.bfloat16),
                                       (cg * jnp.exp(cc)).astype(jnp.bfloat16)], axis=1)
                ys.append(jnp.dot(lhs, rhs, preferred_element_type=jnp.float32))
                bw = (bg_t * jnp.exp(rest_row[r:r + 1, :])).astype(jnp.bfloat16)
                ss.append(jnp.dot(bw, xs, preferred_element_type=jnp.float32))
            y_ref[:, p * LANES:(p + 1) * LANES] = jnp.where(left, ys[0], ys[1])
            decay = jnp.where(left, jnp.exp(tot_row[r0:r0 + 1, :]), jnp.exp(tot_row[r1:r1 + 1, :]))
            h_ref[p] = decay * h_t + jnp.where(left, ss[0], ss[1])


def ssd_scan(xbc, proj, par, b, lp, seq_len, reverse):
    n = xbc.shape[0]
    nc = lp // SSD_CHUNK
    cdt = _COL_DT // LANES

    def row(i, c):
        return i * nc + ((nc - 1 - c) if reverse else c)

    return pl.pallas_call(
        functools.partial(_ssd_kernel, seq_len=seq_len, reverse=reverse),
        grid=(b, nc),
        in_specs=[
            pl.BlockSpec((SSD_CHUNK, SSD_INNER), lambda i, c: (row(i, c), 0)),
            pl.BlockSpec((SSD_CHUNK, SSD_GROUPS * SSD_STATE), lambda i, c: (row(i, c), 2)),
            pl.BlockSpec((SSD_CHUNK, SSD_GROUPS * SSD_STATE), lambda i, c: (row(i, c), 3)),
            pl.BlockSpec((SSD_CHUNK, LANES), lambda i, c: (row(i, c), cdt)),
            pl.BlockSpec((8, LANES), lambda i, c: (0, 0)),
        ],
        out_specs=pl.BlockSpec((SSD_CHUNK, SSD_INNER), lambda i, c: (row(i, c), 0)),
        out_shape=jax.ShapeDtypeStruct((n, SSD_INNER), jnp.float32),
        scratch_shapes=[pltpu.VMEM((SSD_HEADS // 2, SSD_STATE, LANES), jnp.float32)],
        compiler_params=_cparams("parallel", "arbitrary"),
        name="ssd_bwd" if reverse else "ssd_fwd",
    )(xbc, xbc, xbc, proj, par)


def _gate_kernel(yf_ref, yb_ref, x_ref, z_ref, dsk_ref, nw_ref, o_ref):
    z = z_ref[...]
    y = (yf_ref[...] + yb_ref[...] + dsk_ref[...] * x_ref[...]) * (z * _sigmoid(z))
    gw = SSD_INNER // SSD_GROUPS
    for g in range(SSD_GROUPS):
        seg = y[:, g * gw:(g + 1) * gw]
        ms = jnp.mean(seg * seg, axis=-1, keepdims=True)
        o_ref[:, g * gw:(g + 1) * gw] = (seg * lax.rsqrt(ms + LN_EPS) * nw_ref[:, g * gw:(g + 1) * gw]).astype(o_ref.dtype)


def ssd_gate(yf, yb, xbc, proj, dsk, nw, tm):
    n = yf.shape[0]
    blk = lambda i: (i, 0)
    return pl.pallas_call(
        _gate_kernel,
        grid=(n // tm,),
        in_specs=[pl.BlockSpec((tm, SSD_INNER), blk)] * 4 + [pl.BlockSpec((1, SSD_INNER), lambda i: (0, 0))] * 2,
        out_specs=pl.BlockSpec((tm, SSD_INNER), blk),
        out_shape=jax.ShapeDtypeStruct((n, SSD_INNER), jnp.bfloat16),
        compiler_params=_cparams("parallel"),
        name="ssd_gate",
    )(yf, yb, xbc, proj, dsk, nw)


def _bias_kernel(rb_ref, o_ref):
    h = pl.program_id(0)
    d = pl.program_id(1) - 2
    t = SEQ_ALIGN
    kk = lax.broadcasted_iota(jnp.int32, (t, t), 0)
    qq = lax.broadcasted_iota(jnp.int32, (t, t), 1)
    rel = d * t + kk - qq
    n = jnp.abs(rel)
    half = REL_BUCKETS // 2
    vneg = jnp.full((t, t), rb_ref[0, h], jnp.float32)
    vpos = jnp.full((t, t), rb_ref[half, h], jnp.float32)
    for j, start in enumerate(_BUCKET_STARTS, start=1):
        ge = n >= start
        vneg = jnp.where(ge, rb_ref[j, h], vneg)
        vpos = jnp.where(ge, rb_ref[half + j, h], vpos)
    o_ref[0, 0] = jnp.where(rel > 0, vpos, vneg) * LOG2E


def bias_tiles(rel_bias):
    return pl.pallas_call(
        _bias_kernel,
        grid=(ATT_HEADS, 5),
        in_specs=[pl.BlockSpec(memory_space=pltpu.SMEM)],
        out_specs=pl.BlockSpec((1, 1, SEQ_ALIGN, SEQ_ALIGN), lambda h, d: (h, d, 0, 0)),
        out_shape=jax.ShapeDtypeStruct((ATT_HEADS, 5, SEQ_ALIGN, SEQ_ALIGN), jnp.float32),
        compiler_params=_cparams("parallel", "parallel"),
        name="bias_tiles",
    )(rel_bias)


def _attn_kernel(q_ref, v_ref, k_ref, bias_ref, lam_ref, sub_ref, o_ref, s_ref, *, seq_len, lambda_init):
    t = SEQ_ALIGN
    group = ATT_CHUNK
    vd = ATT_V_DIM
    qb = pl.program_id(2)
    sub = lax.broadcasted_iota(jnp.int32, (vd, t), 0)
    qs = []
    for hh in range(ATT_HEADS_PER_STEP):
        q_t = q_ref[0, hh * vd:(hh + 1) * vd, :]
        zero = jnp.zeros_like(q_t)
        qs += [jnp.where(sub < ATT_HEAD_DIM, q_t, zero), jnp.where(sub >= ATT_HEAD_DIM, q_t, zero)]
    n_maps = len(qs)
    n_full = seq_len // t
    tail = seq_len - n_full * t
    tail_rows = -(-tail // LANES) * LANES

    def fold_max(x):
        return jnp.max(x.reshape(x.shape[0] // 8, 8, t), axis=0)

    def score_block(kb, rows, mx):
        r0 = kb * t
        if not isinstance(kb, int):
            r0 = pl.multiple_of(r0, t)
        tile = jnp.clip(kb - qb, -2, 2) + 2
        out = []
        for mi in range(n_maps):
            hh = mi // 2
            kblk = k_ref[0, pl.ds(r0, rows), hh * vd:(hh + 1) * vd]
            s = jnp.dot(kblk, qs[mi], preferred_element_type=jnp.float32) + bias_ref[hh, tile, 0:rows, :]
            if rows != t:
                s = jnp.where(lax.broadcasted_iota(jnp.int32, (rows, 1), 0) < tail, s, -jnp.inf)
            s_ref[mi, pl.ds(r0, rows), :] = s
            out.append(jnp.maximum(mx[mi], fold_max(s)))
        return tuple(out)

    neg = jnp.full((8, t), -jnp.inf, jnp.float32)

    mx = (neg,) * n_maps
    for kb in range(n_full):
        mx = score_block(kb, t, mx)
    if tail:
        mx = score_block(n_full, tail_rows, mx)
    ms_ = [jnp.max(m, axis=0, keepdims=True) for m in mx]

    def prob_blocks(kbs, rows, accs):
        accs = list(accs)
        ones = jnp.ones((BF16_ROWS, rows * len(kbs)), jnp.bfloat16)
        for hh in range(ATT_HEADS_PER_STEP):
            vcat = jnp.concatenate([v_ref[kb][hh * vd:(hh + 1) * vd, 0:rows] for kb in kbs] , axis=1)
            vaug = jnp.concatenate([vcat, ones], axis=0)
            for mi in (2 * hh, 2 * hh + 1):
                ps = []
                for kb in kbs:
                    r0 = kb * t
                    if not isinstance(kb, int):
                        r0 = pl.multiple_of(r0, t)
                    ps.append(jnp.exp2((s_ref[mi, pl.ds(r0, rows), :] - ms_[mi]).astype(jnp.bfloat16)))
                pcat = jnp.concatenate(ps, axis=0) if len(ps) > 1 else ps[0]
                accs[mi] = accs[mi] + jnp.dot(vaug, pcat, preferred_element_type=jnp.float32)
        return tuple(accs)

    zacc = jnp.zeros((vd + BF16_ROWS, t), jnp.float32)
    accs = (zacc,) * n_maps
    for k0 in range(0, n_full, group):
        accs = prob_blocks(list(range(k0, min(k0 + group, n_full))), t, accs)
    if tail:
        accs = prob_blocks([n_full], tail_rows, accs)
    lv = lam_ref[...]
    lam = (jnp.exp(jnp.sum(lv[0:1] * lv[1:2], axis=-1, keepdims=True))
           - jnp.exp(jnp.sum(lv[2:3] * lv[3:4], axis=-1, keepdims=True)) + lambda_init)
    for hh in range(ATT_HEADS_PER_STEP):
        a1, a2 = accs[2 * hh], accs[2 * hh + 1]
        o = a1[0:vd] * (1.0 / a1[vd:vd + 1]) - lam * (a2[0:vd] * (1.0 / a2[vd:vd + 1]))
        ms = jnp.mean(o * o, axis=0, keepdims=True)
        o = o * lax.rsqrt(ms + LN_EPS) * sub_ref[...] * (1.0 - lambda_init)
        o_ref[:, hh * vd:(hh + 1) * vd] = o.T.astype(o_ref.dtype)


def diff_attention(qv, k3, bias5, lam4, subw, b, lp, seq_len, lambda_init):
    nq = lp // SEQ_ALIGN
    hs = ATT_HEADS_PER_STEP
    return pl.pallas_call(
        functools.partial(_attn_kernel, seq_len=seq_len, lambda_init=lambda_init),
        grid=(b, ATT_HEADS // hs, nq),
        in_specs=[
            pl.BlockSpec((1, hs * ATT_V_DIM, SEQ_ALIGN), lambda i, h, q: (i * nq + q, h, 0)),
            pl.BlockSpec((nq, hs * ATT_V_DIM, SEQ_ALIGN), lambda i, h, q: (i, ATT_HEADS // hs + h, 0)),
            pl.BlockSpec((1, lp, hs * ATT_V_DIM), lambda i, h, q: (i, 0, h)),
            pl.BlockSpec((hs, 5, SEQ_ALIGN, SEQ_ALIGN), lambda i, h, q: (h, 0, 0, 0)),
            pl.BlockSpec((8, LANES), lambda i, h, q: (0, 0)),
            pl.BlockSpec((ATT_V_DIM, SEQ_ALIGN), lambda i, h, q: (0, 0)),
        ],
        out_specs=pl.BlockSpec((SEQ_ALIGN, hs * ATT_V_DIM), lambda i, h, q: (i * nq + q, h)),
        out_shape=jax.ShapeDtypeStruct((b * lp, ATT_INNER), jnp.bfloat16),
        scratch_shapes=[pltpu.VMEM((2 * hs, lp, SEQ_ALIGN), jnp.float32)],
        compiler_params=_cparams("parallel", "parallel", "arbitrary"),
        name="diff_attn",
    )(qv, qv, k3, bias5, lam4, subw)


def _layer_norm_rows(t, g_ref, b_ref, tile_row0, lp, seq_len):
    rows = t.shape[0]
    mu = jnp.mean(t, axis=-1, keepdims=True)
    tc = t - mu
    var = jnp.mean(tc * tc, axis=-1, keepdims=True)
    y = tc * lax.rsqrt(var + LN_EPS) * g_ref[...] + b_ref[...]
    pos = tile_row0 % lp + lax.broadcasted_iota(jnp.int32, (rows, 1), 0)
    valid = pos < seq_len
    return jnp.where(valid, y, 0.0), valid


def _norm_route(t, g_ref, b_ref, wr_ref, tile_row0, lp, seq_len, h_ref, hb_ref, aff_ref, afft_ref):
    y, valid = _layer_norm_rows(t, g_ref, b_ref, tile_row0, lp, seq_len)
    h_ref[...] = y
    yb = y.astype(jnp.bfloat16)
    hb_ref[...] = yb
    if aff_ref is not None:
        logits = jnp.dot(yb, wr_ref[...], preferred_element_type=jnp.float32)
        lane = lax.broadcasted_iota(jnp.int32, logits.shape, 1)
        is_e = lane < N_EXPERTS
        logits = jnp.where(is_e, logits, -jnp.inf)
        e = jnp.exp(logits - jnp.max(logits, axis=-1, keepdims=True))
        aff = e / jnp.sum(e, axis=-1, keepdims=True)
        aff = jnp.where(valid, aff, jnp.where(is_e, -1.0, 0.0))
        aff_ref[...] = aff
        afft_ref[0] = aff.T[0:N_EXPERTS, :]


def _outproj_kernel(ys_ref, ya_ref, w_ref, h_ref, g_ref, b_ref, wr_ref, ho_ref, hb_ref, aff_ref, afft_ref, *, lp, seq_len):
    mix = (jnp.dot(ys_ref[...], w_ref[0:SSD_INNER, :], preferred_element_type=jnp.float32)
           + jnp.dot(ya_ref[...], w_ref[SSD_INNER:, :], preferred_element_type=jnp.float32))
    t = ALPHA * h_ref[...] + mix
    _norm_route(t, g_ref, b_ref, wr_ref, pl.program_id(0) * h_ref.shape[0], lp, seq_len, ho_ref, hb_ref, aff_ref, afft_ref)


def _poolproj_kernel(d_ref, w_ref, pb_ref, ps_ref, h_ref, g_ref, b_ref, wr_ref, ho_ref, hb_ref, aff_ref, afft_ref, *, lp, seq_len):
    outs = []
    for gi in range(len(POOL_WINDOWS)):
        outs.append(jnp.dot(d_ref[:, gi * POOL_GROUP:(gi + 1) * POOL_GROUP], w_ref[gi], preferred_element_type=jnp.float32))
    mix = (jnp.concatenate(outs, axis=-1) + pb_ref[...]) * ps_ref[...]
    t = ALPHA * h_ref[...] + mix
    _norm_route(t, g_ref, b_ref, wr_ref, pl.program_id(0) * h_ref.shape[0], lp, seq_len, ho_ref, hb_ref, aff_ref, afft_ref)


def _mix_outputs(n, tm):
    row = lambda i: (i, 0)
    specs = [pl.BlockSpec((tm, D_MODEL), row), pl.BlockSpec((tm, D_MODEL), row), pl.BlockSpec((tm, LANES), row),
             pl.BlockSpec((1, N_EXPERTS, tm), lambda i: (i, 0, 0))]
    shapes = [jax.ShapeDtypeStruct((n, D_MODEL), jnp.float32), jax.ShapeDtypeStruct((n, D_MODEL), jnp.bfloat16),
              jax.ShapeDtypeStruct((n, LANES), jnp.float32), jax.ShapeDtypeStruct((n // tm, N_EXPERTS, tm), jnp.float32)]
    return specs, shapes


def out_proj_norm_route(y_ssd, y_att, w_out, h, g, bb, w_r, lp, seq_len):
    n = h.shape[0]
    tm = TOK_TILE
    row = lambda i: (i, 0)
    fix = lambda i: (0, 0)
    specs, shapes = _mix_outputs(n, tm)
    return pl.pallas_call(
        functools.partial(_outproj_kernel, lp=lp, seq_len=seq_len),
        grid=(n // tm,),
        in_specs=[pl.BlockSpec((tm, SSD_INNER), row), pl.BlockSpec((tm, ATT_INNER), row),
                  pl.BlockSpec((D_MODEL, D_MODEL), fix), pl.BlockSpec((tm, D_MODEL), row),
                  pl.BlockSpec((1, D_MODEL), fix), pl.BlockSpec((1, D_MODEL), fix), pl.BlockSpec((D_MODEL, LANES), fix)],
        out_specs=specs, out_shape=shapes,
        compiler_params=_cparams("parallel"),
        name="outproj_ln_route",
    )(y_ssd, y_att, w_out, h, g, bb, w_r)


def pool_proj_norm_route(d, pool_w, pool_b, pool_scale, h, g, bb, w_r, lp, seq_len):
    n = h.shape[0]
    tm = TOK_TILE
    row = lambda i: (i, 0)
    fix = lambda i: (0, 0)
    specs, shapes = _mix_outputs(n, tm)
    return pl.pallas_call(
        functools.partial(_poolproj_kernel, lp=lp, seq_len=seq_len),
        grid=(n // tm,),
        in_specs=[pl.BlockSpec((tm, D_MODEL), row),
                  pl.BlockSpec((len(POOL_WINDOWS), POOL_GROUP, POOL_GROUP), lambda i: (0, 0, 0)),
                  pl.BlockSpec((1, D_MODEL), fix), pl.BlockSpec((1, D_MODEL), fix), pl.BlockSpec((tm, D_MODEL), row),
                  pl.BlockSpec((1, D_MODEL), fix), pl.BlockSpec((1, D_MODEL), fix), pl.BlockSpec((D_MODEL, LANES), fix)],
        out_specs=specs, out_shape=shapes,
        compiler_params=_cparams("parallel"),
        name="poolproj_ln_route",
    )(d, pool_w, pool_b, pool_scale, h, g, bb, w_r)


def _pool_kernel(x_ref, o_ref, s0_ref, s1_ref, *, seq_len):
    lp = x_ref.shape[1]
    halo = 16
    gi = pl.program_id(1) // (POOL_GROUP // LANES)
    zeros = jnp.zeros((halo, LANES), jnp.float32)
    for ref in (s0_ref, s1_ref):
        ref[0:halo, :] = zeros
        ref[lp + halo:lp + 2 * halo, :] = zeros
    s0_ref[halo:lp + halo, :] = x_ref[0]
    t = lax.broadcasted_iota(jnp.int32, (lp, 1), 0)
    for wi, w in enumerate(POOL_WINDOWS):
        @pl.when(gi == wi)
        def _(w=w):
            src, dst = s0_ref, s1_ref
            width = 1
            while width < w:
                dst[halo:lp + halo, :] = src[halo:lp + halo, :] + src[halo + width:lp + halo + width, :]
                dst[0:halo, :] = src[0:halo, :] + src[width:halo + width, :]
                src, dst = dst, src
                width *= 2
            win = src[halo - w // 2:lp + halo - w // 2, :]
            cnt = jnp.minimum(t + w // 2 - 1, seq_len - 1) - jnp.maximum(t - w // 2, 0) + 1
            d = win / cnt.astype(jnp.float32) - x_ref[0]
            o_ref[0] = jnp.where(t < seq_len, d, 0.0).astype(o_ref.dtype)


def pool_diff(h3, seq_len):
    b, lp, _ = h3.shape
    return pl.pallas_call(
        functools.partial(_pool_kernel, seq_len=seq_len),
        grid=(b, D_MODEL // LANES),
        in_specs=[pl.BlockSpec((1, lp, LANES), lambda i, j: (i, 0, j))],
        out_specs=pl.BlockSpec((1, lp, LANES), lambda i, j: (i, 0, j)),
        out_shape=jax.ShapeDtypeStruct((b, lp, D_MODEL), jnp.bfloat16),
        scratch_shapes=[pltpu.VMEM((lp + 32, LANES), jnp.float32)] * 2,
        compiler_params=_cparams("parallel", "parallel"),
        name="pool_diff",
    )(h3)


def _route_kernel(aff_ref, pos_ref, start_ref, *, cap):
    nt = aff_ref.shape[0]
    t = TOK_TILE

    def bits_of(v):
        return lax.bitcast_convert_type(v, jnp.int32)

    def count(mask):
        return jnp.sum(jnp.sum(jnp.where(mask, 1.0, 0.0), axis=0), axis=1, keepdims=True)

    def search(i, thr):
        cand = thr | jnp.left_shift(jnp.int32(1), 30 - i)
        cnt = count(bits_of(aff_ref[...]) >= cand[None])
        return jnp.where(cnt >= cap, cand, thr)

    thr = lax.fori_loop(0, 31, search, jnp.zeros((N_EXPERTS, 1), jnp.int32))
    need = cap - count(bits_of(aff_ref[...]) > thr[None])

    before = (lax.broadcasted_iota(jnp.int32, (t, t), 0) < lax.broadcasted_iota(jnp.int32, (t, t), 1)).astype(jnp.bfloat16)
    ones8 = jnp.ones((8, t), jnp.bfloat16)
    zpad = jnp.zeros((LANES - N_EXPERTS, t), jnp.float32)

    def tile(ti, carry):
        c_tie, c_sel, c_lane = carry
        b = bits_of(aff_ref[ti])
        tie = jnp.where(b == thr, 1.0, 0.0)
        rank = jnp.dot(tie.astype(jnp.bfloat16), before, preferred_element_type=jnp.float32) + c_tie
        sel = jnp.where(b > thr, 1.0, jnp.where(rank < need, tie, 0.0))
        slot = jnp.dot(sel.astype(jnp.bfloat16), before, preferred_element_type=jnp.float32) + c_sel
        pos_ref[ti] = jnp.where(sel > 0.0, slot, -1.0).astype(jnp.int32)
        start_ref[ti] = c_lane.astype(jnp.int32)
        selp = jnp.concatenate([sel, zpad], axis=0).astype(jnp.bfloat16)
        cnt_lane = lax.dot_general(ones8, selp, (((1,), (1,)), ((), ())), preferred_element_type=jnp.float32)
        return (c_tie + jnp.sum(tie, axis=1, keepdims=True), c_sel + jnp.sum(sel, axis=1, keepdims=True),
                c_lane + cnt_lane[0:1])

    zc = jnp.zeros((N_EXPERTS, 1), jnp.float32)
    _, _, c_lane = lax.fori_loop(0, nt, tile, (zc, zc, jnp.zeros((1, LANES), jnp.float32)))
    start_ref[nt] = c_lane.astype(jnp.int32)


def route(afft, cap):
    nt = afft.shape[0]
    return pl.pallas_call(
        functools.partial(_route_kernel, cap=cap),
        out_shape=[jax.ShapeDtypeStruct((nt, N_EXPERTS, TOK_TILE), jnp.int32),
                   jax.ShapeDtypeStruct((nt + 1, 1, LANES), jnp.int32)],
        compiler_params=pltpu.CompilerParams(vmem_limit_bytes=VMEM_LIMIT),
        name="route",
    )(afft)


HALF_EXPERTS = N_EXPERTS // 2
_REGION_MAX = -(-(TOK_TILE + BF16_ROWS - 1) // DMA_ROWS) * DMA_ROWS
HALF_ROWS = HALF_EXPERTS * _REGION_MAX + SEQ_ALIGN
XE_COLS = D_MODEL + LANES


def _tile_plan(st_ref, ti, half):
    plan = []
    off = jnp.int32(0)
    for e in range(half * HALF_EXPERTS, (half + 1) * HALF_EXPERTS):
        s = st_ref[ti * N_EXPERTS + e]
        end = st_ref[(ti + 1) * N_EXPERTS + e]
        s16 = s - (s & (BF16_ROWS - 1))
        nch = jnp.where(end > s, lax.shift_right_logical(end - s16 + DMA_ROWS - 1, DMA_ROWS.bit_length() - 1), 0)
        plan.append((e, s16, nch, off, end))
        off = off + nch * DMA_ROWS
    return plan, off


def _stage_rows(pos_half, plan):
    sub = lax.broadcasted_iota(jnp.int32, (HALF_EXPERTS, 1), 0)
    delta = jnp.zeros((HALF_EXPERTS, 1), jnp.int32)
    for i, (e, s16, nch, off, end) in enumerate(plan):
        delta = jnp.where(sub == i, off - s16, delta)
    return jnp.where(pos_half >= 0, pos_half + delta, -1)


def _onehot_rows(srow, k0):
    k = k0 + lax.broadcasted_iota(jnp.int32, (SEQ_ALIGN, 1), 0)
    onehot = jnp.zeros((SEQ_ALIGN, srow.shape[1]), jnp.float32)
    for i in range(srow.shape[0]):
        onehot = jnp.where(srow[i:i + 1, :] == k, 1.0, onehot)
    return onehot.astype(jnp.bfloat16)


def _n_blocks(rows):
    return lax.shift_right_logical(rows + SEQ_ALIGN - 1, SEQ_ALIGN.bit_length() - 1)


def _dispatch_kernel(st_ref, x_ref, pos_ref, aff_ref, xe_ref, sx_ref, cx_ref, n_ref, sem_ref, *, cap, cap_pad):
    ti = pl.program_id(0)
    nt = pl.num_programs(0)

    def chunk_copy(half, e, s16, off, j):
        src = pl.multiple_of(off + j * DMA_ROWS, DMA_ROWS)
        dst = pl.multiple_of(s16 + j * DMA_ROWS, BF16_ROWS)
        return pltpu.make_async_copy(sx_ref.at[half, pl.ds(src, DMA_ROWS)], xe_ref.at[e, pl.ds(dst, DMA_ROWS)],
                                     sem_ref.at[half])

    def wait_chunks(half, count):
        def body(_, c):
            chunk_copy(half, 0, 0, 0, 0).wait()
            return c
        lax.fori_loop(0, count, body, 0)

    tail0 = (cap // DMA_ROWS) * DMA_ROWS
    n_tail = (cap_pad - tail0) // DMA_ROWS

    @pl.when(ti == 0)
    def _():
        cx_ref[...] = jnp.zeros(cx_ref.shape, cx_ref.dtype)
        sx_ref[0, 0:DMA_ROWS, :] = jnp.zeros((DMA_ROWS, XE_COLS), sx_ref.dtype)
        for e in range(N_EXPERTS):
            for j in range(n_tail):
                pltpu.make_async_copy(sx_ref.at[0, pl.ds(0, DMA_ROWS)],
                                      xe_ref.at[e, pl.ds(tail0 + j * DMA_ROWS, DMA_ROWS)], sem_ref.at[0]).start()
        wait_chunks(0, N_EXPERTS * n_tail)
        n_ref[0] = 0
        n_ref[1] = 0

    aff = aff_ref[...]
    a1, a2, a3 = _split3(aff)
    gate3 = (a1.astype(jnp.float32) + pltpu.roll(a2.astype(jnp.float32), N_EXPERTS, 1)
             + pltpu.roll(a3.astype(jnp.float32), 2 * N_EXPERTS, 1)).astype(jnp.bfloat16)
    xt = jnp.concatenate([x_ref[...], gate3], axis=1)

    for half in range(2):
        plan, k_tot = _tile_plan(st_ref, ti, half)
        srow = _stage_rows(pos_ref[0, half * HALF_EXPERTS:(half + 1) * HALF_EXPERTS, :], plan)
        wait_chunks(half, n_ref[half])

        def permute(kc, c, half=half, srow=srow):
            k0 = pl.multiple_of(kc * SEQ_ALIGN, SEQ_ALIGN)
            sx_ref[half, pl.ds(k0, SEQ_ALIGN), :] = jnp.dot(_onehot_rows(srow, k0), xt,
                                                            preferred_element_type=jnp.float32).astype(sx_ref.dtype)
            return c

        lax.fori_loop(0, _n_blocks(k_tot), permute, 0)

        total = jnp.int32(0)
        for e, s16, nch, off, end in plan:
            @pl.when(nch > 0)
            def _(half=half, e=e, s16=s16, off=off, end=end):
                head = pl.ds(pl.multiple_of(off, BF16_ROWS), BF16_ROWS)
                sx_ref[half, head, :] = sx_ref[half, head, :] + cx_ref[e]
                part = end & (BF16_ROWS - 1)
                last = pl.ds(pl.multiple_of(off + (end - part) - s16, BF16_ROWS), BF16_ROWS)
                cx_ref[e] = jnp.where(part > 0, sx_ref[half, last, :], jnp.zeros((BF16_ROWS, XE_COLS), cx_ref.dtype))

            def issue(j, c, half=half, e=e, s16=s16, off=off):
                chunk_copy(half, e, s16, off, j).start()
                return c
            lax.fori_loop(0, nch, issue, 0)
            total = total + nch
        n_ref[half] = total

    @pl.when(ti == nt - 1)
    def _():
        wait_chunks(0, n_ref[0])
        wait_chunks(1, n_ref[1])


def dispatch(starts, hb, pos, aff, cap, cap_pad):
    n = hb.shape[0]
    nt = n // TOK_TILE
    grid_spec = pltpu.PrefetchScalarGridSpec(
        num_scalar_prefetch=1,
        grid=(nt,),
        in_specs=[pl.BlockSpec((TOK_TILE, D_MODEL), lambda i, st: (i, 0)),
                  pl.BlockSpec((1, N_EXPERTS, TOK_TILE), lambda i, st: (i, 0, 0)),
                  pl.BlockSpec((TOK_TILE, LANES), lambda i, st: (i, 0))],
        out_specs=pl.BlockSpec(memory_space=pl.ANY),
        scratch_shapes=[pltpu.VMEM((2, HALF_ROWS, XE_COLS), jnp.bfloat16),
                        pltpu.VMEM((N_EXPERTS, BF16_ROWS, XE_COLS), jnp.bfloat16),
                        pltpu.SMEM((2,), jnp.int32), pltpu.SemaphoreType.DMA((2,))],
    )
    return pl.pallas_call(
        functools.partial(_dispatch_kernel, cap=cap, cap_pad=cap_pad),
        grid_spec=grid_spec,
        out_shape=jax.ShapeDtypeStruct((N_EXPERTS, cap_pad, XE_COLS), jnp.bfloat16),
        compiler_params=_cparams("arbitrary"),
        name="moe_dispatch",
    )(starts, hb, pos, aff)


def _ffn_kernel(x_ref, w1_ref, w3_ref, w2_ref, o_ref, acc_ref, *, cap):
    e = pl.program_id(0)
    f = pl.program_id(2)
    x = x_ref[0, :, 0:D_MODEL]
    a = jnp.dot(x, w1_ref[0], preferred_element_type=jnp.float32)
    b = jnp.dot(x, w3_ref[0], preferred_element_type=jnp.float32)
    hid = (a * _sigmoid(a) * b).astype(jnp.bfloat16)
    part = jnp.dot(hid, w2_ref[0], preferred_element_type=jnp.float32)

    @pl.when(f == 0)
    def _():
        acc_ref[...] = part

    @pl.when(f > 0)
    def _():
        acc_ref[...] = acc_ref[...] + part

    @pl.when(f == pl.num_programs(2) - 1)
    def _():
        rows = o_ref.shape[1]
        gl = x_ref[0, :, D_MODEL:XE_COLS].astype(jnp.float32)
        lane = lax.broadcasted_iota(jnp.int32, gl.shape, 1)
        mine = ((lane & (N_EXPERTS - 1)) == e) & (lane < 3 * N_EXPERTS)
        gate = jnp.sum(jnp.where(mine, gl, 0.0), axis=-1, keepdims=True)
        slot = pl.program_id(1) * rows + lax.broadcasted_iota(jnp.int32, (rows, 1), 0)
        o_ref[0] = jnp.where(slot < cap, acc_ref[...] * gate, 0.0).astype(o_ref.dtype)


def expert_ffn(xe, w1, w3, w2, cap):
    e, cap_pad, _ = xe.shape
    return pl.pallas_call(
        functools.partial(_ffn_kernel, cap=cap),
        grid=(e, cap_pad // FFN_ROWS, EXPERT_FF // FFN_COLS),
        in_specs=[pl.BlockSpec((1, FFN_ROWS, XE_COLS), lambda i, m, f: (i, m, 0)),
                  pl.BlockSpec((1, D_MODEL, FFN_COLS), lambda i, m, f: (i, 0, f)),
                  pl.BlockSpec((1, D_MODEL, FFN_COLS), lambda i, m, f: (i, 0, f)),
                  pl.BlockSpec((1, FFN_COLS, D_MODEL), lambda i, m, f: (i, f, 0))],
        out_specs=pl.BlockSpec((1, FFN_ROWS, D_MODEL), lambda i, m, f: (i, m, 0)),
        out_shape=jax.ShapeDtypeStruct((e, cap_pad, D_MODEL), jnp.bfloat16),
        scratch_shapes=[pltpu.VMEM((FFN_ROWS, D_MODEL), jnp.float32)],
        compiler_params=_cparams("parallel", "parallel", "arbitrary"),
        name="expert_ffn",
    )(xe, w1, w3, w2)


def _combine_kernel(st_ref, ye_ref, pos_ref, h_ref, g_ref, b_ref, *rest, lp, seq_len, final):
    if final:
        out_ref, sy_ref, acc_ref, n_ref, sem_ref, ybuf_ref, osem_ref = rest
    else:
        ho_ref, hb_ref, sy_ref, acc_ref, n_ref, sem_ref = rest
    ti = pl.program_id(0)
    nt = pl.num_programs(0)
    t = TOK_TILE

    def chunk_copy(half, e, s16, off, j):
        src = pl.multiple_of(s16 + j * DMA_ROWS, BF16_ROWS)
        dst = pl.multiple_of(off + j * DMA_ROWS, DMA_ROWS)
        return pltpu.make_async_copy(ye_ref.at[e, pl.ds(src, DMA_ROWS)], sy_ref.at[half, pl.ds(dst, DMA_ROWS)],
                                     sem_ref.at[half])

    def fetch(half, tile):
        plan, _ = _tile_plan(st_ref, tile, half)
        total = jnp.int32(0)
        for e, s16, nch, off, end in plan:
            def issue(j, c, e=e, s16=s16, off=off):
                chunk_copy(half, e, s16, off, j).start()
                return c
            lax.fori_loop(0, nch, issue, 0)
            total = total + nch
        n_ref[half] = total

    @pl.when(ti == 0)
    def _():
        sy_ref[...] = jnp.zeros(sy_ref.shape, sy_ref.dtype)
        fetch(0, 0)
        fetch(1, 0)

    acc_ref[...] = jnp.zeros(acc_ref.shape, jnp.float32)
    for half in range(2):
        plan, k_tot = _tile_plan(st_ref, ti, half)
        srow = _stage_rows(pos_ref[0, half * HALF_EXPERTS:(half + 1) * HALF_EXPERTS, :], plan)

        def wait_one(_, c, half=half):
            chunk_copy(half, 0, 0, 0, 0).wait()
            return c
        lax.fori_loop(0, n_ref[half], wait_one, 0)

        def gather(kc, c, half=half, srow=srow):
            k0 = pl.multiple_of(kc * SEQ_ALIGN, SEQ_ALIGN)
            acc_ref[...] = acc_ref[...] + lax.dot_general(
                _onehot_rows(srow, k0), sy_ref[half, pl.ds(k0, SEQ_ALIGN), :], (((0,), (0,)), ((), ())),
                preferred_element_type=jnp.float32)
            return c

        lax.fori_loop(0, _n_blocks(k_tot), gather, 0)

        @pl.when(ti + 1 < nt)
        def _(half=half):
            fetch(half, ti + 1)

    tsum = ALPHA * h_ref[...] + acc_ref[...]
    if not final:
        _norm_route(tsum, g_ref, b_ref, None, ti * t, lp, seq_len, ho_ref, hb_ref, None, None)
        return

    nts = lp // t

    def out_rows(tile):
        tok0 = lax.rem(tile, nts) * t
        first = jnp.where(tok0 == 0, N_META, 0)
        return first, jnp.clip(seq_len - tok0, 0, t) - first

    def out_copy(slot, bi, src, dst):
        return pltpu.make_async_copy(ybuf_ref.at[slot, pl.ds(src, OUT_ROWS)], out_ref.at[bi, pl.ds(dst, OUT_ROWS)],
                                     osem_ref.at[slot])

    def wait_out(slot, tile):
        def body(_, c):
            out_copy(slot, 0, 0, 0).wait()
            return c
        lax.fori_loop(0, out_rows(tile)[1] // OUT_ROWS, body, 0)

    slot = ti & 1

    @pl.when(ti >= 2)
    def _():
        wait_out(slot, ti - 2)

    y, _ = _layer_norm_rows(tsum, g_ref, b_ref, ti * t, lp, seq_len)
    ybuf_ref[slot] = y
    first, rows = out_rows(ti)
    bi = ti // nts
    tok0 = lax.rem(ti, nts) * t

    def send(j, c):
        src = pl.multiple_of(first + j * OUT_ROWS, OUT_ROWS)
        out_copy(slot, bi, src, pl.multiple_of(tok0 - N_META + src, OUT_ROWS)).start()
        return c
    lax.fori_loop(0, rows // OUT_ROWS, send, 0)

    @pl.when(ti == nt - 1)
    def _():
        wait_out(slot, ti)

        @pl.when(nt > 1)
        def _():
            wait_out(1 - slot, ti - 1)


def combine(starts, ye, pos, h, g, bb, lp, seq_len, final):
    n = h.shape[0]
    nt = n // TOK_TILE
    row = lambda i, st: (i, 0)
    fix = lambda i, st: (0, 0)
    scratch = [pltpu.VMEM((2, HALF_ROWS, D_MODEL), jnp.bfloat16), pltpu.VMEM((TOK_TILE, D_MODEL), jnp.float32),
               pltpu.SMEM((2,), jnp.int32), pltpu.SemaphoreType.DMA((2,))]
    if final:
        assert seq_len % OUT_ROWS == 0 and N_META % OUT_ROWS == 0
        out_specs = pl.BlockSpec(memory_space=pl.ANY)
        out_shape = jax.ShapeDtypeStruct((n // lp, seq_len - N_META, D_MODEL), jnp.float32)
        scratch += [pltpu.VMEM((2, TOK_TILE, D_MODEL), jnp.float32), pltpu.SemaphoreType.DMA((2,))]
    else:
        out_specs = [pl.BlockSpec((TOK_TILE, D_MODEL), row), pl.BlockSpec((TOK_TILE, D_MODEL), row)]
        out_shape = [jax.ShapeDtypeStruct((n, D_MODEL), jnp.float32), jax.ShapeDtypeStruct((n, D_MODEL), jnp.bfloat16)]
    grid_spec = pltpu.PrefetchScalarGridSpec(
        num_scalar_prefetch=1,
        grid=(nt,),
        in_specs=[pl.BlockSpec(memory_space=pl.ANY), pl.BlockSpec((1, N_EXPERTS, TOK_TILE), lambda i, st: (i, 0, 0)),
                  pl.BlockSpec((TOK_TILE, D_MODEL), row), pl.BlockSpec((1, D_MODEL), fix), pl.BlockSpec((1, D_MODEL), fix)],
        out_specs=out_specs,
        scratch_shapes=scratch,
    )
    return pl.pallas_call(
        functools.partial(_combine_kernel, lp=lp, seq_len=seq_len, final=final),
        grid_spec=grid_spec,
        out_shape=out_shape,
        compiler_params=_cparams("arbitrary"),
        name="moe_combine_out" if final else "moe_combine",
    )(starts, ye, pos, h, g, bb)


def moe_block(h, hb, aff, afft, w1, w3, w2, g, bb, n_real, lp, seq_len, final=False):
    cap = max(1, CAPACITY_FACTOR * n_real // N_EXPERTS)
    cap_pad = -(-(cap + DMA_ROWS) // FFN_ROWS) * FFN_ROWS
    pos, start3 = route(afft, cap)
    starts = start3[:, 0, :N_EXPERTS].reshape(-1)
    xe = dispatch(starts, hb, pos, aff, cap, cap_pad)
    ye = expert_ffn(xe, w1, w3, w2, cap)
    return combine(starts, ye, pos, h, g, bb, lp, seq_len, final)


def _prep_weights(meta, w_in, conv_w, conv_b, a_log_f, a_log_b, dt_bias_f, dt_bias_b, d_skip, ssd_norm_w,
                  lam_q1, lam_k1, lam_q2, lam_k2, subln_w, rel_bias, w_out, pool_w, pool_b, pool_scale,
                  ln1_g, ln1_b, ln2_g, ln2_b, w_router, w1, w3, w2):
    bf = jnp.bfloat16
    f32 = jnp.float32
    w = w_in[0]
    o = [0, SSD_INNER, SSD_INNER + SSD_CONV_CH, SSD_INNER + SSD_CONV_CH + 2 * SSD_HEADS]
    wz, wxbc, wdt = w[:, o[0]:o[1]], w[:, o[1]:o[2]], w[:, o[2]:o[3]]
    wq, wk, wv = (w[:, o[3] + i * ATT_QK:o[3] + (i + 1) * ATT_QK] for i in range(3))
    pad = jnp.zeros((D_MODEL, _PROJ_COLS - _COL_DT - 2 * SSD_HEADS), f32)
    w_tok = jnp.concatenate([wz, wxbc, wdt, pad], axis=1).astype(bf)
    w_qv = jnp.concatenate([wq * (ATT_HEAD_DIM ** -0.5 * LOG2E), wv], axis=1).T.astype(bf)
    lane_pad = lambda v: jnp.pad(v.astype(f32), (0, LANES - v.shape[0]))
    par = jnp.stack([lane_pad(jnp.concatenate([dt_bias_f[0], dt_bias_b[0]])),
                     lane_pad(jnp.concatenate([a_log_f[0], a_log_b[0]]))] + [jnp.zeros((LANES,), f32)] * 6)
    lam4 = jnp.stack([lane_pad(lam_q1[0]), lane_pad(lam_k1[0]), lane_pad(lam_q2[0]), lane_pad(lam_k2[0])]
                     + [jnp.zeros((LANES,), f32)] * 4)
    return dict(
        meta=meta, w_tok=w_tok, w_k=wk.astype(bf), w_qv=w_qv, conv_w=conv_w[0, :, 0, :], conv_b=conv_b[0][None], par=par,
        dsk=jnp.repeat(d_skip[0], SSD_HEAD_DIM)[None], nw=ssd_norm_w[0][None], lam4=lam4,
        subw=jnp.broadcast_to(subln_w[0][:, None], (ATT_V_DIM, SEQ_ALIGN)), rel_bias=rel_bias,
        w_out=w_out[0].astype(bf), pool_w=pool_w[0].astype(bf), pool_b=pool_b[0][None], pool_scale=pool_scale[0][None],
        ln1_g=ln1_g, ln1_b=ln1_b, ln2_g=ln2_g, ln2_b=ln2_b,
        w_r=jnp.pad(w_router, ((0, 0), (0, 0), (0, LANES - N_EXPERTS))).astype(bf),
        w1=w1.astype(bf), w3=w3.astype(bf), w2=w2.astype(bf))


def _trunk(x, p, bias5):
    b, s, _ = x.shape
    seq_len = s + N_META
    lp = -(-seq_len // SEQ_ALIGN) * SEQ_ALIGN
    n = b * lp
    n_real = b * seq_len
    meta = jnp.broadcast_to(p["meta"][None].astype(x.dtype), (b, N_META, D_MODEL))
    h = jnp.concatenate([meta, x, jnp.zeros((b, lp - seq_len, D_MODEL), x.dtype)], axis=1).reshape(n, D_MODEL)
    hb = h.astype(jnp.bfloat16)
    lambda_init = 0.8 - 0.6 * math.exp(-0.3 * 0)
    tm = next(c for c in (ROW_TILE, ROW_TILE // 2, SEQ_ALIGN) if n % c == 0)

    proj = mm_nn(hb, p["w_tok"], jnp.float32, tm, 512)
    k3 = mm_nn(hb, p["w_k"], jnp.bfloat16, tm, 512).reshape(b, lp, ATT_QK)
    qv = mm_nt_blocked(p["w_qv"], hb, jnp.bfloat16, tm, 512)
    proj3 = proj.reshape(b, lp, _PROJ_COLS)
    xbc = conv_silu(proj3, p["conv_w"], p["conv_b"]).reshape(n, SSD_CONV_CH)
    yf = ssd_scan(xbc, proj, p["par"], b, lp, seq_len, False)
    yb = ssd_scan(xbc, proj, p["par"], b, lp, seq_len, True)
    y_ssd = ssd_gate(yf, yb, xbc, proj, p["dsk"], p["nw"], min(tm, 2 * SEQ_ALIGN))
    y_att = diff_attention(qv, k3, bias5, p["lam4"], p["subw"], b, lp, seq_len, lambda_init)
    h, hb, aff, afft = out_proj_norm_route(y_ssd, y_att, p["w_out"], h, p["ln1_g"][0][None], p["ln1_b"][0][None],
                                           p["w_r"][0], lp, seq_len)
    h, hb = moe_block(h, hb, aff, afft, p["w1"][0], p["w3"][0], p["w2"][0], p["ln2_g"][0][None], p["ln2_b"][0][None],
                      n_real, lp, seq_len)
    d = pool_diff(h.reshape(b, lp, D_MODEL), seq_len).reshape(n, D_MODEL)
    h, hb, aff, afft = pool_proj_norm_route(d, p["pool_w"], p["pool_b"], p["pool_scale"], h, p["ln1_g"][1][None],
                                            p["ln1_b"][1][None], p["w_r"][1], lp, seq_len)
    return moe_block(h, hb, aff, afft, p["w1"][1], p["w3"][1], p["w2"][1], p["ln2_g"][1][None], p["ln2_b"][1][None],
                     n_real, lp, seq_len, final=True)


def kernel(x_prompt, x_sample, meta, w_in, conv_w, conv_b, a_log_f, a_log_b, dt_bias_f, dt_bias_b, d_skip, ssd_norm_w, lam_q1, lam_k1, lam_q2, lam_k2, subln_w, rel_bias, w_out, pool_w, pool_b, pool_scale, ln1_g, ln1_b, ln2_g, ln2_b, w_router, w1, w3, w2):
    p = _prep_weights(meta, w_in, conv_w, conv_b, a_log_f, a_log_b, dt_bias_f, dt_bias_b, d_skip, ssd_norm_w,
                      lam_q1, lam_k1, lam_q2, lam_k2, subln_w, rel_bias, w_out, pool_w, pool_b, pool_scale,
                      ln1_g, ln1_b, ln2_g, ln2_b, w_router, w1, w3, w2)
    bias5 = bias_tiles(rel_bias)
    return (_trunk(x_prompt, p, bias5), _trunk(x_sample, p, bias5))
```

```python
import functools
import math

import jax
import jax.numpy as jnp
from jax import lax
from jax.experimental import pallas as pl
from jax.experimental.pallas import tpu as pltpu

D_MODEL = 2048
DEPTH = 2
N_META = 16
SSD_HEADS = 16
SSD_HEAD_DIM = 64
SSD_INNER = SSD_HEADS * SSD_HEAD_DIM
SSD_GROUPS = 4
SSD_STATE = 128
SSD_CONV = 5
SSD_CHUNK = 128
SSD_CONV_CH = SSD_INNER + 2 * SSD_GROUPS * SSD_STATE
ATT_HEADS = 8
ATT_HEAD_DIM = 64
ATT_V_DIM = 2 * ATT_HEAD_DIM
ATT_QK = ATT_HEADS * 2 * ATT_HEAD_DIM
ATT_INNER = ATT_HEADS * ATT_V_DIM
REL_BUCKETS = 32
POOL_WINDOWS = (2, 4, 8, 16)
POOL_GROUP = D_MODEL // len(POOL_WINDOWS)
N_EXPERTS = 16
EXPERT_FF = 2048
CAPACITY_FACTOR = 2
ALPHA = (2 * DEPTH) ** 0.25
LN_EPS = 1e-5

LANES = 128
BF16_ROWS = 16
SEQ_ALIGN = 256
TOK_TILE = 256
ROW_TILE = 1024
DMA_ROWS = 32
OUT_ROWS = 16
FFN_ROWS = 704
FFN_COLS = 512
VMEM_LIMIT = 56 * 1024 * 1024

_BUCKET_STARTS = (1, 2, 3, 4, 5, 6, 7, 8, 12, 16, 23, 32, 46, 64, 91)
_COL_Z, _COL_XBC, _COL_DT, _PROJ_COLS = 0, 1024, 3072, 3584
ATT_CHUNK = 4
ATT_HEADS_PER_STEP = 2
LOG2E = math.log2(math.e)


def _cparams(*sem):
    return pltpu.CompilerParams(dimension_semantics=sem, vmem_limit_bytes=VMEM_LIMIT)


def _sigmoid(x):
    return 1.0 / (1.0 + jnp.exp(-x))


def _softplus(x):
    return jnp.maximum(x, 0.0) + jnp.log(1.0 + jnp.exp(-jnp.abs(x)))


def _split3(x):
    p1 = x.astype(jnp.bfloat16)
    r1 = x - p1.astype(jnp.float32)
    p2 = r1.astype(jnp.bfloat16)
    p3 = (r1 - p2.astype(jnp.float32)).astype(jnp.bfloat16)
    return p1, p2, p3


def _mm_nn_kernel(x_ref, w_ref, o_ref):
    o_ref[...] = jnp.dot(x_ref[...], w_ref[...], preferred_element_type=jnp.float32).astype(o_ref.dtype)


def mm_nn(x, w, out_dtype, tm, tn):
    n, k = x.shape
    nn = w.shape[1]
    return pl.pallas_call(
        _mm_nn_kernel,
        grid=(n // tm, nn // tn),
        in_specs=[pl.BlockSpec((tm, k), lambda i, j: (i, 0)), pl.BlockSpec((k, tn), lambda i, j: (0, j))],
        out_specs=pl.BlockSpec((tm, tn), lambda i, j: (i, j)),
        out_shape=jax.ShapeDtypeStruct((n, nn), out_dtype),
        compiler_params=_cparams("parallel", "arbitrary"),
        name="mm_nn",
    )(x, w)


def _mm_nt_kernel(w_ref, x_ref, o_ref):
    r = lax.dot_general(w_ref[...], x_ref[...], (((1,), (1,)), ((), ())), preferred_element_type=jnp.float32)
    for s in range(o_ref.shape[0]):
        o_ref[s] = r[:, s * SEQ_ALIGN:(s + 1) * SEQ_ALIGN].astype(o_ref.dtype)


def mm_nt_blocked(wt, x, out_dtype, tm, tn):
    n, k = x.shape
    nn = wt.shape[0]
    per = tm // SEQ_ALIGN
    return pl.pallas_call(
        _mm_nt_kernel,
        grid=(n // tm, nn // tn),
        in_specs=[pl.BlockSpec((tn, k), lambda i, j: (j, 0)), pl.BlockSpec((tm, k), lambda i, j: (i, 0))],
        out_specs=pl.BlockSpec((per, tn, SEQ_ALIGN), lambda i, j: (i, j, 0)),
        out_shape=jax.ShapeDtypeStruct((n // SEQ_ALIGN, nn, SEQ_ALIGN), out_dtype),
        compiler_params=_cparams("parallel", "arbitrary"),
        name="mm_nt",
    )(wt, x)


def _conv_kernel(x_ref, w_ref, b_ref, o_ref, pad_ref, *, rows):
    lp = x_ref.shape[1]
    halo = 8
    pad_ref[0:halo, :] = jnp.zeros((halo, LANES), jnp.float32)
    pad_ref[lp + halo:lp + 2 * halo, :] = jnp.zeros((halo, LANES), jnp.float32)
    pad_ref[halo:lp + halo, :] = x_ref[0]
    for t0 in range(0, lp, rows):
        acc = jnp.broadcast_to(b_ref[...], (rows, LANES))
        for k in range(SSD_CONV):
            off = t0 + halo - SSD_CONV // 2 + k
            acc = acc + pad_ref[off:off + rows, :] * w_ref[k:k + 1, :]
        o_ref[0, t0:t0 + rows, :] = acc * _sigmoid(acc)


def conv_silu(proj3, conv_w, conv_b):
    b, lp, _ = proj3.shape
    c0 = _COL_XBC // LANES
    return pl.pallas_call(
        functools.partial(_conv_kernel, rows=SEQ_ALIGN),
        grid=(b, SSD_CONV_CH // LANES),
        in_specs=[
            pl.BlockSpec((1, lp, LANES), lambda i, j: (i, 0, c0 + j)),
            pl.BlockSpec((SSD_CONV, LANES), lambda i, j: (0, j)),
            pl.BlockSpec((1, LANES), lambda i, j: (0, j)),
        ],
        out_specs=pl.BlockSpec((1, lp, LANES), lambda i, j: (i, 0, j)),
        out_shape=jax.ShapeDtypeStruct((b, lp, SSD_CONV_CH), jnp.float32),
        scratch_shapes=[pltpu.VMEM((lp + 16, LANES), jnp.float32)],
        compiler_params=_cparams("parallel", "parallel"),
        name="conv_silu",
    )(proj3, conv_w, conv_b)


def _ssd_kernel(x_ref, b_ref, c_ref, dt_ref, par_ref, *rest, seq_len, reverse, gated):
    if gated:
        yo_ref, z_ref, dsk_ref, nw_ref, y_ref, h_ref = rest
    else:
        y_ref, h_ref = rest
    cs = SSD_CHUNK
    step = pl.program_id(1)
    chunk = (pl.num_programs(1) - 1 - step) if reverse else step
    lane0 = SSD_HEADS if reverse else 0

    @pl.when(step == 0)
    def _():
        h_ref[...] = jnp.zeros(h_ref.shape, jnp.float32)

    sub = lax.broadcasted_iota(jnp.int32, (cs, cs), 0)
    lan = lax.broadcasted_iota(jnp.int32, (cs, cs), 1)
    valid = (chunk * cs + sub[:, 0:1]) < seq_len
    dt = jnp.where(valid, _softplus(dt_ref[...] + par_ref[0:1, :]), 0.0)
    a = dt * -jnp.exp(par_ref[1:2, :])
    a_t = a.T[0:2 * SSD_HEADS, :]
    if reverse:
        incl, rest = (sub >= lan), (sub < lan)
    else:
        incl, rest = (sub <= lan), (sub > lan)
    tri = jnp.concatenate([incl.astype(jnp.bfloat16), rest.astype(jnp.bfloat16),
                           jnp.ones((cs, cs), jnp.bfloat16)], axis=1)
    pieces = jnp.concatenate(_split3(a_t), axis=0)
    sums = jnp.dot(pieces, tri, preferred_element_type=jnp.float32)
    nh2 = 2 * SSD_HEADS
    sums = sums[0:nh2] + sums[nh2:2 * nh2] + sums[2 * nh2:3 * nh2]
    cum_row, rest_row, tot_row = sums[:, 0:cs], sums[:, cs:2 * cs], sums[:, 2 * cs:3 * cs]
    cum_col = jnp.concatenate([cum_row, jnp.zeros((cs - nh2, cs), jnp.float32)], axis=0).T
    mask = (sub <= lan) if reverse else (sub >= lan)
    left = lan[0:1, :] < SSD_HEAD_DIM
    gated_pairs = []

    for g in range(SSD_GROUPS):
        bg = b_ref[:, g * SSD_STATE:(g + 1) * SSD_STATE]
        cg = c_ref[:, g * SSD_STATE:(g + 1) * SSD_STATE]
        cb = lax.dot_general(cg.astype(jnp.bfloat16), bg.astype(jnp.bfloat16), (((1,), (1,)), ((), ())),
                             preferred_element_type=jnp.float32)
        bg_t = bg.T
        for pp in range(2):
            p = 2 * g + pp
            r0, r1 = lane0 + 2 * p, lane0 + 2 * p + 1
            dtp = jnp.where(left, dt[:, r0:r0 + 1], dt[:, r1:r1 + 1])
            xs = (x_ref[:, p * LANES:(p + 1) * LANES] * dtp).astype(jnp.bfloat16)
            h_t = h_ref[p]
            rhs = jnp.concatenate([xs, h_t.astype(jnp.bfloat16)], axis=0)
            ys, ss = [], []
            for r in (r0, r1):
                cc = cum_col[:, r:r + 1]
                dec = jnp.exp(jnp.where(mask, cc - cum_row[r:r + 1, :], -jnp.inf))
                lhs = jnp.concatenate([(cb * dec).astype(jnp.bfloat16),
                                       (cg * jnp.exp(cc)).astype(jnp.bfloat16)], axis=1)
                ys.append(jnp.dot(lhs, rhs, preferred_element_type=jnp.float32))
                bw = (bg_t * jnp.exp(rest_row[r:r + 1, :])).astype(jnp.bfloat16)
                ss.append(jnp.dot(bw, xs, preferred_element_type=jnp.float32))
            lanes = slice(p * LANES, (p + 1) * LANES)
            y_pair = jnp.where(left, ys[0], ys[1])
            if gated:
                z = z_ref[:, lanes]
                gated_pairs.append((yo_ref[:, lanes] + y_pair + dsk_ref[:, lanes] * x_ref[:, lanes]) * (z * _sigmoid(z)))
            else:
                y_ref[:, lanes] = y_pair
            decay = jnp.where(left, jnp.exp(tot_row[r0:r0 + 1, :]), jnp.exp(tot_row[r1:r1 + 1, :]))
            h_ref[p] = decay * h_t + jnp.where(left, ss[0], ss[1])
        if gated:
            seg = jnp.concatenate(gated_pairs, axis=1)
            gated_pairs = []
            ms = jnp.mean(seg * seg, axis=-1, keepdims=True)
            cols = slice(2 * g * LANES, 2 * (g + 1) * LANES)
            y_ref[:, cols] = (seg * lax.rsqrt(ms + LN_EPS) * nw_ref[:, cols]).astype(y_ref.dtype)


def ssd_scan(xbc, proj, par, b, lp, seq_len, reverse, gate=None):
    n = xbc.shape[0]
    nc = lp // SSD_CHUNK
    cdt = _COL_DT // LANES

    def row(i, c):
        return i * nc + ((nc - 1 - c) if reverse else c)

    tok = lambda col: (lambda i, c: (row(i, c), col))
    fix = lambda i, c: (0, 0)
    in_specs = [
        pl.BlockSpec((SSD_CHUNK, SSD_INNER), tok(0)),
        pl.BlockSpec((SSD_CHUNK, SSD_GROUPS * SSD_STATE), tok(2)),
        pl.BlockSpec((SSD_CHUNK, SSD_GROUPS * SSD_STATE), tok(3)),
        pl.BlockSpec((SSD_CHUNK, LANES), tok(cdt)),
        pl.BlockSpec((8, LANES), fix),
    ]
    args = [xbc, xbc, xbc, proj, par]
    if gate is not None:
        y_other, dsk, nw = gate
        in_specs += [pl.BlockSpec((SSD_CHUNK, SSD_INNER), tok(0)), pl.BlockSpec((SSD_CHUNK, SSD_INNER), tok(0)),
                     pl.BlockSpec((1, SSD_INNER), fix), pl.BlockSpec((1, SSD_INNER), fix)]
        args += [y_other, proj, dsk, nw]
    return pl.pallas_call(
        functools.partial(_ssd_kernel, seq_len=seq_len, reverse=reverse, gated=gate is not None),
        grid=(b, nc),
        in_specs=in_specs,
        out_specs=pl.BlockSpec((SSD_CHUNK, SSD_INNER), tok(0)),
        out_shape=jax.ShapeDtypeStruct((n, SSD_INNER), jnp.float32 if gate is None else jnp.bfloat16),
        scratch_shapes=[pltpu.VMEM((SSD_HEADS // 2, SSD_STATE, LANES), jnp.float32)],
        compiler_params=_cparams("parallel", "arbitrary"),
        name="ssd_bwd" if reverse else "ssd_fwd",
    )(*args)


def _bias_kernel(rb_ref, o_ref):
    h = pl.program_id(0)
    d = pl.program_id(1) - 2
    t = SEQ_ALIGN
    kk = lax.broadcasted_iota(jnp.int32, (t, t), 0)
    qq = lax.broadcasted_iota(jnp.int32, (t, t), 1)
    rel = d * t + kk - qq
    n = jnp.abs(rel)
    half = REL_BUCKETS // 2
    vneg = jnp.full((t, t), rb_ref[0, h], jnp.float32)
    vpos = jnp.full((t, t), rb_ref[half, h], jnp.float32)
    for j, start in enumerate(_BUCKET_STARTS, start=1):
        ge = n >= start
        vneg = jnp.where(ge, rb_ref[j, h], vneg)
        vpos = jnp.where(ge, rb_ref[half + j, h], vpos)
    o_ref[0, 0] = jnp.where(rel > 0, vpos, vneg) * LOG2E


def bias_tiles(rel_bias):
    return pl.pallas_call(
        _bias_kernel,
        grid=(ATT_HEADS, 5),
        in_specs=[pl.BlockSpec(memory_space=pltpu.SMEM)],
        out_specs=pl.BlockSpec((1, 1, SEQ_ALIGN, SEQ_ALIGN), lambda h, d: (h, d, 0, 0)),
        out_shape=jax.ShapeDtypeStruct((ATT_HEADS, 5, SEQ_ALIGN, SEQ_ALIGN), jnp.float32),
        compiler_params=_cparams("parallel", "parallel"),
        name="bias_tiles",
    )(rel_bias)


def _attn_kernel(q_ref, v_ref, k_ref, bias_ref, lam_ref, sub_ref, o_ref, s_ref, *, seq_len, lambda_init):
    t = SEQ_ALIGN
    group = ATT_CHUNK
    vd = ATT_V_DIM
    qb = pl.program_id(2)
    sub = lax.broadcasted_iota(jnp.int32, (vd, t), 0)
    qs = []
    for hh in range(ATT_HEADS_PER_STEP):
        q_t = q_ref[0, hh * vd:(hh + 1) * vd, :]
        zero = jnp.zeros_like(q_t)
        qs += [jnp.where(sub < ATT_HEAD_DIM, q_t, zero), jnp.where(sub >= ATT_HEAD_DIM, q_t, zero)]
    n_maps = len(qs)
    n_full = seq_len // t
    tail = seq_len - n_full * t
    tail_rows = -(-tail // LANES) * LANES

    def fold_max(x):
        return jnp.max(x.reshape(x.shape[0] // 8, 8, t), axis=0)

    def score_block(kb, rows, mx):
        r0 = kb * t
        if not isinstance(kb, int):
            r0 = pl.multiple_of(r0, t)
        tile = jnp.clip(kb - qb, -2, 2) + 2
        out = []
        for mi in range(n_maps):
            hh = mi // 2
            kblk = k_ref[0, pl.ds(r0, rows), hh * vd:(hh + 1) * vd]
            s = jnp.dot(kblk, qs[mi], preferred_element_type=jnp.float32) + bias_ref[hh, tile, 0:rows, :]
            if rows != t:
                s = jnp.where(lax.broadcasted_iota(jnp.int32, (rows, 1), 0) < tail, s, -jnp.inf)
            s_ref[mi, pl.ds(r0, rows), :] = s
            out.append(jnp.maximum(mx[mi], fold_max(s)))
        return tuple(out)

    neg = jnp.full((8, t), -jnp.inf, jnp.float32)

    mx = (neg,) * n_maps
    for kb in range(n_full):
        mx = score_block(kb, t, mx)
    if tail:
        mx = score_block(n_full, tail_rows, mx)
    ms_ = [jnp.max(m, axis=0, keepdims=True) for m in mx]

    def prob_blocks(kbs, rows, accs):
        accs = list(accs)
        ones = jnp.ones((BF16_ROWS, rows * len(kbs)), jnp.bfloat16)
        for hh in range(ATT_HEADS_PER_STEP):
            vcat = jnp.concatenate([v_ref[kb][hh * vd:(hh + 1) * vd, 0:rows] for kb in kbs] , axis=1)
            vaug = jnp.concatenate([vcat, ones], axis=0)
            for mi in (2 * hh, 2 * hh + 1):
                ps = []
                for kb in kbs:
                    r0 = kb * t
                    if not isinstance(kb, int):
                        r0 = pl.multiple_of(r0, t)
                    ps.append(jnp.exp2((s_ref[mi, pl.ds(r0, rows), :] - ms_[mi]).astype(jnp.bfloat16)))
                pcat = jnp.concatenate(ps, axis=0) if len(ps) > 1 else ps[0]
                accs[mi] = accs[mi] + jnp.dot(vaug, pcat, preferred_element_type=jnp.float32)
        return tuple(accs)

    zacc = jnp.zeros((vd + BF16_ROWS, t), jnp.float32)
    accs = (zacc,) * n_maps
    for k0 in range(0, n_full, group):
        accs = prob_blocks(list(range(k0, min(k0 + group, n_full))), t, accs)
    if tail:
        accs = prob_blocks([n_full], tail_rows, accs)
    lv = lam_ref[...]
    lam = (jnp.exp(jnp.sum(lv[0:1] * lv[1:2], axis=-1, keepdims=True))
           - jnp.exp(jnp.sum(lv[2:3] * lv[3:4], axis=-1, keepdims=True)) + lambda_init)
    for hh in range(ATT_HEADS_PER_STEP):
        a1, a2 = accs[2 * hh], accs[2 * hh + 1]
        o = a1[0:vd] * (1.0 / a1[vd:vd + 1]) - lam * (a2[0:vd] * (1.0 / a2[vd:vd + 1]))
        ms = jnp.mean(o * o, axis=0, keepdims=True)
        o = o * lax.rsqrt(ms + LN_EPS) * sub_ref[...] * (1.0 - lambda_init)
        o_ref[:, hh * vd:(hh + 1) * vd] = o.T.astype(o_ref.dtype)


def diff_attention(qv, k3, bias5, lam4, subw, b, lp, seq_len, lambda_init):
    nq = lp // SEQ_ALIGN
    hs = ATT_HEADS_PER_STEP
    return pl.pallas_call(
        functools.partial(_attn_kernel, seq_len=seq_len, lambda_init=lambda_init),
        grid=(b, ATT_HEADS // hs, nq),
        in_specs=[
            pl.BlockSpec((1, hs * ATT_V_DIM, SEQ_ALIGN), lambda i, h, q: (i * nq + q, h, 0)),
            pl.BlockSpec((nq, hs * ATT_V_DIM, SEQ_ALIGN), lambda i, h, q: (i, ATT_HEADS // hs + h, 0)),
            pl.BlockSpec((1, lp, hs * ATT_V_DIM), lambda i, h, q: (i, 0, h)),
            pl.BlockSpec((hs, 5, SEQ_ALIGN, SEQ_ALIGN), lambda i, h, q: (h, 0, 0, 0)),
            pl.BlockSpec((8, LANES), lambda i, h, q: (0, 0)),
            pl.BlockSpec((ATT_V_DIM, SEQ_ALIGN), lambda i, h, q: (0, 0)),
        ],
        out_specs=pl.BlockSpec((SEQ_ALIGN, hs * ATT_V_DIM), lambda i, h, q: (i * nq + q, h)),
        out_shape=jax.ShapeDtypeStruct((b * lp, ATT_INNER), jnp.bfloat16),
        scratch_shapes=[pltpu.VMEM((2 * hs, lp, SEQ_ALIGN), jnp.float32)],
        compiler_params=_cparams("parallel", "parallel", "arbitrary"),
        name="diff_attn",
    )(qv, qv, k3, bias5, lam4, subw)


def _layer_norm_rows(t, g_ref, b_ref, tile_row0, lp, seq_len):
    rows = t.shape[0]
    mu = jnp.mean(t, axis=-1, keepdims=True)
    tc = t - mu
    var = jnp.mean(tc * tc, axis=-1, keepdims=True)
    y = tc * lax.rsqrt(var + LN_EPS) * g_ref[...] + b_ref[...]
    pos = tile_row0 % lp + lax.broadcasted_iota(jnp.int32, (rows, 1), 0)
    valid = pos < seq_len
    return jnp.where(valid, y, 0.0), valid


def _norm_route(t, g_ref, b_ref, wr_ref, tile_row0, lp, seq_len, h_ref, hb_ref, aff_ref, afft_ref):
    y, valid = _layer_norm_rows(t, g_ref, b_ref, tile_row0, lp, seq_len)
    h_ref[...] = y
    yb = y.astype(jnp.bfloat16)
    hb_ref[...] = yb
    if aff_ref is not None:
        logits = jnp.dot(yb, wr_ref[...], preferred_element_type=jnp.float32)
        lane = lax.broadcasted_iota(jnp.int32, logits.shape, 1)
        is_e = lane < N_EXPERTS
        logits = jnp.where(is_e, logits, -jnp.inf)
        e = jnp.exp(logits - jnp.max(logits, axis=-1, keepdims=True))
        aff = e / jnp.sum(e, axis=-1, keepdims=True)
        aff = jnp.where(valid, aff, jnp.where(is_e, -1.0, 0.0))
        aff_ref[...] = aff
        afft_ref[0] = aff.T[0:N_EXPERTS, :]


def _outproj_kernel(ys_ref, ya_ref, w_ref, h_ref, g_ref, b_ref, wr_ref, ho_ref, hb_ref, aff_ref, afft_ref, *, lp, seq_len):
    mix = (jnp.dot(ys_ref[...], w_ref[0:SSD_INNER, :], preferred_element_type=jnp.float32)
           + jnp.dot(ya_ref[...], w_ref[SSD_INNER:, :], preferred_element_type=jnp.float32))
    t = ALPHA * h_ref[...] + mix
    _norm_route(t, g_ref, b_ref, wr_ref, pl.program_id(0) * h_ref.shape[0], lp, seq_len, ho_ref, hb_ref, aff_ref, afft_ref)


def _poolproj_kernel(d_ref, w_ref, pb_ref, ps_ref, h_ref, g_ref, b_ref, wr_ref, ho_ref, hb_ref, aff_ref, afft_ref, *, lp, seq_len):
    outs = []
    for gi in range(len(POOL_WINDOWS)):
        outs.append(jnp.dot(d_ref[:, gi * POOL_GROUP:(gi + 1) * POOL_GROUP], w_ref[gi], preferred_element_type=jnp.float32))
    mix = (jnp.concatenate(outs, axis=-1) + pb_ref[...]) * ps_ref[...]
    t = ALPHA * h_ref[...] + mix
    _norm_route(t, g_ref, b_ref, wr_ref, pl.program_id(0) * h_ref.shape[0], lp, seq_len, ho_ref, hb_ref, aff_ref, afft_ref)


def _mix_outputs(n, tm):
    row = lambda i: (i, 0)
    specs = [pl.BlockSpec((tm, D_MODEL), row), pl.BlockSpec((tm, D_MODEL), row), pl.BlockSpec((tm, LANES), row),
             pl.BlockSpec((1, N_EXPERTS, tm), lambda i: (i, 0, 0))]
    shapes = [jax.ShapeDtypeStruct((n, D_MODEL), jnp.float32), jax.ShapeDtypeStruct((n, D_MODEL), jnp.bfloat16),
              jax.ShapeDtypeStruct((n, LANES), jnp.float32), jax.ShapeDtypeStruct((n // tm, N_EXPERTS, tm), jnp.float32)]
    return specs, shapes


def out_proj_norm_route(y_ssd, y_att, w_out, h, g, bb, w_r, lp, seq_len):
    n = h.shape[0]
    tm = TOK_TILE
    row = lambda i: (i, 0)
    fix = lambda i: (0, 0)
    specs, shapes = _mix_outputs(n, tm)
    return pl.pallas_call(
        functools.partial(_outproj_kernel, lp=lp, seq_len=seq_len),
        grid=(n // tm,),
        in_specs=[pl.BlockSpec((tm, SSD_INNER), row), pl.BlockSpec((tm, ATT_INNER), row),
                  pl.BlockSpec((D_MODEL, D_MODEL), fix), pl.BlockSpec((tm, D_MODEL), row),
                  pl.BlockSpec((1, D_MODEL), fix), pl.BlockSpec((1, D_MODEL), fix), pl.BlockSpec((D_MODEL, LANES), fix)],
        out_specs=specs, out_shape=shapes,
        compiler_params=_cparams("parallel"),
        name="outproj_ln_route",
    )(y_ssd, y_att, w_out, h, g, bb, w_r)


def pool_proj_norm_route(d, pool_w, pool_b, pool_scale, h, g, bb, w_r, lp, seq_len):
    n = h.shape[0]
    tm = TOK_TILE
    row = lambda i: (i, 0)
    fix = lambda i: (0, 0)
    specs, shapes = _mix_outputs(n, tm)
    return pl.pallas_call(
        functools.partial(_poolproj_kernel, lp=lp, seq_len=seq_len),
        grid=(n // tm,),
        in_specs=[pl.BlockSpec((tm, D_MODEL), row),
                  pl.BlockSpec((len(POOL_WINDOWS), POOL_GROUP, POOL_GROUP), lambda i: (0, 0, 0)),
                  pl.BlockSpec((1, D_MODEL), fix), pl.BlockSpec((1, D_MODEL), fix), pl.BlockSpec((tm, D_MODEL), row),
                  pl.BlockSpec((1, D_MODEL), fix), pl.BlockSpec((1, D_MODEL), fix), pl.BlockSpec((D_MODEL, LANES), fix)],
        out_specs=specs, out_shape=shapes,
        compiler_params=_cparams("parallel"),
        name="poolproj_ln_route",
    )(d, pool_w, pool_b, pool_scale, h, g, bb, w_r)


def _pool_kernel(x_ref, o_ref, s0_ref, s1_ref, *, seq_len):
    lp = x_ref.shape[1]
    halo = 16
    gi = pl.program_id(1) // (POOL_GROUP // LANES)
    zeros = jnp.zeros((halo, LANES), jnp.float32)
    for ref in (s0_ref, s1_ref):
        ref[0:halo, :] = zeros
        ref[lp + halo:lp + 2 * halo, :] = zeros
    s0_ref[halo:lp + halo, :] = x_ref[0]
    t = lax.broadcasted_iota(jnp.int32, (lp, 1), 0)
    for wi, w in enumerate(POOL_WINDOWS):
        @pl.when(gi == wi)
        def _(w=w):
            src, dst = s0_ref, s1_ref
            width = 1
            while width < w:
                dst[halo:lp + halo, :] = src[halo:lp + halo, :] + src[halo + width:lp + halo + width, :]
                dst[0:halo, :] = src[0:halo, :] + src[width:halo + width, :]
                src, dst = dst, src
                width *= 2
            win = src[halo - w // 2:lp + halo - w // 2, :]
            cnt = jnp.minimum(t + w // 2 - 1, seq_len - 1) - jnp.maximum(t - w // 2, 0) + 1
            d = win / cnt.astype(jnp.float32) - x_ref[0]
            o_ref[0] = jnp.where(t < seq_len, d, 0.0).astype(o_ref.dtype)


def pool_diff(h3, seq_len):
    b, lp, _ = h3.shape
    return pl.pallas_call(
        functools.partial(_pool_kernel, seq_len=seq_len),
        grid=(b, D_MODEL // LANES),
        in_specs=[pl.BlockSpec((1, lp, LANES), lambda i, j: (i, 0, j))],
        out_specs=pl.BlockSpec((1, lp, LANES), lambda i, j: (i, 0, j)),
        out_shape=jax.ShapeDtypeStruct((b, lp, D_MODEL), jnp.bfloat16),
        scratch_shapes=[pltpu.VMEM((lp + 32, LANES), jnp.float32)] * 2,
        compiler_params=_cparams("parallel", "parallel"),
        name="pool_diff",
    )(h3)


def _route_kernel(aff_ref, pos_ref, start_ref, *, cap):
    nt = aff_ref.shape[0]
    t = TOK_TILE

    def bits_of(v):
        return lax.bitcast_convert_type(v, jnp.int32)

    def count(mask):
        return jnp.sum(jnp.sum(jnp.where(mask, 1.0, 0.0), axis=0), axis=1, keepdims=True)

    def search(i, thr):
        cand = thr | jnp.left_shift(jnp.int32(1), 30 - i)
        cnt = count(bits_of(aff_ref[...]) >= cand[None])
        return jnp.where(cnt >= cap, cand, thr)

    thr = lax.fori_loop(0, 31, search, jnp.zeros((N_EXPERTS, 1), jnp.int32))
    need = cap - count(bits_of(aff_ref[...]) > thr[None])

    before = (lax.broadcasted_iota(jnp.int32, (t, t), 0) < lax.broadcasted_iota(jnp.int32, (t, t), 1)).astype(jnp.bfloat16)
    ones8 = jnp.ones((8, t), jnp.bfloat16)
    zpad = jnp.zeros((LANES - N_EXPERTS, t), jnp.float32)

    def tile(ti, carry):
        c_tie, c_sel, c_lane = carry
        b = bits_of(aff_ref[ti])
        tie = jnp.where(b == thr, 1.0, 0.0)
        rank = jnp.dot(tie.astype(jnp.bfloat16), before, preferred_element_type=jnp.float32) + c_tie
        sel = jnp.where(b > thr, 1.0, jnp.where(rank < need, tie, 0.0))
        slot = jnp.dot(sel.astype(jnp.bfloat16), before, preferred_element_type=jnp.float32) + c_sel
        pos_ref[ti] = jnp.where(sel > 0.0, slot, -1.0).astype(jnp.int32)
        start_ref[ti] = c_lane.astype(jnp.int32)
        selp = jnp.concatenate([sel, zpad], axis=0).astype(jnp.bfloat16)
        cnt_lane = lax.dot_general(ones8, selp, (((1,), (1,)), ((), ())), preferred_element_type=jnp.float32)
        return (c_tie + jnp.sum(tie, axis=1, keepdims=True), c_sel + jnp.sum(sel, axis=1, keepdims=True),
                c_lane + cnt_lane[0:1])

    zc = jnp.zeros((N_EXPERTS, 1), jnp.float32)
    _, _, c_lane = lax.fori_loop(0, nt, tile, (zc, zc, jnp.zeros((1, LANES), jnp.float32)))
    start_ref[nt] = c_lane.astype(jnp.int32)


def route(afft, cap):
    nt = afft.shape[0]
    return pl.pallas_call(
        functools.partial(_route_kernel, cap=cap),
        out_shape=[jax.ShapeDtypeStruct((nt, N_EXPERTS, TOK_TILE), jnp.int32),
                   jax.ShapeDtypeStruct((nt + 1, 1, LANES), jnp.int32)],
        compiler_params=pltpu.CompilerParams(vmem_limit_bytes=VMEM_LIMIT),
        name="route",
    )(afft)


HALF_EXPERTS = N_EXPERTS // 2
_REGION_MAX = -(-(TOK_TILE + BF16_ROWS - 1) // DMA_ROWS) * DMA_ROWS
HALF_ROWS = HALF_EXPERTS * _REGION_MAX + SEQ_ALIGN
XE_COLS = D_MODEL + LANES


def _tile_plan(st_ref, ti, half):
    plan = []
    off = jnp.int32(0)
    for e in range(half * HALF_EXPERTS, (half + 1) * HALF_EXPERTS):
        s = st_ref[ti * N_EXPERTS + e]
        end = st_ref[(ti + 1) * N_EXPERTS + e]
        s16 = s - (s & (BF16_ROWS - 1))
        nch = jnp.where(end > s, lax.shift_right_logical(end - s16 + DMA_ROWS - 1, DMA_ROWS.bit_length() - 1), 0)
        plan.append((e, s16, nch, off, end))
        off = off + nch * DMA_ROWS
    return plan, off


def _stage_rows(pos_half, plan):
    sub = lax.broadcasted_iota(jnp.int32, (HALF_EXPERTS, 1), 0)
    delta = jnp.zeros((HALF_EXPERTS, 1), jnp.int32)
    for i, (e, s16, nch, off, end) in enumerate(plan):
        delta = jnp.where(sub == i, off - s16, delta)
    return jnp.where(pos_half >= 0, pos_half + delta, -1)


def _onehot_rows(srow, k0):
    k = k0 + lax.broadcasted_iota(jnp.int32, (SEQ_ALIGN, 1), 0)
    onehot = jnp.zeros((SEQ_ALIGN, srow.shape[1]), jnp.float32)
    for i in range(srow.shape[0]):
        onehot = jnp.where(srow[i:i + 1, :] == k, 1.0, onehot)
    return onehot.astype(jnp.bfloat16)


def _n_blocks(rows):
    return lax.shift_right_logical(rows + SEQ_ALIGN - 1, SEQ_ALIGN.bit_length() - 1)


def _dispatch_kernel(st_ref, x_ref, pos_ref, aff_ref, xe_ref, sx_ref, cx_ref, n_ref, sem_ref, *, cap, cap_pad):
    ti = pl.program_id(0)
    nt = pl.num_programs(0)

    def chunk_copy(half, e, s16, off, j):
        src = pl.multiple_of(off + j * DMA_ROWS, DMA_ROWS)
        dst = pl.multiple_of(s16 + j * DMA_ROWS, BF16_ROWS)
        return pltpu.make_async_copy(sx_ref.at[half, pl.ds(src, DMA_ROWS)], xe_ref.at[e, pl.ds(dst, DMA_ROWS)],
                                     sem_ref.at[half])

    def wait_chunks(half, count):
        def body(_, c):
            chunk_copy(half, 0, 0, 0, 0).wait()
            return c
        lax.fori_loop(0, count, body, 0)

    tail0 = (cap // DMA_ROWS) * DMA_ROWS
    n_tail = (cap_pad - tail0) // DMA_ROWS

    @pl.when(ti == 0)
    def _():
        cx_ref[...] = jnp.zeros(cx_ref.shape, cx_ref.dtype)
        sx_ref[0, 0:DMA_ROWS, :] = jnp.zeros((DMA_ROWS, XE_COLS), sx_ref.dtype)
        for e in range(N_EXPERTS):
            for j in range(n_tail):
                pltpu.make_async_copy(sx_ref.at[0, pl.ds(0, DMA_ROWS)],
                                      xe_ref.at[e, pl.ds(tail0 + j * DMA_ROWS, DMA_ROWS)], sem_ref.at[0]).start()
        wait_chunks(0, N_EXPERTS * n_tail)
        n_ref[0] = 0
        n_ref[1] = 0

    aff = aff_ref[...]
    a1, a2, a3 = _split3(aff)
    gate3 = (a1.astype(jnp.float32) + pltpu.roll(a2.astype(jnp.float32), N_EXPERTS, 1)
             + pltpu.roll(a3.astype(jnp.float32), 2 * N_EXPERTS, 1)).astype(jnp.bfloat16)
    xt = jnp.concatenate([x_ref[...], gate3], axis=1)

    for half in range(2):
        plan, k_tot = _tile_plan(st_ref, ti, half)
        srow = _stage_rows(pos_ref[0, half * HALF_EXPERTS:(half + 1) * HALF_EXPERTS, :], plan)
        wait_chunks(half, n_ref[half])

        def permute(kc, c, half=half, srow=srow):
            k0 = pl.multiple_of(kc * SEQ_ALIGN, SEQ_ALIGN)
            sx_ref[half, pl.ds(k0, SEQ_ALIGN), :] = jnp.dot(_onehot_rows(srow, k0), xt,
                                                            preferred_element_type=jnp.float32).astype(sx_ref.dtype)
            return c

        lax.fori_loop(0, _n_blocks(k_tot), permute, 0)

        total = jnp.int32(0)
        for e, s16, nch, off, end in plan:
            @pl.when(nch > 0)
            def _(half=half, e=e, s16=s16, off=off, end=end):
                head = pl.ds(pl.multiple_of(off, BF16_ROWS), BF16_ROWS)
                sx_ref[half, head, :] = sx_ref[half, head, :] + cx_ref[e]
                part = end & (BF16_ROWS - 1)
                last = pl.ds(pl.multiple_of(off + (end - part) - s16, BF16_ROWS), BF16_ROWS)
                cx_ref[e] = jnp.where(part > 0, sx_ref[half, last, :], jnp.zeros((BF16_ROWS, XE_COLS), cx_ref.dtype))

            def issue(j, c, half=half, e=e, s16=s16, off=off):
                chunk_copy(half, e, s16, off, j).start()
                return c
            lax.fori_loop(0, nch, issue, 0)
            total = total + nch
        n_ref[half] = total

    @pl.when(ti == nt - 1)
    def _():
        wait_chunks(0, n_ref[0])
        wait_chunks(1, n_ref[1])


def dispatch(starts, hb, pos, aff, cap, cap_pad):
    n = hb.shape[0]
    nt = n // TOK_TILE
    grid_spec = pltpu.PrefetchScalarGridSpec(
        num_scalar_prefetch=1,
        grid=(nt,),
        in_specs=[pl.BlockSpec((TOK_TILE, D_MODEL), lambda i, st: (i, 0)),
                  pl.BlockSpec((1, N_EXPERTS, TOK_TILE), lambda i, st: (i, 0, 0)),
                  pl.BlockSpec((TOK_TILE, LANES), lambda i, st: (i, 0))],
        out_specs=pl.BlockSpec(memory_space=pl.ANY),
        scratch_shapes=[pltpu.VMEM((2, HALF_ROWS, XE_COLS), jnp.bfloat16),
                        pltpu.VMEM((N_EXPERTS, BF16_ROWS, XE_COLS), jnp.bfloat16),
                        pltpu.SMEM((2,), jnp.int32), pltpu.SemaphoreType.DMA((2,))],
    )
    return pl.pallas_call(
        functools.partial(_dispatch_kernel, cap=cap, cap_pad=cap_pad),
        grid_spec=grid_spec,
        out_shape=jax.ShapeDtypeStruct((N_EXPERTS, cap_pad, XE_COLS), jnp.bfloat16),
        compiler_params=_cparams("arbitrary"),
        name="moe_dispatch",
    )(starts, hb, pos, aff)


def _ffn_kernel(x_ref, w1_ref, w3_ref, w2_ref, o_ref, acc_ref, *, cap):
    e = pl.program_id(0)
    f = pl.program_id(2)
    x = x_ref[0, :, 0:D_MODEL]
    a = jnp.dot(x, w1_ref[0, 0], preferred_element_type=jnp.float32)
    b = jnp.dot(x, w3_ref[0, 0], preferred_element_type=jnp.float32)
    hid = (a * _sigmoid(a) * b).astype(jnp.bfloat16)
    part = jnp.dot(hid, w2_ref[0, 0], preferred_element_type=jnp.float32)

    @pl.when(f == 0)
    def _():
        acc_ref[...] = part

    @pl.when(f > 0)
    def _():
        acc_ref[...] = acc_ref[...] + part

    @pl.when(f == pl.num_programs(2) - 1)
    def _():
        rows = o_ref.shape[1]
        gl = x_ref[0, :, D_MODEL:XE_COLS].astype(jnp.float32)
        lane = lax.broadcasted_iota(jnp.int32, gl.shape, 1)
        mine = ((lane & (N_EXPERTS - 1)) == e) & (lane < 3 * N_EXPERTS)
        gate = jnp.sum(jnp.where(mine, gl, 0.0), axis=-1, keepdims=True)
        slot = pl.program_id(1) * rows + lax.broadcasted_iota(jnp.int32, (rows, 1), 0)
        o_ref[0] = jnp.where(slot < cap, acc_ref[...] * gate, 0.0).astype(o_ref.dtype)


def expert_ffn(xe, w1, w3, w2, layer, cap):
    e, cap_pad, _ = xe.shape
    return pl.pallas_call(
        functools.partial(_ffn_kernel, cap=cap),
        grid=(e, cap_pad // FFN_ROWS, EXPERT_FF // FFN_COLS),
        in_specs=[pl.BlockSpec((1, FFN_ROWS, XE_COLS), lambda i, m, f: (i, m, 0)),
                  pl.BlockSpec((1, 1, D_MODEL, FFN_COLS), lambda i, m, f: (layer, i, 0, f)),
                  pl.BlockSpec((1, 1, D_MODEL, FFN_COLS), lambda i, m, f: (layer, i, 0, f)),
                  pl.BlockSpec((1, 1, FFN_COLS, D_MODEL), lambda i, m, f: (layer, i, f, 0))],
        out_specs=pl.BlockSpec((1, FFN_ROWS, D_MODEL), lambda i, m, f: (i, m, 0)),
        out_shape=jax.ShapeDtypeStruct((e, cap_pad, D_MODEL), jnp.bfloat16),
        scratch_shapes=[pltpu.VMEM((FFN_ROWS, D_MODEL), jnp.float32)],
        compiler_params=_cparams("parallel", "parallel", "arbitrary"),
        name="expert_ffn",
    )(xe, w1, w3, w2)


def _combine_kernel(st_ref, ye_ref, pos_ref, h_ref, g_ref, b_ref, *rest, lp, seq_len, final):
    if final:
        out_ref, sy_ref, acc_ref, n_ref, sem_ref, ybuf_ref, osem_ref = rest
    else:
        ho_ref, hb_ref, sy_ref, acc_ref, n_ref, sem_ref = rest
    ti = pl.program_id(0)
    nt = pl.num_programs(0)
    t = TOK_TILE

    def chunk_copy(half, e, s16, off, j):
        src = pl.multiple_of(s16 + j * DMA_ROWS, BF16_ROWS)
        dst = pl.multiple_of(off + j * DMA_ROWS, DMA_ROWS)
        return pltpu.make_async_copy(ye_ref.at[e, pl.ds(src, DMA_ROWS)], sy_ref.at[half, pl.ds(dst, DMA_ROWS)],
                                     sem_ref.at[half])

    def fetch(half, tile):
        plan, _ = _tile_plan(st_ref, tile, half)
        total = jnp.int32(0)
        for e, s16, nch, off, end in plan:
            def issue(j, c, e=e, s16=s16, off=off):
                chunk_copy(half, e, s16, off, j).start()
                return c
            lax.fori_loop(0, nch, issue, 0)
            total = total + nch
        n_ref[half] = total

    @pl.when(ti == 0)
    def _():
        sy_ref[...] = jnp.zeros(sy_ref.shape, sy_ref.dtype)
        fetch(0, 0)
        fetch(1, 0)

    acc_ref[...] = jnp.zeros(acc_ref.shape, jnp.float32)
    for half in range(2):
        plan, k_tot = _tile_plan(st_ref, ti, half)
        srow = _stage_rows(pos_ref[0, half * HALF_EXPERTS:(half + 1) * HALF_EXPERTS, :], plan)

        def wait_one(_, c, half=half):
            chunk_copy(half, 0, 0, 0, 0).wait()
            return c
        lax.fori_loop(0, n_ref[half], wait_one, 0)

        def gather(kc, c, half=half, srow=srow):
            k0 = pl.multiple_of(kc * SEQ_ALIGN, SEQ_ALIGN)
            acc_ref[...] = acc_ref[...] + lax.dot_general(
                _onehot_rows(srow, k0), sy_ref[half, pl.ds(k0, SEQ_ALIGN), :], (((0,), (0,)), ((), ())),
                preferred_element_type=jnp.float32)
            return c

        lax.fori_loop(0, _n_blocks(k_tot), gather, 0)

        @pl.when(ti + 1 < nt)
        def _(half=half):
            fetch(half, ti + 1)

    tsum = ALPHA * h_ref[...] + acc_ref[...]
    if not final:
        _norm_route(tsum, g_ref, b_ref, None, ti * t, lp, seq_len, ho_ref, hb_ref, None, None)
        return

    nts = lp // t

    def out_rows(tile):
        tok0 = lax.rem(tile, nts) * t
        first = jnp.where(tok0 == 0, N_META, 0)
        return first, jnp.clip(seq_len - tok0, 0, t) - first

    def out_copy(slot, bi, src, dst):
        return pltpu.make_async_copy(ybuf_ref.at[slot, pl.ds(src, OUT_ROWS)], out_ref.at[bi, pl.ds(dst, OUT_ROWS)],
                                     osem_ref.at[slot])

    def wait_out(slot, tile):
        def body(_, c):
            out_copy(slot, 0, 0, 0).wait()
            return c
        lax.fori_loop(0, out_rows(tile)[1] // OUT_ROWS, body, 0)

    slot = ti & 1

    @pl.when(ti >= 2)
    def _():
        wait_out(slot, ti - 2)

    y, _ = _layer_norm_rows(tsum, g_ref, b_ref, ti * t, lp, seq_len)
    ybuf_ref[slot] = y
    first, rows = out_rows(ti)
    bi = ti // nts
    tok0 = lax.rem(ti, nts) * t

    def send(j, c):
        src = pl.multiple_of(first + j * OUT_ROWS, OUT_ROWS)
        out_copy(slot, bi, src, pl.multiple_of(tok0 - N_META + src, OUT_ROWS)).start()
        return c
    lax.fori_loop(0, rows // OUT_ROWS, send, 0)

    @pl.when(ti == nt - 1)
    def _():
        wait_out(slot, ti)

        @pl.when(nt > 1)
        def _():
            wait_out(1 - slot, ti - 1)


def combine(starts, ye, pos, h, g, bb, lp, seq_len, final):
    n = h.shape[0]
    nt = n // TOK_TILE
    row = lambda i, st: (i, 0)
    fix = lambda i, st: (0, 0)
    scratch = [pltpu.VMEM((2, HALF_ROWS, D_MODEL), jnp.bfloat16), pltpu.VMEM((TOK_TILE, D_MODEL), jnp.float32),
               pltpu.SMEM((2,), jnp.int32), pltpu.SemaphoreType.DMA((2,))]
    if final:
        assert seq_len % OUT_ROWS == 0 and N_META % OUT_ROWS == 0
        out_specs = pl.BlockSpec(memory_space=pl.ANY)
        out_shape = jax.ShapeDtypeStruct((n // lp, seq_len - N_META, D_MODEL), jnp.float32)
        scratch += [pltpu.VMEM((2, TOK_TILE, D_MODEL), jnp.float32), pltpu.SemaphoreType.DMA((2,))]
    else:
        out_specs = [pl.BlockSpec((TOK_TILE, D_MODEL), row), pl.BlockSpec((TOK_TILE, D_MODEL), row)]
        out_shape = [jax.ShapeDtypeStruct((n, D_MODEL), jnp.float32), jax.ShapeDtypeStruct((n, D_MODEL), jnp.bfloat16)]
    grid_spec = pltpu.PrefetchScalarGridSpec(
        num_scalar_prefetch=1,
        grid=(nt,),
        in_specs=[pl.BlockSpec(memory_space=pl.ANY), pl.BlockSpec((1, N_EXPERTS, TOK_TILE), lambda i, st: (i, 0, 0)),
                  pl.BlockSpec((TOK_TILE, D_MODEL), row), pl.BlockSpec((1, D_MODEL), fix), pl.BlockSpec((1, D_MODEL), fix)],
        out_specs=out_specs,
        scratch_shapes=scratch,
    )
    return pl.pallas_call(
        functools.partial(_combine_kernel, lp=lp, seq_len=seq_len, final=final),
        grid_spec=grid_spec,
        out_shape=out_shape,
        compiler_params=_cparams("arbitrary"),
        name="moe_combine_out" if final else "moe_combine",
    )(starts, ye, pos, h, g, bb)


def moe_block(h, hb, aff, afft, w1, w3, w2, layer, g, bb, n_real, lp, seq_len, final=False):
    cap = max(1, CAPACITY_FACTOR * n_real // N_EXPERTS)
    cap_pad = -(-(cap + DMA_ROWS) // FFN_ROWS) * FFN_ROWS
    pos, start3 = route(afft, cap)
    starts = start3[:, 0, :N_EXPERTS].reshape(-1)
    xe = dispatch(starts, hb, pos, aff, cap, cap_pad)
    ye = expert_ffn(xe, w1, w3, w2, layer, cap)
    return combine(starts, ye, pos, h, g, bb, lp, seq_len, final)


def _prep_weights(meta, w_in, conv_w, conv_b, a_log_f, a_log_b, dt_bias_f, dt_bias_b, d_skip, ssd_norm_w,
                  lam_q1, lam_k1, lam_q2, lam_k2, subln_w, rel_bias, w_out, pool_w, pool_b, pool_scale,
                  ln1_g, ln1_b, ln2_g, ln2_b, w_router, w1, w3, w2):
    bf = jnp.bfloat16
    f32 = jnp.float32
    w = w_in[0]
    o = [0, SSD_INNER, SSD_INNER + SSD_CONV_CH, SSD_INNER + SSD_CONV_CH + 2 * SSD_HEADS]
    wz, wxbc, wdt = w[:, o[0]:o[1]], w[:, o[1]:o[2]], w[:, o[2]:o[3]]
    wq, wk, wv = (w[:, o[3] + i * ATT_QK:o[3] + (i + 1) * ATT_QK] for i in range(3))
    pad = jnp.zeros((D_MODEL, _PROJ_COLS - _COL_DT - 2 * SSD_HEADS), f32)
    w_tok = jnp.concatenate([wz, wxbc, wdt, pad], axis=1).astype(bf)
    w_qv = jnp.concatenate([wq * (ATT_HEAD_DIM ** -0.5 * LOG2E), wv], axis=1).T.astype(bf)
    lane_pad = lambda v: jnp.pad(v.astype(f32), (0, LANES - v.shape[0]))
    par = jnp.stack([lane_pad(jnp.concatenate([dt_bias_f[0], dt_bias_b[0]])),
                     lane_pad(jnp.concatenate([a_log_f[0], a_log_b[0]]))] + [jnp.zeros((LANES,), f32)] * 6)
    lam4 = jnp.stack([lane_pad(lam_q1[0]), lane_pad(lam_k1[0]), lane_pad(lam_q2[0]), lane_pad(lam_k2[0])]
                     + [jnp.zeros((LANES,), f32)] * 4)
    return dict(
        meta=meta, w_tok=w_tok, w_k=wk.astype(bf), w_qv=w_qv, conv_w=conv_w[0, :, 0, :], conv_b=conv_b[0][None], par=par,
        dsk=jnp.repeat(d_skip[0], SSD_HEAD_DIM)[None], nw=ssd_norm_w[0][None], lam4=lam4,
        subw=jnp.broadcast_to(subln_w[0][:, None], (ATT_V_DIM, SEQ_ALIGN)), rel_bias=rel_bias,
        w_out=w_out[0].astype(bf), pool_w=pool_w[0].astype(bf), pool_b=pool_b[0][None], pool_scale=pool_scale[0][None],
        ln1_g=ln1_g, ln1_b=ln1_b, ln2_g=ln2_g, ln2_b=ln2_b,
        w_r=jnp.pad(w_router, ((0, 0), (0, 0), (0, LANES - N_EXPERTS))).astype(bf),
        w1=w1.astype(bf), w3=w3.astype(bf), w2=w2.astype(bf))


def _trunk(x, p, bias5):
    b, s, _ = x.shape
    seq_len = s + N_META
    lp = -(-seq_len // SEQ_ALIGN) * SEQ_ALIGN
    n = b * lp
    n_real = b * seq_len
    meta = jnp.broadcast_to(p["meta"][None].astype(x.dtype), (b, N_META, D_MODEL))
    h = jnp.concatenate([meta, x, jnp.zeros((b, lp - seq_len, D_MODEL), x.dtype)], axis=1).reshape(n, D_MODEL)
    hb = h.astype(jnp.bfloat16)
    lambda_init = 0.8 - 0.6 * math.exp(-0.3 * 0)
    tm = next(c for c in (ROW_TILE, ROW_TILE // 2, SEQ_ALIGN) if n % c == 0)

    proj = mm_nn(hb, p["w_tok"], jnp.float32, tm, 512)
    k3 = mm_nn(hb, p["w_k"], jnp.bfloat16, tm, 512).reshape(b, lp, ATT_QK)
    qv = mm_nt_blocked(p["w_qv"], hb, jnp.bfloat16, tm, 512)
    proj3 = proj.reshape(b, lp, _PROJ_COLS)
    xbc = conv_silu(proj3, p["conv_w"], p["conv_b"]).reshape(n, SSD_CONV_CH)
    yf = ssd_scan(xbc, proj, p["par"], b, lp, seq_len, False)
    y_ssd = ssd_scan(xbc, proj, p["par"], b, lp, seq_len, True, gate=(yf, p["dsk"], p["nw"]))
    y_att = diff_attention(qv, k3, bias5, p["lam4"], p["subw"], b, lp, seq_len, lambda_init)
    h, hb, aff, afft = out_proj_norm_route(y_ssd, y_att, p["w_out"], h, p["ln1_g"][0][None], p["ln1_b"][0][None],
                                           p["w_r"][0], lp, seq_len)
    h, hb = moe_block(h, hb, aff, afft, p["w1"], p["w3"], p["w2"], 0, p["ln2_g"][0][None], p["ln2_b"][0][None],
                      n_real, lp, seq_len)
    d = pool_diff(h.reshape(b, lp, D_MODEL), seq_len).reshape(n, D_MODEL)
    h, hb, aff, afft = pool_proj_norm_route(d, p["pool_w"], p["pool_b"], p["pool_scale"], h, p["ln1_g"][1][None],
                                            p["ln1_b"][1][None], p["w_r"][1], lp, seq_len)
    return moe_block(h, hb, aff, afft, p["w1"], p["w3"], p["w2"], 1, p["ln2_g"][1][None], p["ln2_b"][1][None],
                     n_real, lp, seq_len, final=True)


def kernel(x_prompt, x_sample, meta, w_in, conv_w, conv_b, a_log_f, a_log_b, dt_bias_f, dt_bias_b, d_skip, ssd_norm_w, lam_q1, lam_k1, lam_q2, lam_k2, subln_w, rel_bias, w_out, pool_w, pool_b, pool_scale, ln1_g, ln1_b, ln2_g, ln2_b, w_router, w1, w3, w2):
    p = _prep_weights(meta, w_in, conv_w, conv_b, a_log_f, a_log_b, dt_bias_f, dt_bias_b, d_skip, ssd_norm_w,
                      lam_q1, lam_k1, lam_q2, lam_k2, subln_w, rel_bias, w_out, pool_w, pool_b, pool_scale,
                      ln1_g, ln1_b, ln2_g, ln2_b, w_router, w1, w3, w2)
    bias5 = bias_tiles(rel_bias)
    return (_trunk(x_prompt, p, bias5), _trunk(x_sample, p, bias5))
```

```python
import functools
import math

import jax
import jax.numpy as jnp
from jax import lax
from jax.experimental import pallas as pl
from jax.experimental.pallas import tpu as pltpu

D_MODEL = 2048
DEPTH = 2
N_META = 16
SSD_HEADS = 16
SSD_HEAD_DIM = 64
SSD_INNER = SSD_HEADS * SSD_HEAD_DIM
SSD_GROUPS = 4
SSD_STATE = 128
SSD_CONV = 5
SSD_CHUNK = 128
SSD_CONV_CH = SSD_INNER + 2 * SSD_GROUPS * SSD_STATE
ATT_HEADS = 8
ATT_HEAD_DIM = 64
ATT_V_DIM = 2 * ATT_HEAD_DIM
ATT_QK = ATT_HEADS * 2 * ATT_HEAD_DIM
ATT_INNER = ATT_HEADS * ATT_V_DIM
REL_BUCKETS = 32
POOL_WINDOWS = (2, 4, 8, 16)
POOL_GROUP = D_MODEL // len(POOL_WINDOWS)
N_EXPERTS = 16
EXPERT_FF = 2048
CAPACITY_FACTOR = 2
ALPHA = (2 * DEPTH) ** 0.25
LN_EPS = 1e-5

LANES = 128
BF16_ROWS = 16
SEQ_ALIGN = 256
TOK_TILE = 256
ROW_TILE = 1024
DMA_ROWS = 32
SLAB = 256
OUT_ROWS = 16
FFN_ROWS = 704
FFN_COLS = 512
VMEM_LIMIT = 56 * 1024 * 1024

_BUCKET_STARTS = (1, 2, 3, 4, 5, 6, 7, 8, 12, 16, 23, 32, 46, 64, 91)
_COL_Z, _COL_XBC, _COL_DT, _PROJ_COLS = 0, 1024, 3072, 3584
ATT_CHUNK = 4
ATT_HEADS_PER_STEP = 2
LOG2E = math.log2(math.e)


def _cparams(*sem):
    return pltpu.CompilerParams(dimension_semantics=sem, vmem_limit_bytes=VMEM_LIMIT)


def _sigmoid(x):
    return 1.0 / (1.0 + jnp.exp(-x))


def _softplus(x):
    return jnp.maximum(x, 0.0) + jnp.log(1.0 + jnp.exp(-jnp.abs(x)))


def _split3(x):
    p1 = x.astype(jnp.bfloat16)
    r1 = x - p1.astype(jnp.float32)
    p2 = r1.astype(jnp.bfloat16)
    p3 = (r1 - p2.astype(jnp.float32)).astype(jnp.bfloat16)
    return p1, p2, p3


def _mm_nn_kernel(x_ref, w_ref, o_ref):
    o_ref[...] = jnp.dot(x_ref[...], w_ref[...], preferred_element_type=jnp.float32).astype(o_ref.dtype)


def mm_nn(x, w, out_dtype, tm, tn):
    n, k = x.shape
    nn = w.shape[1]
    return pl.pallas_call(
        _mm_nn_kernel,
        grid=(n // tm, nn // tn),
        in_specs=[pl.BlockSpec((tm, k), lambda i, j: (i, 0)), pl.BlockSpec((k, tn), lambda i, j: (0, j))],
        out_specs=pl.BlockSpec((tm, tn), lambda i, j: (i, j)),
        out_shape=jax.ShapeDtypeStruct((n, nn), out_dtype),
        compiler_params=_cparams("parallel", "arbitrary"),
        name="mm_nn",
    )(x, w)


def _mm_nt_kernel(w_ref, x_ref, o_ref):
    r = lax.dot_general(w_ref[...], x_ref[...], (((1,), (1,)), ((), ())), preferred_element_type=jnp.float32)
    for s in range(o_ref.shape[0]):
        o_ref[s] = r[:, s * SEQ_ALIGN:(s + 1) * SEQ_ALIGN].astype(o_ref.dtype)


def mm_nt_blocked(wt, x, out_dtype, tm, tn):
    n, k = x.shape
    nn = wt.shape[0]
    per = tm // SEQ_ALIGN
    return pl.pallas_call(
        _mm_nt_kernel,
        grid=(n // tm, nn // tn),
        in_specs=[pl.BlockSpec((tn, k), lambda i, j: (j, 0)), pl.BlockSpec((tm, k), lambda i, j: (i, 0))],
        out_specs=pl.BlockSpec((per, tn, SEQ_ALIGN), lambda i, j: (i, j, 0)),
        out_shape=jax.ShapeDtypeStruct((n // SEQ_ALIGN, nn, SEQ_ALIGN), out_dtype),
        compiler_params=_cparams("parallel", "arbitrary"),
        name="mm_nt",
    )(wt, x)


def _conv_kernel(x_ref, w_ref, b_ref, o_ref, pad_ref, *, rows):
    lp = x_ref.shape[1]
    halo = 8
    pad_ref[0:halo, :] = jnp.zeros((halo, SLAB), jnp.float32)
    pad_ref[lp + halo:lp + 2 * halo, :] = jnp.zeros((halo, SLAB), jnp.float32)
    pad_ref[halo:lp + halo, :] = x_ref[0]
    for t0 in range(0, lp, rows):
        acc = jnp.broadcast_to(b_ref[...], (rows, SLAB))
        for k in range(SSD_CONV):
            off = t0 + halo - SSD_CONV // 2 + k
            acc = acc + pad_ref[off:off + rows, :] * w_ref[k:k + 1, :]
        o_ref[0, t0:t0 + rows, :] = acc * _sigmoid(acc)


def conv_silu(proj3, conv_w, conv_b):
    b, lp, _ = proj3.shape
    c0 = _COL_XBC // SLAB
    return pl.pallas_call(
        functools.partial(_conv_kernel, rows=SEQ_ALIGN),
        grid=(b, SSD_CONV_CH // SLAB),
        in_specs=[
            pl.BlockSpec((1, lp, SLAB), lambda i, j: (i, 0, c0 + j)),
            pl.BlockSpec((SSD_CONV, SLAB), lambda i, j: (0, j)),
            pl.BlockSpec((1, SLAB), lambda i, j: (0, j)),
        ],
        out_specs=pl.BlockSpec((1, lp, SLAB), lambda i, j: (i, 0, j)),
        out_shape=jax.ShapeDtypeStruct((b, lp, SSD_CONV_CH), jnp.float32),
        scratch_shapes=[pltpu.VMEM((lp + 16, SLAB), jnp.float32)],
        compiler_params=_cparams("parallel", "parallel"),
        name="conv_silu",
    )(proj3, conv_w, conv_b)


def _ssd_kernel(x_ref, b_ref, c_ref, dt_ref, par_ref, *rest, seq_len, reverse, gated):
    if gated:
        yo_ref, z_ref, dsk_ref, nw_ref, y_ref, h_ref = rest
    else:
        y_ref, h_ref = rest
    cs = SSD_CHUNK
    step = pl.program_id(1)
    chunk = (pl.num_programs(1) - 1 - step) if reverse else step
    lane0 = SSD_HEADS if reverse else 0

    @pl.when(step == 0)
    def _():
        h_ref[...] = jnp.zeros(h_ref.shape, jnp.float32)

    sub = lax.broadcasted_iota(jnp.int32, (cs, cs), 0)
    lan = lax.broadcasted_iota(jnp.int32, (cs, cs), 1)
    valid = (chunk * cs + sub[:, 0:1]) < seq_len
    dt = jnp.where(valid, _softplus(dt_ref[...] + par_ref[0:1, :]), 0.0)
    a = dt * -jnp.exp(par_ref[1:2, :])
    a_t = a.T[0:2 * SSD_HEADS, :]
    if reverse:
        incl, rest = (sub >= lan), (sub < lan)
    else:
        incl, rest = (sub <= lan), (sub > lan)
    tri = jnp.concatenate([incl.astype(jnp.bfloat16), rest.astype(jnp.bfloat16),
                           jnp.ones((cs, cs), jnp.bfloat16)], axis=1)
    pieces = jnp.concatenate(_split3(a_t), axis=0)
    sums = jnp.dot(pieces, tri, preferred_element_type=jnp.float32)
    nh2 = 2 * SSD_HEADS
    sums = sums[0:nh2] + sums[nh2:2 * nh2] + sums[2 * nh2:3 * nh2]
    cum_row, rest_row, tot_row = sums[:, 0:cs], sums[:, cs:2 * cs], sums[:, 2 * cs:3 * cs]
    cum_col = jnp.concatenate([cum_row, jnp.zeros((cs - nh2, cs), jnp.float32)], axis=0).T
    mask = (sub <= lan) if reverse else (sub >= lan)
    left = lan[0:1, :] < SSD_HEAD_DIM
    gated_pairs = []

    for g in range(SSD_GROUPS):
        bg = b_ref[:, g * SSD_STATE:(g + 1) * SSD_STATE]
        cg = c_ref[:, g * SSD_STATE:(g + 1) * SSD_STATE]
        cb = lax.dot_general(cg.astype(jnp.bfloat16), bg.astype(jnp.bfloat16), (((1,), (1,)), ((), ())),
                             preferred_element_type=jnp.float32)
        bg_t = bg.T
        for pp in range(2):
            p = 2 * g + pp
            r0, r1 = lane0 + 2 * p, lane0 + 2 * p + 1
            dtp = jnp.where(left, dt[:, r0:r0 + 1], dt[:, r1:r1 + 1])
            xs = (x_ref[:, p * LANES:(p + 1) * LANES] * dtp).astype(jnp.bfloat16)
            h_t = h_ref[p]
            rhs = jnp.concatenate([xs, h_t.astype(jnp.bfloat16)], axis=0)
            ys, ss = [], []
            for r in (r0, r1):
                cc = cum_col[:, r:r + 1]
                dec = jnp.exp(jnp.where(mask, cc - cum_row[r:r + 1, :], -jnp.inf))
                lhs = jnp.concatenate([(cb * dec).astype(jnp.bfloat16),
                                       (cg * jnp.exp(cc)).astype(jnp.bfloat16)], axis=1)
                ys.append(jnp.dot(lhs, rhs, preferred_element_type=jnp.float32))
                bw = (bg_t * jnp.exp(rest_row[r:r + 1, :])).astype(jnp.bfloat16)
                ss.append(jnp.dot(bw, xs, preferred_element_type=jnp.float32))
            lanes = slice(p * LANES, (p + 1) * LANES)
            y_pair = jnp.where(left, ys[0], ys[1])
            if gated:
                z = z_ref[:, lanes]
                gated_pairs.append((yo_ref[:, lanes] + y_pair + dsk_ref[:, lanes] * x_ref[:, lanes]) * (z * _sigmoid(z)))
            else:
                y_ref[:, lanes] = y_pair
            decay = jnp.where(left, jnp.exp(tot_row[r0:r0 + 1, :]), jnp.exp(tot_row[r1:r1 + 1, :]))
            h_ref[p] = decay * h_t + jnp.where(left, ss[0], ss[1])
        if gated:
            seg = jnp.concatenate(gated_pairs, axis=1)
            gated_pairs = []
            ms = jnp.mean(seg * seg, axis=-1, keepdims=True)
            cols = slice(2 * g * LANES, 2 * (g + 1) * LANES)
            y_ref[:, cols] = (seg * lax.rsqrt(ms + LN_EPS) * nw_ref[:, cols]).astype(y_ref.dtype)


def ssd_scan(xbc, proj, par, b, lp, seq_len, reverse, gate=None):
    n = xbc.shape[0]
    nc = lp // SSD_CHUNK
    cdt = _COL_DT // LANES

    def row(i, c):
        return i * nc + ((nc - 1 - c) if reverse else c)

    tok = lambda col: (lambda i, c: (row(i, c), col))
    fix = lambda i, c: (0, 0)
    in_specs = [
        pl.BlockSpec((SSD_CHUNK, SSD_INNER), tok(0)),
        pl.BlockSpec((SSD_CHUNK, SSD_GROUPS * SSD_STATE), tok(2)),
        pl.BlockSpec((SSD_CHUNK, SSD_GROUPS * SSD_STATE), tok(3)),
        pl.BlockSpec((SSD_CHUNK, LANES), tok(cdt)),
        pl.BlockSpec((8, LANES), fix),
    ]
    args = [xbc, xbc, xbc, proj, par]
    if gate is not None:
        y_other, dsk, nw = gate
        in_specs += [pl.BlockSpec((SSD_CHUNK, SSD_INNER), tok(0)), pl.BlockSpec((SSD_CHUNK, SSD_INNER), tok(0)),
                     pl.BlockSpec((1, SSD_INNER), fix), pl.BlockSpec((1, SSD_INNER), fix)]
        args += [y_other, proj, dsk, nw]
    return pl.pallas_call(
        functools.partial(_ssd_kernel, seq_len=seq_len, reverse=reverse, gated=gate is not None),
        grid=(b, nc),
        in_specs=in_specs,
        out_specs=pl.BlockSpec((SSD_CHUNK, SSD_INNER), tok(0)),
        out_shape=jax.ShapeDtypeStruct((n, SSD_INNER), jnp.float32 if gate is None else jnp.bfloat16),
        scratch_shapes=[pltpu.VMEM((SSD_HEADS // 2, SSD_STATE, LANES), jnp.float32)],
        compiler_params=_cparams("parallel", "arbitrary"),
        name="ssd_bwd" if reverse else "ssd_fwd",
    )(*args)


def _bias_kernel(rb_ref, o_ref):
    h = pl.program_id(0)
    d = pl.program_id(1) - 2
    t = SEQ_ALIGN
    kk = lax.broadcasted_iota(jnp.int32, (t, t), 0)
    qq = lax.broadcasted_iota(jnp.int32, (t, t), 1)
    rel = d * t + kk - qq
    n = jnp.abs(rel)
    half = REL_BUCKETS // 2
    vneg = jnp.full((t, t), rb_ref[0, h], jnp.float32)
    vpos = jnp.full((t, t), rb_ref[half, h], jnp.float32)
    for j, start in enumerate(_BUCKET_STARTS, start=1):
        ge = n >= start
        vneg = jnp.where(ge, rb_ref[j, h], vneg)
        vpos = jnp.where(ge, rb_ref[half + j, h], vpos)
    o_ref[0, 0] = jnp.where(rel > 0, vpos, vneg) * LOG2E


def bias_tiles(rel_bias):
    return pl.pallas_call(
        _bias_kernel,
        grid=(ATT_HEADS, 5),
        in_specs=[pl.BlockSpec(memory_space=pltpu.SMEM)],
        out_specs=pl.BlockSpec((1, 1, SEQ_ALIGN, SEQ_ALIGN), lambda h, d: (h, d, 0, 0)),
        out_shape=jax.ShapeDtypeStruct((ATT_HEADS, 5, SEQ_ALIGN, SEQ_ALIGN), jnp.float32),
        compiler_params=_cparams("parallel", "parallel"),
        name="bias_tiles",
    )(rel_bias)


def _attn_kernel(q_ref, v_ref, k_ref, bias_ref, lam_ref, sub_ref, o_ref, s_ref, *, seq_len, lambda_init):
    t = SEQ_ALIGN
    group = ATT_CHUNK
    vd = ATT_V_DIM
    qb = pl.program_id(2)
    sub = lax.broadcasted_iota(jnp.int32, (vd, t), 0)
    qs = []
    for hh in range(ATT_HEADS_PER_STEP):
        q_t = q_ref[0, hh * vd:(hh + 1) * vd, :]
        zero = jnp.zeros_like(q_t)
        qs += [jnp.where(sub < ATT_HEAD_DIM, q_t, zero), jnp.where(sub >= ATT_HEAD_DIM, q_t, zero)]
    n_maps = len(qs)
    n_full = seq_len // t
    tail = seq_len - n_full * t
    tail_rows = -(-tail // LANES) * LANES

    def fold_max(x):
        return jnp.max(x.reshape(x.shape[0] // 8, 8, t), axis=0)

    def score_block(kb, rows, mx):
        r0 = kb * t
        if not isinstance(kb, int):
            r0 = pl.multiple_of(r0, t)
        tile = jnp.clip(kb - qb, -2, 2) + 2
        out = []
        for mi in range(n_maps):
            hh = mi // 2
            kblk = k_ref[0, pl.ds(r0, rows), hh * vd:(hh + 1) * vd]
            s = jnp.dot(kblk, qs[mi], preferred_element_type=jnp.float32) + bias_ref[hh, tile, 0:rows, :]
            if rows != t:
                s = jnp.where(lax.broadcasted_iota(jnp.int32, (rows, 1), 0) < tail, s, -jnp.inf)
            s_ref[mi, pl.ds(r0, rows), :] = s
            out.append(jnp.maximum(mx[mi], fold_max(s)))
        return tuple(out)

    neg = jnp.full((8, t), -jnp.inf, jnp.float32)

    mx = (neg,) * n_maps
    for kb in range(n_full):
        mx = score_block(kb, t, mx)
    if tail:
        mx = score_block(n_full, tail_rows, mx)
    ms_ = [jnp.max(m, axis=0, keepdims=True) for m in mx]

    def prob_blocks(kbs, rows, accs):
        accs = list(accs)
        ones = jnp.ones((BF16_ROWS, rows * len(kbs)), jnp.bfloat16)
        for hh in range(ATT_HEADS_PER_STEP):
            vcat = jnp.concatenate([v_ref[kb][hh * vd:(hh + 1) * vd, 0:rows] for kb in kbs] , axis=1)
            vaug = jnp.concatenate([vcat, ones], axis=0)
            for mi in (2 * hh, 2 * hh + 1):
                ps = []
                for kb in kbs:
                    r0 = kb * t
                    if not isinstance(kb, int):
                        r0 = pl.multiple_of(r0, t)
                    ps.append(jnp.exp2((s_ref[mi, pl.ds(r0, rows), :] - ms_[mi]).astype(jnp.bfloat16)))
                pcat = jnp.concatenate(ps, axis=0) if len(ps) > 1 else ps[0]
                accs[mi] = accs[mi] + jnp.dot(vaug, pcat, preferred_element_type=jnp.float32)
        return tuple(accs)

    zacc = jnp.zeros((vd + BF16_ROWS, t), jnp.float32)
    accs = (zacc,) * n_maps
    for k0 in range(0, n_full, group):
        accs = prob_blocks(list(range(k0, min(k0 + group, n_full))), t, accs)
    if tail:
        accs = prob_blocks([n_full], tail_rows, accs)
    lv = lam_ref[...]
    lam = (jnp.exp(jnp.sum(lv[0:1] * lv[1:2], axis=-1, keepdims=True))
           - jnp.exp(jnp.sum(lv[2:3] * lv[3:4], axis=-1, keepdims=True)) + lambda_init)
    for hh in range(ATT_HEADS_PER_STEP):
        a1, a2 = accs[2 * hh], accs[2 * hh + 1]
        o = a1[0:vd] * (1.0 / a1[vd:vd + 1]) - lam * (a2[0:vd] * (1.0 / a2[vd:vd + 1]))
        ms = jnp.mean(o * o, axis=0, keepdims=True)
        o = o * lax.rsqrt(ms + LN_EPS) * sub_ref[...] * (1.0 - lambda_init)
        o_ref[:, hh * vd:(hh + 1) * vd] = o.T.astype(o_ref.dtype)


def diff_attention(qv, k3, bias5, lam4, subw, b, lp, seq_len, lambda_init):
    nq = lp // SEQ_ALIGN
    hs = ATT_HEADS_PER_STEP
    return pl.pallas_call(
        functools.partial(_attn_kernel, seq_len=seq_len, lambda_init=lambda_init),
        grid=(b, ATT_HEADS // hs, nq),
        in_specs=[
            pl.BlockSpec((1, hs * ATT_V_DIM, SEQ_ALIGN), lambda i, h, q: (i * nq + q, h, 0)),
            pl.BlockSpec((nq, hs * ATT_V_DIM, SEQ_ALIGN), lambda i, h, q: (i, ATT_HEADS // hs + h, 0)),
            pl.BlockSpec((1, lp, hs * ATT_V_DIM), lambda i, h, q: (i, 0, h)),
            pl.BlockSpec((hs, 5, SEQ_ALIGN, SEQ_ALIGN), lambda i, h, q: (h, 0, 0, 0)),
            pl.BlockSpec((8, LANES), lambda i, h, q: (0, 0)),
            pl.BlockSpec((ATT_V_DIM, SEQ_ALIGN), lambda i, h, q: (0, 0)),
        ],
        out_specs=pl.BlockSpec((SEQ_ALIGN, hs * ATT_V_DIM), lambda i, h, q: (i * nq + q, h)),
        out_shape=jax.ShapeDtypeStruct((b * lp, ATT_INNER), jnp.bfloat16),
        scratch_shapes=[pltpu.VMEM((2 * hs, lp, SEQ_ALIGN), jnp.float32)],
        compiler_params=_cparams("parallel", "parallel", "arbitrary"),
        name="diff_attn",
    )(qv, qv, k3, bias5, lam4, subw)


def _layer_norm_rows(t, g_ref, b_ref, tile_row0, lp, seq_len):
    rows = t.shape[0]
    mu = jnp.mean(t, axis=-1, keepdims=True)
    tc = t - mu
    var = jnp.mean(tc * tc, axis=-1, keepdims=True)
    y = tc * lax.rsqrt(var + LN_EPS) * g_ref[...] + b_ref[...]
    pos = tile_row0 % lp + lax.broadcasted_iota(jnp.int32, (rows, 1), 0)
    valid = pos < seq_len
    return jnp.where(valid, y, 0.0), valid


def _norm_route(t, g_ref, b_ref, wr_ref, tile_row0, lp, seq_len, h_ref, hb_ref, aff_ref, afft_ref):
    y, valid = _layer_norm_rows(t, g_ref, b_ref, tile_row0, lp, seq_len)
    h_ref[...] = y
    yb = y.astype(jnp.bfloat16)
    hb_ref[...] = yb
    if aff_ref is not None:
        logits = jnp.dot(yb, wr_ref[...], preferred_element_type=jnp.float32)
        lane = lax.broadcasted_iota(jnp.int32, logits.shape, 1)
        is_e = lane < N_EXPERTS
        logits = jnp.where(is_e, logits, -jnp.inf)
        e = jnp.exp(logits - jnp.max(logits, axis=-1, keepdims=True))
        aff = e / jnp.sum(e, axis=-1, keepdims=True)
        aff = jnp.where(valid, aff, jnp.where(is_e, -1.0, 0.0))
        aff_ref[...] = aff
        afft_ref[0] = aff.T[0:N_EXPERTS, :]


def _outproj_kernel(ys_ref, ya_ref, w_ref, h_ref, g_ref, b_ref, wr_ref, ho_ref, hb_ref, aff_ref, afft_ref, *, lp, seq_len):
    mix = (jnp.dot(ys_ref[...], w_ref[0:SSD_INNER, :], preferred_element_type=jnp.float32)
           + jnp.dot(ya_ref[...], w_ref[SSD_INNER:, :], preferred_element_type=jnp.float32))
    t = ALPHA * h_ref[...] + mix
    _norm_route(t, g_ref, b_ref, wr_ref, pl.program_id(0) * h_ref.shape[0], lp, seq_len, ho_ref, hb_ref, aff_ref, afft_ref)


def _poolproj_kernel(d_ref, w_ref, pb_ref, ps_ref, h_ref, g_ref, b_ref, wr_ref, ho_ref, hb_ref, aff_ref, afft_ref, *, lp, seq_len):
    outs = []
    for gi in range(len(POOL_WINDOWS)):
        outs.append(jnp.dot(d_ref[:, gi * POOL_GROUP:(gi + 1) * POOL_GROUP], w_ref[gi], preferred_element_type=jnp.float32))
    mix = (jnp.concatenate(outs, axis=-1) + pb_ref[...]) * ps_ref[...]
    t = ALPHA * h_ref[...] + mix
    _norm_route(t, g_ref, b_ref, wr_ref, pl.program_id(0) * h_ref.shape[0], lp, seq_len, ho_ref, hb_ref, aff_ref, afft_ref)


def _mix_outputs(n, tm):
    row = lambda i: (i, 0)
    specs = [pl.BlockSpec((tm, D_MODEL), row), pl.BlockSpec((tm, D_MODEL), row), pl.BlockSpec((tm, LANES), row),
             pl.BlockSpec((1, N_EXPERTS, tm), lambda i: (i, 0, 0))]
    shapes = [jax.ShapeDtypeStruct((n, D_MODEL), jnp.float32), jax.ShapeDtypeStruct((n, D_MODEL), jnp.bfloat16),
              jax.ShapeDtypeStruct((n, LANES), jnp.float32), jax.ShapeDtypeStruct((n // tm, N_EXPERTS, tm), jnp.float32)]
    return specs, shapes


def out_proj_norm_route(y_ssd, y_att, w_out, h, g, bb, w_r, lp, seq_len):
    n = h.shape[0]
    tm = TOK_TILE
    row = lambda i: (i, 0)
    fix = lambda i: (0, 0)
    specs, shapes = _mix_outputs(n, tm)
    return pl.pallas_call(
        functools.partial(_outproj_kernel, lp=lp, seq_len=seq_len),
        grid=(n // tm,),
        in_specs=[pl.BlockSpec((tm, SSD_INNER), row), pl.BlockSpec((tm, ATT_INNER), row),
                  pl.BlockSpec((D_MODEL, D_MODEL), fix), pl.BlockSpec((tm, D_MODEL), row),
                  pl.BlockSpec((1, D_MODEL), fix), pl.BlockSpec((1, D_MODEL), fix), pl.BlockSpec((D_MODEL, LANES), fix)],
        out_specs=specs, out_shape=shapes,
        compiler_params=_cparams("parallel"),
        name="outproj_ln_route",
    )(y_ssd, y_att, w_out, h, g, bb, w_r)


def pool_proj_norm_route(d, pool_w, pool_b, pool_scale, h, g, bb, w_r, lp, seq_len):
    n = h.shape[0]
    tm = TOK_TILE
    row = lambda i: (i, 0)
    fix = lambda i: (0, 0)
    specs, shapes = _mix_outputs(n, tm)
    return pl.pallas_call(
        functools.partial(_poolproj_kernel, lp=lp, seq_len=seq_len),
        grid=(n // tm,),
        in_specs=[pl.BlockSpec((tm, D_MODEL), row),
                  pl.BlockSpec((len(POOL_WINDOWS), POOL_GROUP, POOL_GROUP), lambda i: (0, 0, 0)),
                  pl.BlockSpec((1, D_MODEL), fix), pl.BlockSpec((1, D_MODEL), fix), pl.BlockSpec((tm, D_MODEL), row),
                  pl.BlockSpec((1, D_MODEL), fix), pl.BlockSpec((1, D_MODEL), fix), pl.BlockSpec((D_MODEL, LANES), fix)],
        out_specs=specs, out_shape=shapes,
        compiler_params=_cparams("parallel"),
        name="poolproj_ln_route",
    )(d, pool_w, pool_b, pool_scale, h, g, bb, w_r)


def _pool_kernel(x_ref, o_ref, s0_ref, s1_ref, *, seq_len):
    lp = x_ref.shape[1]
    halo = 16
    gi = pl.program_id(1) // (POOL_GROUP // SLAB)
    zeros = jnp.zeros((halo, SLAB), jnp.float32)
    for ref in (s0_ref, s1_ref):
        ref[0:halo, :] = zeros
        ref[lp + halo:lp + 2 * halo, :] = zeros
    s0_ref[halo:lp + halo, :] = x_ref[0]
    t = lax.broadcasted_iota(jnp.int32, (lp, 1), 0)
    for wi, w in enumerate(POOL_WINDOWS):
        @pl.when(gi == wi)
        def _(w=w):
            src, dst = s0_ref, s1_ref
            width = 1
            while width < w:
                dst[halo:lp + halo, :] = src[halo:lp + halo, :] + src[halo + width:lp + halo + width, :]
                dst[0:halo, :] = src[0:halo, :] + src[width:halo + width, :]
                src, dst = dst, src
                width *= 2
            win = src[halo - w // 2:lp + halo - w // 2, :]
            cnt = jnp.minimum(t + w // 2 - 1, seq_len - 1) - jnp.maximum(t - w // 2, 0) + 1
            d = win / cnt.astype(jnp.float32) - x_ref[0]
            o_ref[0] = jnp.where(t < seq_len, d, 0.0).astype(o_ref.dtype)


def pool_diff(h3, seq_len):
    b, lp, _ = h3.shape
    return pl.pallas_call(
        functools.partial(_pool_kernel, seq_len=seq_len),
        grid=(b, D_MODEL // SLAB),
        in_specs=[pl.BlockSpec((1, lp, SLAB), lambda i, j: (i, 0, j))],
        out_specs=pl.BlockSpec((1, lp, SLAB), lambda i, j: (i, 0, j)),
        out_shape=jax.ShapeDtypeStruct((b, lp, D_MODEL), jnp.bfloat16),
        scratch_shapes=[pltpu.VMEM((lp + 32, SLAB), jnp.float32)] * 2,
        compiler_params=_cparams("parallel", "parallel"),
        name="pool_diff",
    )(h3)


def _route_kernel(aff_ref, pos_ref, start_ref, *, cap):
    nt = aff_ref.shape[0]
    t = TOK_TILE

    def bits_of(v):
        return lax.bitcast_convert_type(v, jnp.int32)

    def count(mask):
        return jnp.sum(jnp.sum(jnp.where(mask, 1.0, 0.0), axis=0), axis=1, keepdims=True)

    def search(i, thr):
        cand = thr | jnp.left_shift(jnp.int32(1), 30 - i)
        cnt = count(bits_of(aff_ref[...]) >= cand[None])
        return jnp.where(cnt >= cap, cand, thr)

    thr = lax.fori_loop(0, 31, search, jnp.zeros((N_EXPERTS, 1), jnp.int32))
    need = cap - count(bits_of(aff_ref[...]) > thr[None])

    before = (lax.broadcasted_iota(jnp.int32, (t, t), 0) < lax.broadcasted_iota(jnp.int32, (t, t), 1)).astype(jnp.bfloat16)
    ones8 = jnp.ones((8, t), jnp.bfloat16)
    zpad = jnp.zeros((LANES - N_EXPERTS, t), jnp.float32)

    def tile(ti, carry):
        c_tie, c_sel, c_lane = carry
        b = bits_of(aff_ref[ti])
        tie = jnp.where(b == thr, 1.0, 0.0)
        rank = jnp.dot(tie.astype(jnp.bfloat16), before, preferred_element_type=jnp.float32) + c_tie
        sel = jnp.where(b > thr, 1.0, jnp.where(rank < need, tie, 0.0))
        slot = jnp.dot(sel.astype(jnp.bfloat16), before, preferred_element_type=jnp.float32) + c_sel
        pos_ref[ti] = jnp.where(sel > 0.0, slot, -1.0).astype(jnp.int32)
        start_ref[ti] = c_lane.astype(jnp.int32)
        selp = jnp.concatenate([sel, zpad], axis=0).astype(jnp.bfloat16)
        cnt_lane = lax.dot_general(ones8, selp, (((1,), (1,)), ((), ())), preferred_element_type=jnp.float32)
        return (c_tie + jnp.sum(tie, axis=1, keepdims=True), c_sel + jnp.sum(sel, axis=1, keepdims=True),
                c_lane + cnt_lane[0:1])

    zc = jnp.zeros((N_EXPERTS, 1), jnp.float32)
    _, _, c_lane = lax.fori_loop(0, nt, tile, (zc, zc, jnp.zeros((1, LANES), jnp.float32)))
    start_ref[nt] = c_lane.astype(jnp.int32)


def route(afft, cap):
    nt = afft.shape[0]
    return pl.pallas_call(
        functools.partial(_route_kernel, cap=cap),
        out_shape=[jax.ShapeDtypeStruct((nt, N_EXPERTS, TOK_TILE), jnp.int32),
                   jax.ShapeDtypeStruct((nt + 1, 1, LANES), jnp.int32)],
        compiler_params=pltpu.CompilerParams(vmem_limit_bytes=VMEM_LIMIT),
        name="route",
    )(afft)


HALF_EXPERTS = N_EXPERTS // 2
_REGION_MAX = -(-(TOK_TILE + BF16_ROWS - 1) // DMA_ROWS) * DMA_ROWS
HALF_ROWS = HALF_EXPERTS * _REGION_MAX + SEQ_ALIGN
XE_COLS = D_MODEL + LANES


def _tile_plan(st_ref, ti, half):
    plan = []
    off = jnp.int32(0)
    for e in range(half * HALF_EXPERTS, (half + 1) * HALF_EXPERTS):
        s = st_ref[ti * N_EXPERTS + e]
        end = st_ref[(ti + 1) * N_EXPERTS + e]
        s16 = s - (s & (BF16_ROWS - 1))
        nch = jnp.where(end > s, lax.shift_right_logical(end - s16 + DMA_ROWS - 1, DMA_ROWS.bit_length() - 1), 0)
        plan.append((e, s16, nch, off, end))
        off = off + nch * DMA_ROWS
    return plan, off


def _stage_rows(pos_half, plan):
    sub = lax.broadcasted_iota(jnp.int32, (HALF_EXPERTS, 1), 0)
    delta = jnp.zeros((HALF_EXPERTS, 1), jnp.int32)
    for i, (e, s16, nch, off, end) in enumerate(plan):
        delta = jnp.where(sub == i, off - s16, delta)
    return jnp.where(pos_half >= 0, pos_half + delta, -1)


def _onehot_rows(srow, k0):
    k = k0 + lax.broadcasted_iota(jnp.int32, (SEQ_ALIGN, 1), 0)
    onehot = jnp.zeros((SEQ_ALIGN, srow.shape[1]), jnp.float32)
    for i in range(srow.shape[0]):
        onehot = jnp.where(srow[i:i + 1, :] == k, 1.0, onehot)
    return onehot.astype(jnp.bfloat16)


def _n_blocks(rows):
    return lax.shift_right_logical(rows + SEQ_ALIGN - 1, SEQ_ALIGN.bit_length() - 1)


def _dispatch_kernel(st_ref, x_ref, pos_ref, aff_ref, xe_ref, sx_ref, cx_ref, n_ref, sem_ref, *, cap, cap_pad):
    ti = pl.program_id(0)
    nt = pl.num_programs(0)

    def chunk_copy(half, e, s16, off, j):
        src = pl.multiple_of(off + j * DMA_ROWS, DMA_ROWS)
        dst = pl.multiple_of(s16 + j * DMA_ROWS, BF16_ROWS)
        return pltpu.make_async_copy(sx_ref.at[half, pl.ds(src, DMA_ROWS)], xe_ref.at[e, pl.ds(dst, DMA_ROWS)],
                                     sem_ref.at[half])

    def wait_chunks(half, count):
        def body(_, c):
            chunk_copy(half, 0, 0, 0, 0).wait()
            return c
        lax.fori_loop(0, count, body, 0)

    tail0 = (cap // DMA_ROWS) * DMA_ROWS
    n_tail = (cap_pad - tail0) // DMA_ROWS

    @pl.when(ti == 0)
    def _():
        cx_ref[...] = jnp.zeros(cx_ref.shape, cx_ref.dtype)
        sx_ref[0, 0:DMA_ROWS, :] = jnp.zeros((DMA_ROWS, XE_COLS), sx_ref.dtype)
        for e in range(N_EXPERTS):
            for j in range(n_tail):
                pltpu.make_async_copy(sx_ref.at[0, pl.ds(0, DMA_ROWS)],
                                      xe_ref.at[e, pl.ds(tail0 + j * DMA_ROWS, DMA_ROWS)], sem_ref.at[0]).start()
        wait_chunks(0, N_EXPERTS * n_tail)
        n_ref[0] = 0
        n_ref[1] = 0

    aff = aff_ref[...]
    a1, a2, a3 = _split3(aff)
    gate3 = (a1.astype(jnp.float32) + pltpu.roll(a2.astype(jnp.float32), N_EXPERTS, 1)
             + pltpu.roll(a3.astype(jnp.float32), 2 * N_EXPERTS, 1)).astype(jnp.bfloat16)
    xt = jnp.concatenate([x_ref[...], gate3], axis=1)

    for half in range(2):
        plan, k_tot = _tile_plan(st_ref, ti, half)
        srow = _stage_rows(pos_ref[0, half * HALF_EXPERTS:(half + 1) * HALF_EXPERTS, :], plan)
        wait_chunks(half, n_ref[half])

        def permute(kc, c, half=half, srow=srow):
            k0 = pl.multiple_of(kc * SEQ_ALIGN, SEQ_ALIGN)
            sx_ref[half, pl.ds(k0, SEQ_ALIGN), :] = jnp.dot(_onehot_rows(srow, k0), xt,
                                                            preferred_element_type=jnp.float32).astype(sx_ref.dtype)
            return c

        lax.fori_loop(0, _n_blocks(k_tot), permute, 0)

        total = jnp.int32(0)
        for e, s16, nch, off, end in plan:
            @pl.when(nch > 0)
            def _(half=half, e=e, s16=s16, off=off, end=end):
                head = pl.ds(pl.multiple_of(off, BF16_ROWS), BF16_ROWS)
                sx_ref[half, head, :] = sx_ref[half, head, :] + cx_ref[e]
                part = end & (BF16_ROWS - 1)
                last = pl.ds(pl.multiple_of(off + (end - part) - s16, BF16_ROWS), BF16_ROWS)
                cx_ref[e] = jnp.where(part > 0, sx_ref[half, last, :], jnp.zeros((BF16_ROWS, XE_COLS), cx_ref.dtype))

            def issue(j, c, half=half, e=e, s16=s16, off=off):
                chunk_copy(half, e, s16, off, j).start()
                return c
            lax.fori_loop(0, nch, issue, 0)
            total = total + nch
        n_ref[half] = total

    @pl.when(ti == nt - 1)
    def _():
        wait_chunks(0, n_ref[0])
        wait_chunks(1, n_ref[1])


def dispatch(starts, hb, pos, aff, cap, cap_pad):
    n = hb.shape[0]
    nt = n // TOK_TILE
    grid_spec = pltpu.PrefetchScalarGridSpec(
        num_scalar_prefetch=1,
        grid=(nt,),
        in_specs=[pl.BlockSpec((TOK_TILE, D_MODEL), lambda i, st: (i, 0)),
                  pl.BlockSpec((1, N_EXPERTS, TOK_TILE), lambda i, st: (i, 0, 0)),
                  pl.BlockSpec((TOK_TILE, LANES), lambda i, st: (i, 0))],
        out_specs=pl.BlockSpec(memory_space=pl.ANY),
        scratch_shapes=[pltpu.VMEM((2, HALF_ROWS, XE_COLS), jnp.bfloat16),
                        pltpu.VMEM((N_EXPERTS, BF16_ROWS, XE_COLS), jnp.bfloat16),
                        pltpu.SMEM((2,), jnp.int32), pltpu.SemaphoreType.DMA((2,))],
    )
    return pl.pallas_call(
        functools.partial(_dispatch_kernel, cap=cap, cap_pad=cap_pad),
        grid_spec=grid_spec,
        out_shape=jax.ShapeDtypeStruct((N_EXPERTS, cap_pad, XE_COLS), jnp.bfloat16),
        compiler_params=_cparams("arbitrary"),
        name="moe_dispatch",
    )(starts, hb, pos, aff)


def _ffn_kernel(x_ref, w1_ref, w3_ref, w2_ref, o_ref, acc_ref, *, cap):
    e = pl.program_id(0)
    f = pl.program_id(2)
    @pl.when(f == 0)
    def _():
        acc_ref[...] = jnp.zeros(acc_ref.shape, jnp.float32)

    x = x_ref[0, :, 0:D_MODEL]
    a = jnp.dot(x, w1_ref[0, 0], preferred_element_type=jnp.float32)
    b = jnp.dot(x, w3_ref[0, 0], preferred_element_type=jnp.float32)
    hid = (a * _sigmoid(a) * b).astype(jnp.bfloat16)
    acc_ref[...] = acc_ref[...] + jnp.dot(hid, w2_ref[0, 0], preferred_element_type=jnp.float32)

    @pl.when(f == pl.num_programs(2) - 1)
    def _():
        rows = o_ref.shape[1]
        gl = x_ref[0, :, D_MODEL:XE_COLS].astype(jnp.float32)
        lane = lax.broadcasted_iota(jnp.int32, gl.shape, 1)
        mine = ((lane & (N_EXPERTS - 1)) == e) & (lane < 3 * N_EXPERTS)
        gate = jnp.sum(jnp.where(mine, gl, 0.0), axis=-1, keepdims=True)
        slot = pl.program_id(1) * rows + lax.broadcasted_iota(jnp.int32, (rows, 1), 0)
        o_ref[0] = jnp.where(slot < cap, acc_ref[...] * gate, 0.0).astype(o_ref.dtype)


def expert_ffn(xe, w1, w3, w2, layer, cap):
    e, cap_pad, _ = xe.shape
    return pl.pallas_call(
        functools.partial(_ffn_kernel, cap=cap),
        grid=(e, cap_pad // FFN_ROWS, EXPERT_FF // FFN_COLS),
        in_specs=[pl.BlockSpec((1, FFN_ROWS, XE_COLS), lambda i, m, f: (i, m, 0)),
                  pl.BlockSpec((1, 1, D_MODEL, FFN_COLS), lambda i, m, f: (layer, i, 0, f)),
                  pl.BlockSpec((1, 1, D_MODEL, FFN_COLS), lambda i, m, f: (layer, i, 0, f)),
                  pl.BlockSpec((1, 1, FFN_COLS, D_MODEL), lambda i, m, f: (layer, i, f, 0))],
        out_specs=pl.BlockSpec((1, FFN_ROWS, D_MODEL), lambda i, m, f: (i, m, 0)),
        out_shape=jax.ShapeDtypeStruct((e, cap_pad, D_MODEL), jnp.bfloat16),
        scratch_shapes=[pltpu.VMEM((FFN_ROWS, D_MODEL), jnp.float32)],
        compiler_params=_cparams("parallel", "parallel", "arbitrary"),
        name="expert_ffn",
    )(xe, w1, w3, w2)


def _combine_kernel(st_ref, ye_ref, pos_ref, h_ref, g_ref, b_ref, *rest, lp, seq_len, final):
    if final:
        out_ref, sy_ref, acc_ref, n_ref, sem_ref, ybuf_ref, osem_ref = rest
    else:
        ho_ref, hb_ref, sy_ref, acc_ref, n_ref, sem_ref = rest
    ti = pl.program_id(0)
    nt = pl.num_programs(0)
    t = TOK_TILE

    def chunk_copy(half, e, s16, off, j):
        src = pl.multiple_of(s16 + j * DMA_ROWS, BF16_ROWS)
        dst = pl.multiple_of(off + j * DMA_ROWS, DMA_ROWS)
        return pltpu.make_async_copy(ye_ref.at[e, pl.ds(src, DMA_ROWS)], sy_ref.at[half, pl.ds(dst, DMA_ROWS)],
                                     sem_ref.at[half])

    def fetch(half, tile):
        plan, _ = _tile_plan(st_ref, tile, half)
        total = jnp.int32(0)
        for e, s16, nch, off, end in plan:
            def issue(j, c, e=e, s16=s16, off=off):
                chunk_copy(half, e, s16, off, j).start()
                return c
            lax.fori_loop(0, nch, issue, 0)
            total = total + nch
        n_ref[half] = total

    @pl.when(ti == 0)
    def _():
        sy_ref[...] = jnp.zeros(sy_ref.shape, sy_ref.dtype)
        fetch(0, 0)
        fetch(1, 0)

    acc_ref[...] = jnp.zeros(acc_ref.shape, jnp.float32)
    for half in range(2):
        plan, k_tot = _tile_plan(st_ref, ti, half)
        srow = _stage_rows(pos_ref[0, half * HALF_EXPERTS:(half + 1) * HALF_EXPERTS, :], plan)

        def wait_one(_, c, half=half):
            chunk_copy(half, 0, 0, 0, 0).wait()
            return c
        lax.fori_loop(0, n_ref[half], wait_one, 0)

        def gather(kc, c, half=half, srow=srow):
            k0 = pl.multiple_of(kc * SEQ_ALIGN, SEQ_ALIGN)
            acc_ref[...] = acc_ref[...] + lax.dot_general(
                _onehot_rows(srow, k0), sy_ref[half, pl.ds(k0, SEQ_ALIGN), :], (((0,), (0,)), ((), ())),
                preferred_element_type=jnp.float32)
            return c

        lax.fori_loop(0, _n_blocks(k_tot), gather, 0)

        @pl.when(ti + 1 < nt)
        def _(half=half):
            fetch(half, ti + 1)

    tsum = ALPHA * h_ref[...] + acc_ref[...]
    if not final:
        _norm_route(tsum, g_ref, b_ref, None, ti * t, lp, seq_len, ho_ref, hb_ref, None, None)
        return

    nts = lp // t

    def out_rows(tile):
        tok0 = lax.rem(tile, nts) * t
        first = jnp.where(tok0 == 0, N_META, 0)
        return first, jnp.clip(seq_len - tok0, 0, t) - first

    def out_copy(slot, bi, src, dst):
        return pltpu.make_async_copy(ybuf_ref.at[slot, pl.ds(src, OUT_ROWS)], out_ref.at[bi, pl.ds(dst, OUT_ROWS)],
                                     osem_ref.at[slot])

    def wait_out(slot, tile):
        def body(_, c):
            out_copy(slot, 0, 0, 0).wait()
            return c
        lax.fori_loop(0, out_rows(tile)[1] // OUT_ROWS, body, 0)

    slot = ti & 1

    @pl.when(ti >= 2)
    def _():
        wait_out(slot, ti - 2)

    y, _ = _layer_norm_rows(tsum, g_ref, b_ref, ti * t, lp, seq_len)
    ybuf_ref[slot] = y
    first, rows = out_rows(ti)
    bi = ti // nts
    tok0 = lax.rem(ti, nts) * t

    def send(j, c):
        src = pl.multiple_of(first + j * OUT_ROWS, OUT_ROWS)
        out_copy(slot, bi, src, pl.multiple_of(tok0 - N_META + src, OUT_ROWS)).start()
        return c
    lax.fori_loop(0, rows // OUT_ROWS, send, 0)

    @pl.when(ti == nt - 1)
    def _():
        wait_out(slot, ti)

        @pl.when(nt > 1)
        def _():
            wait_out(1 - slot, ti - 1)


def combine(starts, ye, pos, h, g, bb, lp, seq_len, final):
    n = h.shape[0]
    nt = n // TOK_TILE
    row = lambda i, st: (i, 0)
    fix = lambda i, st: (0, 0)
    scratch = [pltpu.VMEM((2, HALF_ROWS, D_MODEL), jnp.bfloat16), pltpu.VMEM((TOK_TILE, D_MODEL), jnp.float32),
               pltpu.SMEM((2,), jnp.int32), pltpu.SemaphoreType.DMA((2,))]
    if final:
        assert seq_len % OUT_ROWS == 0 and N_META % OUT_ROWS == 0
        out_specs = pl.BlockSpec(memory_space=pl.ANY)
        out_shape = jax.ShapeDtypeStruct((n // lp, seq_len - N_META, D_MODEL), jnp.float32)
        scratch += [pltpu.VMEM((2, TOK_TILE, D_MODEL), jnp.float32), pltpu.SemaphoreType.DMA((2,))]
    else:
        out_specs = [pl.BlockSpec((TOK_TILE, D_MODEL), row), pl.BlockSpec((TOK_TILE, D_MODEL), row)]
        out_shape = [jax.ShapeDtypeStruct((n, D_MODEL), jnp.float32), jax.ShapeDtypeStruct((n, D_MODEL), jnp.bfloat16)]
    grid_spec = pltpu.PrefetchScalarGridSpec(
        num_scalar_prefetch=1,
        grid=(nt,),
        in_specs=[pl.BlockSpec(memory_space=pl.ANY), pl.BlockSpec((1, N_EXPERTS, TOK_TILE), lambda i, st: (i, 0, 0)),
                  pl.BlockSpec((TOK_TILE, D_MODEL), row), pl.BlockSpec((1, D_MODEL), fix), pl.BlockSpec((1, D_MODEL), fix)],
        out_specs=out_specs,
        scratch_shapes=scratch,
    )
    return pl.pallas_call(
        functools.partial(_combine_kernel, lp=lp, seq_len=seq_len, final=final),
        grid_spec=grid_spec,
        out_shape=out_shape,
        compiler_params=_cparams("arbitrary"),
        name="moe_combine_out" if final else "moe_combine",
    )(starts, ye, pos, h, g, bb)


def moe_block(h, hb, aff, afft, w1, w3, w2, layer, g, bb, n_real, lp, seq_len, final=False):
    cap = max(1, CAPACITY_FACTOR * n_real // N_EXPERTS)
    cap_pad = -(-(cap + DMA_ROWS) // FFN_ROWS) * FFN_ROWS
    pos, start3 = route(afft, cap)
    starts = start3[:, 0, :N_EXPERTS].reshape(-1)
    xe = dispatch(starts, hb, pos, aff, cap, cap_pad)
    ye = expert_ffn(xe, w1, w3, w2, layer, cap)
    return combine(starts, ye, pos, h, g, bb, lp, seq_len, final)


def _prep_weights(meta, w_in, conv_w, conv_b, a_log_f, a_log_b, dt_bias_f, dt_bias_b, d_skip, ssd_norm_w,
                  lam_q1, lam_k1, lam_q2, lam_k2, subln_w, rel_bias, w_out, pool_w, pool_b, pool_scale,
                  ln1_g, ln1_b, ln2_g, ln2_b, w_router, w1, w3, w2):
    bf = jnp.bfloat16
    f32 = jnp.float32
    w = w_in[0]
    o = [0, SSD_INNER, SSD_INNER + SSD_CONV_CH, SSD_INNER + SSD_CONV_CH + 2 * SSD_HEADS]
    wz, wxbc, wdt = w[:, o[0]:o[1]], w[:, o[1]:o[2]], w[:, o[2]:o[3]]
    wq, wk, wv = (w[:, o[3] + i * ATT_QK:o[3] + (i + 1) * ATT_QK] for i in range(3))
    pad = jnp.zeros((D_MODEL, _PROJ_COLS - _COL_DT - 2 * SSD_HEADS), f32)
    w_tok = jnp.concatenate([wz, wxbc, wdt, pad], axis=1).astype(bf)
    w_qv = jnp.concatenate([wq * (ATT_HEAD_DIM ** -0.5 * LOG2E), wv], axis=1).T.astype(bf)
    lane_pad = lambda v: jnp.pad(v.astype(f32), (0, LANES - v.shape[0]))
    par = jnp.stack([lane_pad(jnp.concatenate([dt_bias_f[0], dt_bias_b[0]])),
                     lane_pad(jnp.concatenate([a_log_f[0], a_log_b[0]]))] + [jnp.zeros((LANES,), f32)] * 6)
    lam4 = jnp.stack([lane_pad(lam_q1[0]), lane_pad(lam_k1[0]), lane_pad(lam_q2[0]), lane_pad(lam_k2[0])]
                     + [jnp.zeros((LANES,), f32)] * 4)
    return dict(
        meta=meta, w_tok=w_tok, w_k=wk.astype(bf), w_qv=w_qv, conv_w=conv_w[0, :, 0, :], conv_b=conv_b[0][None], par=par,
        dsk=jnp.repeat(d_skip[0], SSD_HEAD_DIM)[None], nw=ssd_norm_w[0][None], lam4=lam4,
        subw=jnp.broadcast_to(subln_w[0][:, None], (ATT_V_DIM, SEQ_ALIGN)), rel_bias=rel_bias,
        w_out=w_out[0].astype(bf), pool_w=pool_w[0].astype(bf), pool_b=pool_b[0][None], pool_scale=pool_scale[0][None],
        ln1_g=ln1_g, ln1_b=ln1_b, ln2_g=ln2_g, ln2_b=ln2_b,
        w_r=jnp.pad(w_router, ((0, 0), (0, 0), (0, LANES - N_EXPERTS))).astype(bf),
        w1=w1.astype(bf), w3=w3.astype(bf), w2=w2.astype(bf))


def _trunk(x, p, bias5):
    b, s, _ = x.shape
    seq_len = s + N_META
    lp = -(-seq_len // SEQ_ALIGN) * SEQ_ALIGN
    n = b * lp
    n_real = b * seq_len
    meta = jnp.broadcast_to(p["meta"][None].astype(x.dtype), (b, N_META, D_MODEL))
    h = jnp.concatenate([meta, x, jnp.zeros((b, lp - seq_len, D_MODEL), x.dtype)], axis=1).reshape(n, D_MODEL)
    hb = h.astype(jnp.bfloat16)
    lambda_init = 0.8 - 0.6 * math.exp(-0.3 * 0)
    tm = next(c for c in (ROW_TILE, ROW_TILE // 2, SEQ_ALIGN) if n % c == 0)

    proj = mm_nn(hb, p["w_tok"], jnp.float32, tm, 512)
    k3 = mm_nn(hb, p["w_k"], jnp.bfloat16, tm, 512).reshape(b, lp, ATT_QK)
    qv = mm_nt_blocked(p["w_qv"], hb, jnp.bfloat16, tm, 512)
    proj3 = proj.reshape(b, lp, _PROJ_COLS)
    xbc = conv_silu(proj3, p["conv_w"], p["conv_b"]).reshape(n, SSD_CONV_CH)
    yf = ssd_scan(xbc, proj, p["par"], b, lp, seq_len, False)
    y_ssd = ssd_scan(xbc, proj, p["par"], b, lp, seq_len, True, gate=(yf, p["dsk"], p["nw"]))
    y_att = diff_attention(qv, k3, bias5, p["lam4"], p["subw"], b, lp, seq_len, lambda_init)
    h, hb, aff, afft = out_proj_norm_route(y_ssd, y_att, p["w_out"], h, p["ln1_g"][0][None], p["ln1_b"][0][None],
                                           p["w_r"][0], lp, seq_len)
    h, hb = moe_block(h, hb, aff, afft, p["w1"], p["w3"], p["w2"], 0, p["ln2_g"][0][None], p["ln2_b"][0][None],
                      n_real, lp, seq_len)
    d = pool_diff(h.reshape(b, lp, D_MODEL), seq_len).reshape(n, D_MODEL)
    h, hb, aff, afft = pool_proj_norm_route(d, p["pool_w"], p["pool_b"], p["pool_scale"], h, p["ln1_g"][1][None],
                                            p["ln1_b"][1][None], p["w_r"][1], lp, seq_len)
    return moe_block(h, hb, aff, afft, p["w1"], p["w3"], p["w2"], 1, p["ln2_g"][1][None], p["ln2_b"][1][None],
                     n_real, lp, seq_len, final=True)


def kernel(x_prompt, x_sample, meta, w_in, conv_w, conv_b, a_log_f, a_log_b, dt_bias_f, dt_bias_b, d_skip, ssd_norm_w, lam_q1, lam_k1, lam_q2, lam_k2, subln_w, rel_bias, w_out, pool_w, pool_b, pool_scale, ln1_g, ln1_b, ln2_g, ln2_b, w_router, w1, w3, w2):
    p = _prep_weights(meta, w_in, conv_w, conv_b, a_log_f, a_log_b, dt_bias_f, dt_bias_b, d_skip, ssd_norm_w,
                      lam_q1, lam_k1, lam_q2, lam_k2, subln_w, rel_bias, w_out, pool_w, pool_b, pool_scale,
                      ln1_g, ln1_b, ln2_g, ln2_b, w_router, w1, w3, w2)
    bias5 = bias_tiles(rel_bias)
    return (_trunk(x_prompt, p, bias5), _trunk(x_sample, p, bias5))
```

```python
import functools
import math

import jax
import jax.numpy as jnp
from jax import lax
from jax.experimental import pallas as pl
from jax.experimental.pallas import tpu as pltpu

D_MODEL = 2048
DEPTH = 2
N_META = 16
SSD_HEADS = 16
SSD_HEAD_DIM = 64
SSD_INNER = SSD_HEADS * SSD_HEAD_DIM
SSD_GROUPS = 4
SSD_STATE = 128
SSD_CONV = 5
SSD_CHUNK = 128
SSD_CONV_CH = SSD_INNER + 2 * SSD_GROUPS * SSD_STATE
ATT_HEADS = 8
ATT_HEAD_DIM = 64
ATT_V_DIM = 2 * ATT_HEAD_DIM
ATT_QK = ATT_HEADS * 2 * ATT_HEAD_DIM
ATT_INNER = ATT_HEADS * ATT_V_DIM
REL_BUCKETS = 32
POOL_WINDOWS = (2, 4, 8, 16)
POOL_GROUP = D_MODEL // len(POOL_WINDOWS)
N_EXPERTS = 16
EXPERT_FF = 2048
CAPACITY_FACTOR = 2
ALPHA = (2 * DEPTH) ** 0.25
LN_EPS = 1e-5

LANES = 128
BF16_ROWS = 16
SEQ_ALIGN = 256
TOK_TILE = 256
ROW_TILE = 1024
DMA_ROWS = 32
SLAB = 256
OUT_ROWS = 16
FFN_ROWS = 704
FFN_COLS = 512
VMEM_LIMIT = 56 * 1024 * 1024

_BUCKET_STARTS = (1, 2, 3, 4, 5, 6, 7, 8, 12, 16, 23, 32, 46, 64, 91)
_COL_Z, _COL_XBC, _COL_DT, _PROJ_COLS = 0, 1024, 3072, 3584
ATT_CHUNK = 4
ATT_HEADS_PER_STEP = 2
LOG2E = math.log2(math.e)


def _cparams(*sem):
    return pltpu.CompilerParams(dimension_semantics=sem, vmem_limit_bytes=VMEM_LIMIT)


def _sigmoid(x):
    return 1.0 / (1.0 + jnp.exp(-x))


def _softplus(x):
    return jnp.maximum(x, 0.0) + jnp.log(1.0 + jnp.exp(-jnp.abs(x)))


def _split3(x):
    p1 = x.astype(jnp.bfloat16)
    r1 = x - p1.astype(jnp.float32)
    p2 = r1.astype(jnp.bfloat16)
    p3 = (r1 - p2.astype(jnp.float32)).astype(jnp.bfloat16)
    return p1, p2, p3


def _mm_nn_kernel(x_ref, w_ref, o_ref):
    o_ref[...] = jnp.dot(x_ref[...], w_ref[...], preferred_element_type=jnp.float32).astype(o_ref.dtype)


def mm_nn(x, w, out_dtype, tm, tn):
    n, k = x.shape
    nn = w.shape[1]
    return pl.pallas_call(
        _mm_nn_kernel,
        grid=(n // tm, nn // tn),
        in_specs=[pl.BlockSpec((tm, k), lambda i, j: (i, 0)), pl.BlockSpec((k, tn), lambda i, j: (0, j))],
        out_specs=pl.BlockSpec((tm, tn), lambda i, j: (i, j)),
        out_shape=jax.ShapeDtypeStruct((n, nn), out_dtype),
        compiler_params=_cparams("parallel", "arbitrary"),
        name="mm_nn",
    )(x, w)


def _mm_nt_kernel(w_ref, x_ref, o_ref):
    r = lax.dot_general(w_ref[...], x_ref[...], (((1,), (1,)), ((), ())), preferred_element_type=jnp.float32)
    for s in range(o_ref.shape[0]):
        o_ref[s] = r[:, s * SEQ_ALIGN:(s + 1) * SEQ_ALIGN].astype(o_ref.dtype)


def mm_nt_blocked(wt, x, out_dtype, tm, tn):
    n, k = x.shape
    nn = wt.shape[0]
    per = tm // SEQ_ALIGN
    return pl.pallas_call(
        _mm_nt_kernel,
        grid=(n // tm, nn // tn),
        in_specs=[pl.BlockSpec((tn, k), lambda i, j: (j, 0)), pl.BlockSpec((tm, k), lambda i, j: (i, 0))],
        out_specs=pl.BlockSpec((per, tn, SEQ_ALIGN), lambda i, j: (i, j, 0)),
        out_shape=jax.ShapeDtypeStruct((n // SEQ_ALIGN, nn, SEQ_ALIGN), out_dtype),
        compiler_params=_cparams("parallel", "arbitrary"),
        name="mm_nt",
    )(wt, x)


def _conv_kernel(x_ref, w_ref, b_ref, o_ref, pad_ref, *, rows):
    lp = x_ref.shape[1]
    halo = 8
    pad_ref[0:halo, :] = jnp.zeros((halo, SLAB), jnp.float32)
    pad_ref[lp + halo:lp + 2 * halo, :] = jnp.zeros((halo, SLAB), jnp.float32)
    pad_ref[halo:lp + halo, :] = x_ref[0]
    for t0 in range(0, lp, rows):
        acc = jnp.broadcast_to(b_ref[...], (rows, SLAB))
        for k in range(SSD_CONV):
            off = t0 + halo - SSD_CONV // 2 + k
            acc = acc + pad_ref[off:off + rows, :] * w_ref[k:k + 1, :]
        o_ref[0, t0:t0 + rows, :] = acc * _sigmoid(acc)


def conv_silu(proj3, conv_w, conv_b):
    b, lp, _ = proj3.shape
    c0 = _COL_XBC // SLAB
    return pl.pallas_call(
        functools.partial(_conv_kernel, rows=SEQ_ALIGN),
        grid=(b, SSD_CONV_CH // SLAB),
        in_specs=[
            pl.BlockSpec((1, lp, SLAB), lambda i, j: (i, 0, c0 + j)),
            pl.BlockSpec((SSD_CONV, SLAB), lambda i, j: (0, j)),
            pl.BlockSpec((1, SLAB), lambda i, j: (0, j)),
        ],
        out_specs=pl.BlockSpec((1, lp, SLAB), lambda i, j: (i, 0, j)),
        out_shape=jax.ShapeDtypeStruct((b, lp, SSD_CONV_CH), jnp.float32),
        scratch_shapes=[pltpu.VMEM((lp + 16, SLAB), jnp.float32)],
        compiler_params=_cparams("parallel", "parallel"),
        name="conv_silu",
    )(proj3, conv_w, conv_b)


def _ssd_kernel(x_ref, b_ref, c_ref, dt_ref, par_ref, *rest, seq_len, reverse, gated):
    if gated:
        yo_ref, z_ref, dsk_ref, nw_ref, y_ref, h_ref = rest
    else:
        y_ref, h_ref = rest
    cs = SSD_CHUNK
    step = pl.program_id(1)
    chunk = (pl.num_programs(1) - 1 - step) if reverse else step
    lane0 = SSD_HEADS if reverse else 0

    @pl.when(step == 0)
    def _():
        h_ref[...] = jnp.zeros(h_ref.shape, jnp.float32)

    sub = lax.broadcasted_iota(jnp.int32, (cs, cs), 0)
    lan = lax.broadcasted_iota(jnp.int32, (cs, cs), 1)
    valid = (chunk * cs + sub[:, 0:1]) < seq_len
    dt = jnp.where(valid, _softplus(dt_ref[...] + par_ref[0:1, :]), 0.0)
    a = dt * -jnp.exp(par_ref[1:2, :])
    a_t = a.T[0:2 * SSD_HEADS, :]
    if reverse:
        incl, rest = (sub >= lan), (sub < lan)
    else:
        incl, rest = (sub <= lan), (sub > lan)
    tri = jnp.concatenate([incl.astype(jnp.bfloat16), rest.astype(jnp.bfloat16),
                           jnp.ones((cs, cs), jnp.bfloat16)], axis=1)
    pieces = jnp.concatenate(_split3(a_t), axis=0)
    sums = jnp.dot(pieces, tri, preferred_element_type=jnp.float32)
    nh2 = 2 * SSD_HEADS
    sums = sums[0:nh2] + sums[nh2:2 * nh2] + sums[2 * nh2:3 * nh2]
    cum_row, rest_row, tot_row = sums[:, 0:cs], sums[:, cs:2 * cs], sums[:, 2 * cs:3 * cs]
    cum_col = jnp.concatenate([cum_row, jnp.zeros((cs - nh2, cs), jnp.float32)], axis=0).T
    mask = (sub <= lan) if reverse else (sub >= lan)
    left = lan[0:1, :] < SSD_HEAD_DIM
    gated_pairs = []

    for g in range(SSD_GROUPS):
        bg = b_ref[:, g * SSD_STATE:(g + 1) * SSD_STATE]
        cg = c_ref[:, g * SSD_STATE:(g + 1) * SSD_STATE]
        cb = lax.dot_general(cg.astype(jnp.bfloat16), bg.astype(jnp.bfloat16), (((1,), (1,)), ((), ())),
                             preferred_element_type=jnp.float32)
        bg_t = bg.T
        for pp in range(2):
            p = 2 * g + pp
            r0, r1 = lane0 + 2 * p, lane0 + 2 * p + 1
            dtp = jnp.where(left, dt[:, r0:r0 + 1], dt[:, r1:r1 + 1])
            xs = (x_ref[:, p * LANES:(p + 1) * LANES] * dtp).astype(jnp.bfloat16)
            h_t = h_ref[p]
            rhs = jnp.concatenate([xs, h_t.astype(jnp.bfloat16)], axis=0)
            ys, ss = [], []
            for r in (r0, r1):
                cc = cum_col[:, r:r + 1]
                dec = jnp.exp(jnp.where(mask, cc - cum_row[r:r + 1, :], -jnp.inf))
                lhs = jnp.concatenate([(cb * dec).astype(jnp.bfloat16),
                                       (cg * jnp.exp(cc)).astype(jnp.bfloat16)], axis=1)
                ys.append(jnp.dot(lhs, rhs, preferred_element_type=jnp.float32))
                bw = (bg_t * jnp.exp(rest_row[r:r + 1, :])).astype(jnp.bfloat16)
                ss.append(jnp.dot(bw, xs, preferred_element_type=jnp.float32))
            lanes = slice(p * LANES, (p + 1) * LANES)
            y_pair = jnp.where(left, ys[0], ys[1])
            if gated:
                z = z_ref[:, lanes]
                gated_pairs.append((yo_ref[:, lanes] + y_pair + dsk_ref[:, lanes] * x_ref[:, lanes]) * (z * _sigmoid(z)))
            else:
                y_ref[:, lanes] = y_pair
            decay = jnp.where(left, jnp.exp(tot_row[r0:r0 + 1, :]), jnp.exp(tot_row[r1:r1 + 1, :]))
            h_ref[p] = decay * h_t + jnp.where(left, ss[0], ss[1])
        if gated:
            seg = jnp.concatenate(gated_pairs, axis=1)
            gated_pairs = []
            ms = jnp.mean(seg * seg, axis=-1, keepdims=True)
            cols = slice(2 * g * LANES, 2 * (g + 1) * LANES)
            y_ref[:, cols] = (seg * lax.rsqrt(ms + LN_EPS) * nw_ref[:, cols]).astype(y_ref.dtype)


def ssd_scan(xbc, proj, par, b, lp, seq_len, reverse, gate=None):
    n = xbc.shape[0]
    nc = lp // SSD_CHUNK
    cdt = _COL_DT // LANES

    def row(i, c):
        return i * nc + ((nc - 1 - c) if reverse else c)

    tok = lambda col: (lambda i, c: (row(i, c), col))
    fix = lambda i, c: (0, 0)
    in_specs = [
        pl.BlockSpec((SSD_CHUNK, SSD_INNER), tok(0)),
        pl.BlockSpec((SSD_CHUNK, SSD_GROUPS * SSD_STATE), tok(2)),
        pl.BlockSpec((SSD_CHUNK, SSD_GROUPS * SSD_STATE), tok(3)),
        pl.BlockSpec((SSD_CHUNK, LANES), tok(cdt)),
        pl.BlockSpec((8, LANES), fix),
    ]
    args = [xbc, xbc, xbc, proj, par]
    if gate is not None:
        y_other, dsk, nw = gate
        in_specs += [pl.BlockSpec((SSD_CHUNK, SSD_INNER), tok(0)), pl.BlockSpec((SSD_CHUNK, SSD_INNER), tok(0)),
                     pl.BlockSpec((1, SSD_INNER), fix), pl.BlockSpec((1, SSD_INNER), fix)]
        args += [y_other, proj, dsk, nw]
    return pl.pallas_call(
        functools.partial(_ssd_kernel, seq_len=seq_len, reverse=reverse, gated=gate is not None),
        grid=(b, nc),
        in_specs=in_specs,
        out_specs=pl.BlockSpec((SSD_CHUNK, SSD_INNER), tok(0)),
        out_shape=jax.ShapeDtypeStruct((n, SSD_INNER), jnp.float32 if gate is None else jnp.bfloat16),
        scratch_shapes=[pltpu.VMEM((SSD_HEADS // 2, SSD_STATE, LANES), jnp.float32)],
        compiler_params=_cparams("parallel", "arbitrary"),
        name="ssd_bwd" if reverse else "ssd_fwd",
    )(*args)


def _bias_kernel(rb_ref, o_ref):
    h = pl.program_id(0)
    d = pl.program_id(1) - 2
    t = SEQ_ALIGN
    kk = lax.broadcasted_iota(jnp.int32, (t, t), 0)
    qq = lax.broadcasted_iota(jnp.int32, (t, t), 1)
    rel = d * t + kk - qq
    n = jnp.abs(rel)
    half = REL_BUCKETS // 2
    vneg = jnp.full((t, t), rb_ref[0, h], jnp.float32)
    vpos = jnp.full((t, t), rb_ref[half, h], jnp.float32)
    for j, start in enumerate(_BUCKET_STARTS, start=1):
        ge = n >= start
        vneg = jnp.where(ge, rb_ref[j, h], vneg)
        vpos = jnp.where(ge, rb_ref[half + j, h], vpos)
    o_ref[0, 0] = jnp.where(rel > 0, vpos, vneg) * LOG2E


def bias_tiles(rel_bias):
    return pl.pallas_call(
        _bias_kernel,
        grid=(ATT_HEADS, 5),
        in_specs=[pl.BlockSpec(memory_space=pltpu.SMEM)],
        out_specs=pl.BlockSpec((1, 1, SEQ_ALIGN, SEQ_ALIGN), lambda h, d: (h, d, 0, 0)),
        out_shape=jax.ShapeDtypeStruct((ATT_HEADS, 5, SEQ_ALIGN, SEQ_ALIGN), jnp.float32),
        compiler_params=_cparams("parallel", "parallel"),
        name="bias_tiles",
    )(rel_bias)


def _attn_kernel(q_ref, v_ref, k_ref, bias_ref, lam_ref, sub_ref, o_ref, s_ref, *, seq_len, lambda_init):
    t = SEQ_ALIGN
    group = ATT_CHUNK
    vd = ATT_V_DIM
    qb = pl.program_id(2)
    sub = lax.broadcasted_iota(jnp.int32, (vd, t), 0)
    qs = []
    for hh in range(ATT_HEADS_PER_STEP):
        q_t = q_ref[0, hh * vd:(hh + 1) * vd, :]
        zero = jnp.zeros_like(q_t)
        qs += [jnp.where(sub < ATT_HEAD_DIM, q_t, zero), jnp.where(sub >= ATT_HEAD_DIM, q_t, zero)]
    n_maps = len(qs)
    n_full = seq_len // t
    tail = seq_len - n_full * t
    tail_rows = -(-tail // LANES) * LANES

    def fold_max(x):
        return jnp.max(x.reshape(x.shape[0] // 8, 8, t), axis=0)

    def score_block(kb, rows, mx):
        r0 = kb * t
        if not isinstance(kb, int):
            r0 = pl.multiple_of(r0, t)
        tile = jnp.clip(kb - qb, -2, 2) + 2
        out = []
        for mi in range(n_maps):
            hh = mi // 2
            kblk = k_ref[0, pl.ds(r0, rows), hh * vd:(hh + 1) * vd]
            s = jnp.dot(kblk, qs[mi], preferred_element_type=jnp.float32) + bias_ref[hh, tile, 0:rows, :]
            if rows != t:
                s = jnp.where(lax.broadcasted_iota(jnp.int32, (rows, 1), 0) < tail, s, -jnp.inf)
            s_ref[mi, pl.ds(r0, rows), :] = s
            out.append(jnp.maximum(mx[mi], fold_max(s)))
        return tuple(out)

    neg = jnp.full((8, t), -jnp.inf, jnp.float32)

    mx = (neg,) * n_maps
    for kb in range(n_full):
        mx = score_block(kb, t, mx)
    if tail:
        mx = score_block(n_full, tail_rows, mx)
    ms_ = [jnp.max(m, axis=0, keepdims=True) for m in mx]

    def prob_blocks(kbs, rows, accs):
        accs = list(accs)
        ones = jnp.ones((BF16_ROWS, rows * len(kbs)), jnp.bfloat16)
        for hh in range(ATT_HEADS_PER_STEP):
            vcat = jnp.concatenate([v_ref[kb][hh * vd:(hh + 1) * vd, 0:rows] for kb in kbs] , axis=1)
            vaug = jnp.concatenate([vcat, ones], axis=0)
            for mi in (2 * hh, 2 * hh + 1):
                ps = []
                for kb in kbs:
                    r0 = kb * t
                    if not isinstance(kb, int):
                        r0 = pl.multiple_of(r0, t)
                    ps.append(jnp.exp2((s_ref[mi, pl.ds(r0, rows), :] - ms_[mi]).astype(jnp.bfloat16)))
                pcat = jnp.concatenate(ps, axis=0) if len(ps) > 1 else ps[0]
                accs[mi] = accs[mi] + jnp.dot(vaug, pcat, preferred_element_type=jnp.float32)
        return tuple(accs)

    zacc = jnp.zeros((vd + BF16_ROWS, t), jnp.float32)
    accs = (zacc,) * n_maps
    for k0 in range(0, n_full, group):
        accs = prob_blocks(list(range(k0, min(k0 + group, n_full))), t, accs)
    if tail:
        accs = prob_blocks([n_full], tail_rows, accs)
    lv = lam_ref[...]
    lam = (jnp.exp(jnp.sum(lv[0:1] * lv[1:2], axis=-1, keepdims=True))
           - jnp.exp(jnp.sum(lv[2:3] * lv[3:4], axis=-1, keepdims=True)) + lambda_init)
    for hh in range(ATT_HEADS_PER_STEP):
        a1, a2 = accs[2 * hh], accs[2 * hh + 1]
        o = a1[0:vd] * (1.0 / a1[vd:vd + 1]) - lam * (a2[0:vd] * (1.0 / a2[vd:vd + 1]))
        ms = jnp.mean(o * o, axis=0, keepdims=True)
        o = o * lax.rsqrt(ms + LN_EPS) * sub_ref[...] * (1.0 - lambda_init)
        o_ref[:, hh * vd:(hh + 1) * vd] = o.T.astype(o_ref.dtype)


def diff_attention(qv, k3, bias5, lam4, subw, b, lp, seq_len, lambda_init):
    nq = lp // SEQ_ALIGN
    hs = ATT_HEADS_PER_STEP
    return pl.pallas_call(
        functools.partial(_attn_kernel, seq_len=seq_len, lambda_init=lambda_init),
        grid=(b, ATT_HEADS // hs, nq),
        in_specs=[
            pl.BlockSpec((1, hs * ATT_V_DIM, SEQ_ALIGN), lambda i, h, q: (i * nq + q, h, 0)),
            pl.BlockSpec((nq, hs * ATT_V_DIM, SEQ_ALIGN), lambda i, h, q: (i, ATT_HEADS // hs + h, 0)),
            pl.BlockSpec((1, lp, hs * ATT_V_DIM), lambda i, h, q: (i, 0, h)),
            pl.BlockSpec((hs, 5, SEQ_ALIGN, SEQ_ALIGN), lambda i, h, q: (h, 0, 0, 0)),
            pl.BlockSpec((8, LANES), lambda i, h, q: (0, 0)),
            pl.BlockSpec((ATT_V_DIM, SEQ_ALIGN), lambda i, h, q: (0, 0)),
        ],
        out_specs=pl.BlockSpec((SEQ_ALIGN, hs * ATT_V_DIM), lambda i, h, q: (i * nq + q, h)),
        out_shape=jax.ShapeDtypeStruct((b * lp, ATT_INNER), jnp.bfloat16),
        scratch_shapes=[pltpu.VMEM((2 * hs, lp, SEQ_ALIGN), jnp.float32)],
        compiler_params=_cparams("parallel", "parallel", "arbitrary"),
        name="diff_attn",
    )(qv, qv, k3, bias5, lam4, subw)


def _layer_norm_rows(t, g_ref, b_ref, tile_row0, lp, seq_len):
    rows = t.shape[0]
    mu = jnp.mean(t, axis=-1, keepdims=True)
    tc = t - mu
    var = jnp.mean(tc * tc, axis=-1, keepdims=True)
    y = tc * lax.rsqrt(var + LN_EPS) * g_ref[...] + b_ref[...]
    pos = tile_row0 % lp + lax.broadcasted_iota(jnp.int32, (rows, 1), 0)
    valid = pos < seq_len
    return jnp.where(valid, y, 0.0), valid


def _norm_route(t, g_ref, b_ref, wr_ref, tile_row0, lp, seq_len, h_ref, hb_ref, aff_ref, afft_ref):
    y, valid = _layer_norm_rows(t, g_ref, b_ref, tile_row0, lp, seq_len)
    h_ref[...] = y
    yb = y.astype(jnp.bfloat16)
    hb_ref[...] = yb
    if aff_ref is not None:
        logits = jnp.dot(yb, wr_ref[...], preferred_element_type=jnp.float32)
        lane = lax.broadcasted_iota(jnp.int32, logits.shape, 1)
        is_e = lane < N_EXPERTS
        logits = jnp.where(is_e, logits, -jnp.inf)
        e = jnp.exp(logits - jnp.max(logits, axis=-1, keepdims=True))
        aff = e / jnp.sum(e, axis=-1, keepdims=True)
        aff = jnp.where(valid, aff, jnp.where(is_e, -1.0, 0.0))
        aff_ref[...] = aff
        afft_ref[0] = aff.T[0:N_EXPERTS, :]


def _outproj_kernel(ys_ref, ya_ref, w_ref, h_ref, g_ref, b_ref, wr_ref, ho_ref, hb_ref, aff_ref, afft_ref, *, lp, seq_len):
    mix = (jnp.dot(ys_ref[...], w_ref[0:SSD_INNER, :], preferred_element_type=jnp.float32)
           + jnp.dot(ya_ref[...], w_ref[SSD_INNER:, :], preferred_element_type=jnp.float32))
    t = ALPHA * h_ref[...] + mix
    _norm_route(t, g_ref, b_ref, wr_ref, pl.program_id(0) * h_ref.shape[0], lp, seq_len, ho_ref, hb_ref, aff_ref, afft_ref)


def _poolproj_kernel(d_ref, w_ref, pb_ref, ps_ref, h_ref, g_ref, b_ref, wr_ref, ho_ref, hb_ref, aff_ref, afft_ref, *, lp, seq_len):
    outs = []
    for gi in range(len(POOL_WINDOWS)):
        outs.append(jnp.dot(d_ref[:, gi * POOL_GROUP:(gi + 1) * POOL_GROUP], w_ref[gi], preferred_element_type=jnp.float32))
    mix = (jnp.concatenate(outs, axis=-1) + pb_ref[...]) * ps_ref[...]
    t = ALPHA * h_ref[...] + mix
    _norm_route(t, g_ref, b_ref, wr_ref, pl.program_id(0) * h_ref.shape[0], lp, seq_len, ho_ref, hb_ref, aff_ref, afft_ref)


def _mix_outputs(n, tm):
    row = lambda i: (i, 0)
    specs = [pl.BlockSpec((tm, D_MODEL), row), pl.BlockSpec((tm, D_MODEL), row), pl.BlockSpec((tm, LANES), row),
             pl.BlockSpec((1, N_EXPERTS, tm), lambda i: (i, 0, 0))]
    shapes = [jax.ShapeDtypeStruct((n, D_MODEL), jnp.float32), jax.ShapeDtypeStruct((n, D_MODEL), jnp.bfloat16),
              jax.ShapeDtypeStruct((n, LANES), jnp.float32), jax.ShapeDtypeStruct((n // tm, N_EXPERTS, tm), jnp.float32)]
    return specs, shapes


def out_proj_norm_route(y_ssd, y_att, w_out, h, g, bb, w_r, lp, seq_len):
    n = h.shape[0]
    tm = TOK_TILE
    row = lambda i: (i, 0)
    fix = lambda i: (0, 0)
    specs, shapes = _mix_outputs(n, tm)
    return pl.pallas_call(
        functools.partial(_outproj_kernel, lp=lp, seq_len=seq_len),
        grid=(n // tm,),
        in_specs=[pl.BlockSpec((tm, SSD_INNER), row), pl.BlockSpec((tm, ATT_INNER), row),
                  pl.BlockSpec((D_MODEL, D_MODEL), fix), pl.BlockSpec((tm, D_MODEL), row),
                  pl.BlockSpec((1, D_MODEL), fix), pl.BlockSpec((1, D_MODEL), fix), pl.BlockSpec((D_MODEL, LANES), fix)],
        out_specs=specs, out_shape=shapes,
        compiler_params=_cparams("parallel"),
        name="outproj_ln_route",
    )(y_ssd, y_att, w_out, h, g, bb, w_r)


def pool_proj_norm_route(d, pool_w, pool_b, pool_scale, h, g, bb, w_r, lp, seq_len):
    n = h.shape[0]
    tm = TOK_TILE
    row = lambda i: (i, 0)
    fix = lambda i: (0, 0)
    specs, shapes = _mix_outputs(n, tm)
    return pl.pallas_call(
        functools.partial(_poolproj_kernel, lp=lp, seq_len=seq_len),
        grid=(n // tm,),
        in_specs=[pl.BlockSpec((tm, D_MODEL), row),
                  pl.BlockSpec((len(POOL_WINDOWS), POOL_GROUP, POOL_GROUP), lambda i: (0, 0, 0)),
                  pl.BlockSpec((1, D_MODEL), fix), pl.BlockSpec((1, D_MODEL), fix), pl.BlockSpec((tm, D_MODEL), row),
                  pl.BlockSpec((1, D_MODEL), fix), pl.BlockSpec((1, D_MODEL), fix), pl.BlockSpec((D_MODEL, LANES), fix)],
        out_specs=specs, out_shape=shapes,
        compiler_params=_cparams("parallel"),
        name="poolproj_ln_route",
    )(d, pool_w, pool_b, pool_scale, h, g, bb, w_r)


def _pool_kernel(x_ref, o_ref, s0_ref, s1_ref, *, seq_len):
    lp = x_ref.shape[1]
    halo = 16
    gi = pl.program_id(1) // (POOL_GROUP // SLAB)
    zeros = jnp.zeros((halo, SLAB), jnp.float32)
    for ref in (s0_ref, s1_ref):
        ref[0:halo, :] = zeros
        ref[lp + halo:lp + 2 * halo, :] = zeros
    s0_ref[halo:lp + halo, :] = x_ref[0]
    t = lax.broadcasted_iota(jnp.int32, (lp, 1), 0)
    for wi, w in enumerate(POOL_WINDOWS):
        @pl.when(gi == wi)
        def _(w=w):
            src, dst = s0_ref, s1_ref
            width = 1
            while width < w:
                dst[halo:lp + halo, :] = src[halo:lp + halo, :] + src[halo + width:lp + halo + width, :]
                dst[0:halo, :] = src[0:halo, :] + src[width:halo + width, :]
                src, dst = dst, src
                width *= 2
            win = src[halo - w // 2:lp + halo - w // 2, :]
            cnt = jnp.minimum(t + w // 2 - 1, seq_len - 1) - jnp.maximum(t - w // 2, 0) + 1
            d = win / cnt.astype(jnp.float32) - x_ref[0]
            o_ref[0] = jnp.where(t < seq_len, d, 0.0).astype(o_ref.dtype)


def pool_diff(h3, seq_len):
    b, lp, _ = h3.shape
    return pl.pallas_call(
        functools.partial(_pool_kernel, seq_len=seq_len),
        grid=(b, D_MODEL // SLAB),
        in_specs=[pl.BlockSpec((1, lp, SLAB), lambda i, j: (i, 0, j))],
        out_specs=pl.BlockSpec((1, lp, SLAB), lambda i, j: (i, 0, j)),
        out_shape=jax.ShapeDtypeStruct((b, lp, D_MODEL), jnp.bfloat16),
        scratch_shapes=[pltpu.VMEM((lp + 32, SLAB), jnp.float32)] * 2,
        compiler_params=_cparams("parallel", "parallel"),
        name="pool_diff",
    )(h3)


def _route_kernel(aff_ref, pos_ref, start_ref, *, cap):
    nt = aff_ref.shape[0]
    t = TOK_TILE

    def bits_of(v):
        return lax.bitcast_convert_type(v, jnp.int32)

    def count(mask):
        return jnp.sum(jnp.sum(jnp.where(mask, 1.0, 0.0), axis=0), axis=1, keepdims=True)

    def search(i, thr):
        cand = thr | jnp.left_shift(jnp.int32(1), 30 - i)
        cnt = count(bits_of(aff_ref[...]) >= cand[None])
        return jnp.where(cnt >= cap, cand, thr)

    thr = lax.fori_loop(0, 31, search, jnp.zeros((N_EXPERTS, 1), jnp.int32))
    need = cap - count(bits_of(aff_ref[...]) > thr[None])

    before = (lax.broadcasted_iota(jnp.int32, (t, t), 0) < lax.broadcasted_iota(jnp.int32, (t, t), 1)).astype(jnp.bfloat16)
    ones8 = jnp.ones((8, t), jnp.bfloat16)
    zpad = jnp.zeros((LANES - N_EXPERTS, t), jnp.float32)

    def tile(ti, carry):
        c_tie, c_sel, c_lane = carry
        b = bits_of(aff_ref[ti])
        tie = jnp.where(b == thr, 1.0, 0.0)
        rank = jnp.dot(tie.astype(jnp.bfloat16), before, preferred_element_type=jnp.float32) + c_tie
        sel = jnp.where(b > thr, 1.0, jnp.where(rank < need, tie, 0.0))
        slot = jnp.dot(sel.astype(jnp.bfloat16), before, preferred_element_type=jnp.float32) + c_sel
        pos_ref[ti] = jnp.where(sel > 0.0, slot, -1.0).astype(jnp.int32)
        start_ref[ti] = c_lane.astype(jnp.int32)
        selp = jnp.concatenate([sel, zpad], axis=0).astype(jnp.bfloat16)
        cnt_lane = lax.dot_general(ones8, selp, (((1,), (1,)), ((), ())), preferred_element_type=jnp.float32)
        return (c_tie + jnp.sum(tie, axis=1, keepdims=True), c_sel + jnp.sum(sel, axis=1, keepdims=True),
                c_lane + cnt_lane[0:1])

    zc = jnp.zeros((N_EXPERTS, 1), jnp.float32)
    _, _, c_lane = lax.fori_loop(0, nt, tile, (zc, zc, jnp.zeros((1, LANES), jnp.float32)))
    start_ref[nt] = c_lane.astype(jnp.int32)


def route(afft, cap):
    nt = afft.shape[0]
    return pl.pallas_call(
        functools.partial(_route_kernel, cap=cap),
        out_shape=[jax.ShapeDtypeStruct((nt, N_EXPERTS, TOK_TILE), jnp.int32),
                   jax.ShapeDtypeStruct((nt + 1, 1, LANES), jnp.int32)],
        compiler_params=pltpu.CompilerParams(vmem_limit_bytes=VMEM_LIMIT),
        name="route",
    )(afft)


HALF_EXPERTS = N_EXPERTS // 2
_REGION_MAX = -(-(TOK_TILE + BF16_ROWS - 1) // DMA_ROWS) * DMA_ROWS
HALF_ROWS = -(-(HALF_EXPERTS * _REGION_MAX + BF16_ROWS) // (2 * SEQ_ALIGN)) * (2 * SEQ_ALIGN)
XE_COLS = D_MODEL + LANES


def _tile_plan(st_ref, ti, half):
    plan = []
    off = jnp.int32(0)
    for e in range(half * HALF_EXPERTS, (half + 1) * HALF_EXPERTS):
        s = st_ref[ti * N_EXPERTS + e]
        end = st_ref[(ti + 1) * N_EXPERTS + e]
        s16 = s - (s & (BF16_ROWS - 1))
        nch = jnp.where(end > s, lax.shift_right_logical(end - s16 + DMA_ROWS - 1, DMA_ROWS.bit_length() - 1), 0)
        plan.append((e, s16, nch, off, end))
        off = off + nch * DMA_ROWS
    return plan, off


def _stage_rows(pos_half, plan):
    sub = lax.broadcasted_iota(jnp.int32, (HALF_EXPERTS, 1), 0)
    delta = jnp.zeros((HALF_EXPERTS, 1), jnp.int32)
    for i, (e, s16, nch, off, end) in enumerate(plan):
        delta = jnp.where(sub == i, off - s16, delta)
    return jnp.where(pos_half >= 0, pos_half + delta, -1)


def _onehot_rows(srow, k0):
    k = k0 + lax.broadcasted_iota(jnp.int32, (SEQ_ALIGN, 1), 0)
    onehot = jnp.zeros((SEQ_ALIGN, srow.shape[1]), jnp.float32)
    for i in range(srow.shape[0]):
        onehot = jnp.where(srow[i:i + 1, :] == k, 1.0, onehot)
    return onehot.astype(jnp.bfloat16)


def _n_block_pairs(rows):
    return lax.shift_right_logical(rows + 2 * SEQ_ALIGN - 1, SEQ_ALIGN.bit_length())


def _dispatch_kernel(st_ref, x_ref, pos_ref, aff_ref, xe_ref, sx_ref, cx_ref, n_ref, sem_ref, *, cap, cap_pad):
    ti = pl.program_id(0)
    nt = pl.num_programs(0)

    def chunk_copy(half, e, s16, off, j):
        src = pl.multiple_of(off + j * DMA_ROWS, DMA_ROWS)
        dst = pl.multiple_of(s16 + j * DMA_ROWS, BF16_ROWS)
        return pltpu.make_async_copy(sx_ref.at[half, pl.ds(src, DMA_ROWS)], xe_ref.at[e, pl.ds(dst, DMA_ROWS)],
                                     sem_ref.at[half])

    def wait_chunks(half, count):
        def body(_, c):
            chunk_copy(half, 0, 0, 0, 0).wait()
            return c
        lax.fori_loop(0, count, body, 0)

    tail0 = (cap // DMA_ROWS) * DMA_ROWS
    n_tail = (cap_pad - tail0) // DMA_ROWS

    @pl.when(ti == 0)
    def _():
        cx_ref[...] = jnp.zeros(cx_ref.shape, cx_ref.dtype)
        sx_ref[0, 0:DMA_ROWS, :] = jnp.zeros((DMA_ROWS, XE_COLS), sx_ref.dtype)
        for e in range(N_EXPERTS):
            for j in range(n_tail):
                pltpu.make_async_copy(sx_ref.at[0, pl.ds(0, DMA_ROWS)],
                                      xe_ref.at[e, pl.ds(tail0 + j * DMA_ROWS, DMA_ROWS)], sem_ref.at[0]).start()
        wait_chunks(0, N_EXPERTS * n_tail)
        n_ref[0] = 0
        n_ref[1] = 0

    aff = aff_ref[...]
    a1, a2, a3 = _split3(aff)
    gate3 = (a1.astype(jnp.float32) + pltpu.roll(a2.astype(jnp.float32), N_EXPERTS, 1)
             + pltpu.roll(a3.astype(jnp.float32), 2 * N_EXPERTS, 1)).astype(jnp.bfloat16)
    xt = jnp.concatenate([x_ref[...], gate3], axis=1)

    for half in range(2):
        plan, k_tot = _tile_plan(st_ref, ti, half)
        srow = _stage_rows(pos_ref[0, half * HALF_EXPERTS:(half + 1) * HALF_EXPERTS, :], plan)
        wait_chunks(half, n_ref[half])

        def permute(kp, c, half=half, srow=srow):
            for u in range(2):
                k0 = pl.multiple_of((2 * kp + u) * SEQ_ALIGN, SEQ_ALIGN)
                sx_ref[half, pl.ds(k0, SEQ_ALIGN), :] = jnp.dot(_onehot_rows(srow, k0), xt,
                                                                preferred_element_type=jnp.float32).astype(sx_ref.dtype)
            return c

        lax.fori_loop(0, _n_block_pairs(k_tot), permute, 0)

        total = jnp.int32(0)
        for e, s16, nch, off, end in plan:
            @pl.when(nch > 0)
            def _(half=half, e=e, s16=s16, off=off, end=end):
                head = pl.ds(pl.multiple_of(off, BF16_ROWS), BF16_ROWS)
                sx_ref[half, head, :] = sx_ref[half, head, :] + cx_ref[e]
                part = end & (BF16_ROWS - 1)
                last = pl.ds(pl.multiple_of(off + (end - part) - s16, BF16_ROWS), BF16_ROWS)
                cx_ref[e] = jnp.where(part > 0, sx_ref[half, last, :], jnp.zeros((BF16_ROWS, XE_COLS), cx_ref.dtype))

            def issue(j, c, half=half, e=e, s16=s16, off=off):
                chunk_copy(half, e, s16, off, j).start()
                return c
            lax.fori_loop(0, nch, issue, 0)
            total = total + nch
        n_ref[half] = total

    @pl.when(ti == nt - 1)
    def _():
        wait_chunks(0, n_ref[0])
        wait_chunks(1, n_ref[1])


def dispatch(starts, hb, pos, aff, cap, cap_pad):
    n = hb.shape[0]
    nt = n // TOK_TILE
    grid_spec = pltpu.PrefetchScalarGridSpec(
        num_scalar_prefetch=1,
        grid=(nt,),
        in_specs=[pl.BlockSpec((TOK_TILE, D_MODEL), lambda i, st: (i, 0)),
                  pl.BlockSpec((1, N_EXPERTS, TOK_TILE), lambda i, st: (i, 0, 0)),
                  pl.BlockSpec((TOK_TILE, LANES), lambda i, st: (i, 0))],
        out_specs=pl.BlockSpec(memory_space=pl.ANY),
        scratch_shapes=[pltpu.VMEM((2, HALF_ROWS, XE_COLS), jnp.bfloat16),
                        pltpu.VMEM((N_EXPERTS, BF16_ROWS, XE_COLS), jnp.bfloat16),
                        pltpu.SMEM((2,), jnp.int32), pltpu.SemaphoreType.DMA((2,))],
    )
    return pl.pallas_call(
        functools.partial(_dispatch_kernel, cap=cap, cap_pad=cap_pad),
        grid_spec=grid_spec,
        out_shape=jax.ShapeDtypeStruct((N_EXPERTS, cap_pad, XE_COLS), jnp.bfloat16),
        compiler_params=_cparams("arbitrary"),
        name="moe_dispatch",
    )(starts, hb, pos, aff)


def _ffn_kernel(x_ref, w1_ref, w3_ref, w2_ref, o_ref, acc_ref, *, cap):
    e = pl.program_id(0)
    f = pl.program_id(2)
    @pl.when(f == 0)
    def _():
        acc_ref[...] = jnp.zeros(acc_ref.shape, jnp.float32)

    x = x_ref[0, :, 0:D_MODEL]
    a = jnp.dot(x, w1_ref[0, 0], preferred_element_type=jnp.float32)
    b = jnp.dot(x, w3_ref[0, 0], preferred_element_type=jnp.float32)
    hid = (a * _sigmoid(a) * b).astype(jnp.bfloat16)
    acc_ref[...] = acc_ref[...] + jnp.dot(hid, w2_ref[0, 0], preferred_element_type=jnp.float32)

    @pl.when(f == pl.num_programs(2) - 1)
    def _():
        rows = o_ref.shape[1]
        gl = x_ref[0, :, D_MODEL:XE_COLS].astype(jnp.float32)
        lane = lax.broadcasted_iota(jnp.int32, gl.shape, 1)
        mine = ((lane & (N_EXPERTS - 1)) == e) & (lane < 3 * N_EXPERTS)
        gate = jnp.sum(jnp.where(mine, gl, 0.0), axis=-1, keepdims=True)
        slot = pl.program_id(1) * rows + lax.broadcasted_iota(jnp.int32, (rows, 1), 0)
        o_ref[0] = jnp.where(slot < cap, acc_ref[...] * gate, 0.0).astype(o_ref.dtype)


def expert_ffn(xe, w1, w3, w2, layer, cap):
    e, cap_pad, _ = xe.shape
    return pl.pallas_call(
        functools.partial(_ffn_kernel, cap=cap),
        grid=(e, cap_pad // FFN_ROWS, EXPERT_FF // FFN_COLS),
        in_specs=[pl.BlockSpec((1, FFN_ROWS, XE_COLS), lambda i, m, f: (i, m, 0)),
                  pl.BlockSpec((1, 1, D_MODEL, FFN_COLS), lambda i, m, f: (layer, i, 0, f)),
                  pl.BlockSpec((1, 1, D_MODEL, FFN_COLS), lambda i, m, f: (layer, i, 0, f)),
                  pl.BlockSpec((1, 1, FFN_COLS, D_MODEL), lambda i, m, f: (layer, i, f, 0))],
        out_specs=pl.BlockSpec((1, FFN_ROWS, D_MODEL), lambda i, m, f: (i, m, 0)),
        out_shape=jax.ShapeDtypeStruct((e, cap_pad, D_MODEL), jnp.bfloat16),
        scratch_shapes=[pltpu.VMEM((FFN_ROWS, D_MODEL), jnp.float32)],
        compiler_params=_cparams("parallel", "parallel", "arbitrary"),
        name="expert_ffn",
    )(xe, w1, w3, w2)


def _combine_kernel(st_ref, ye_ref, pos_ref, h_ref, g_ref, b_ref, *rest, lp, seq_len, final):
    if final:
        out_ref, sy_ref, acc_ref, n_ref, sem_ref, ybuf_ref, osem_ref = rest
    else:
        ho_ref, hb_ref, sy_ref, acc_ref, n_ref, sem_ref = rest
    ti = pl.program_id(0)
    nt = pl.num_programs(0)
    t = TOK_TILE

    def chunk_copy(half, e, s16, off, j):
        src = pl.multiple_of(s16 + j * DMA_ROWS, BF16_ROWS)
        dst = pl.multiple_of(off + j * DMA_ROWS, DMA_ROWS)
        return pltpu.make_async_copy(ye_ref.at[e, pl.ds(src, DMA_ROWS)], sy_ref.at[half, pl.ds(dst, DMA_ROWS)],
                                     sem_ref.at[half])

    def fetch(half, tile):
        plan, _ = _tile_plan(st_ref, tile, half)
        total = jnp.int32(0)
        for e, s16, nch, off, end in plan:
            def issue(j, c, e=e, s16=s16, off=off):
                chunk_copy(half, e, s16, off, j).start()
                return c
            lax.fori_loop(0, nch, issue, 0)
            total = total + nch
        n_ref[half] = total

    @pl.when(ti == 0)
    def _():
        sy_ref[...] = jnp.zeros(sy_ref.shape, sy_ref.dtype)
        fetch(0, 0)
        fetch(1, 0)

    acc_ref[...] = jnp.zeros(acc_ref.shape, jnp.float32)
    for half in range(2):
        plan, k_tot = _tile_plan(st_ref, ti, half)
        srow = _stage_rows(pos_ref[0, half * HALF_EXPERTS:(half + 1) * HALF_EXPERTS, :], plan)

        def wait_one(_, c, half=half):
            chunk_copy(half, 0, 0, 0, 0).wait()
            return c
        lax.fori_loop(0, n_ref[half], wait_one, 0)

        def gather(kp, c, half=half, srow=srow):
            k0 = pl.multiple_of(kp * (2 * SEQ_ALIGN), 2 * SEQ_ALIGN)
            onehot = jnp.concatenate([_onehot_rows(srow, k0), _onehot_rows(srow, k0 + SEQ_ALIGN)], axis=0)
            acc_ref[...] = acc_ref[...] + lax.dot_general(
                onehot, sy_ref[half, pl.ds(k0, 2 * SEQ_ALIGN), :], (((0,), (0,)), ((), ())),
                preferred_element_type=jnp.float32)
            return c

        lax.fori_loop(0, _n_block_pairs(k_tot), gather, 0)

        @pl.when(ti + 1 < nt)
        def _(half=half):
            fetch(half, ti + 1)

    tsum = ALPHA * h_ref[...] + acc_ref[...]
    if not final:
        _norm_route(tsum, g_ref, b_ref, None, ti * t, lp, seq_len, ho_ref, hb_ref, None, None)
        return

    nts = lp // t

    def out_rows(tile):
        tok0 = lax.rem(tile, nts) * t
        first = jnp.where(tok0 == 0, N_META, 0)
        return first, jnp.clip(seq_len - tok0, 0, t) - first

    def out_copy(slot, bi, src, dst):
        return pltpu.make_async_copy(ybuf_ref.at[slot, pl.ds(src, OUT_ROWS)], out_ref.at[bi, pl.ds(dst, OUT_ROWS)],
                                     osem_ref.at[slot])

    def wait_out(slot, tile):
        def body(_, c):
            out_copy(slot, 0, 0, 0).wait()
            return c
        lax.fori_loop(0, out_rows(tile)[1] // OUT_ROWS, body, 0)

    slot = ti & 1

    @pl.when(ti >= 2)
    def _():
        wait_out(slot, ti - 2)

    y, _ = _layer_norm_rows(tsum, g_ref, b_ref, ti * t, lp, seq_len)
    ybuf_ref[slot] = y
    first, rows = out_rows(ti)
    bi = ti // nts
    tok0 = lax.rem(ti, nts) * t

    def send(j, c):
        src = pl.multiple_of(first + j * OUT_ROWS, OUT_ROWS)
        out_copy(slot, bi, src, pl.multiple_of(tok0 - N_META + src, OUT_ROWS)).start()
        return c
    lax.fori_loop(0, rows // OUT_ROWS, send, 0)

    @pl.when(ti == nt - 1)
    def _():
        wait_out(slot, ti)

        @pl.when(nt > 1)
        def _():
            wait_out(1 - slot, ti - 1)


def combine(starts, ye, pos, h, g, bb, lp, seq_len, final):
    n = h.shape[0]
    nt = n // TOK_TILE
    row = lambda i, st: (i, 0)
    fix = lambda i, st: (0, 0)
    scratch = [pltpu.VMEM((2, HALF_ROWS, D_MODEL), jnp.bfloat16), pltpu.VMEM((TOK_TILE, D_MODEL), jnp.float32),
               pltpu.SMEM((2,), jnp.int32), pltpu.SemaphoreType.DMA((2,))]
    if final:
        assert seq_len % OUT_ROWS == 0 and N_META % OUT_ROWS == 0
        out_specs = pl.BlockSpec(memory_space=pl.ANY)
        out_shape = jax.ShapeDtypeStruct((n // lp, seq_len - N_META, D_MODEL), jnp.float32)
        scratch += [pltpu.VMEM((2, TOK_TILE, D_MODEL), jnp.float32), pltpu.SemaphoreType.DMA((2,))]
    else:
        out_specs = [pl.BlockSpec((TOK_TILE, D_MODEL), row), pl.BlockSpec((TOK_TILE, D_MODEL), row)]
        out_shape = [jax.ShapeDtypeStruct((n, D_MODEL), jnp.float32), jax.ShapeDtypeStruct((n, D_MODEL), jnp.bfloat16)]
    grid_spec = pltpu.PrefetchScalarGridSpec(
        num_scalar_prefetch=1,
        grid=(nt,),
        in_specs=[pl.BlockSpec(memory_space=pl.ANY), pl.BlockSpec((1, N_EXPERTS, TOK_TILE), lambda i, st: (i, 0, 0)),
                  pl.BlockSpec((TOK_TILE, D_MODEL), row), pl.BlockSpec((1, D_MODEL), fix), pl.BlockSpec((1, D_MODEL), fix)],
        out_specs=out_specs,
        scratch_shapes=scratch,
    )
    return pl.pallas_call(
        functools.partial(_combine_kernel, lp=lp, seq_len=seq_len, final=final),
        grid_spec=grid_spec,
        out_shape=out_shape,
        compiler_params=_cparams("arbitrary"),
        name="moe_combine_out" if final else "moe_combine",
    )(starts, ye, pos, h, g, bb)


def moe_block(h, hb, aff, afft, w1, w3, w2, layer, g, bb, n_real, lp, seq_len, final=False):
    cap = max(1, CAPACITY_FACTOR * n_real // N_EXPERTS)
    cap_pad = -(-(cap + DMA_ROWS) // FFN_ROWS) * FFN_ROWS
    pos, start3 = route(afft, cap)
    starts = start3[:, 0, :N_EXPERTS].reshape(-1)
    xe = dispatch(starts, hb, pos, aff, cap, cap_pad)
    ye = expert_ffn(xe, w1, w3, w2, layer, cap)
    return combine(starts, ye, pos, h, g, bb, lp, seq_len, final)


def _prep_weights(meta, w_in, conv_w, conv_b, a_log_f, a_log_b, dt_bias_f, dt_bias_b, d_skip, ssd_norm_w,
                  lam_q1, lam_k1, lam_q2, lam_k2, subln_w, rel_bias, w_out, pool_w, pool_b, pool_scale,
                  ln1_g, ln1_b, ln2_g, ln2_b, w_router, w1, w3, w2):
    bf = jnp.bfloat16
    f32 = jnp.float32
    w = w_in[0]
    o = [0, SSD_INNER, SSD_INNER + SSD_CONV_CH, SSD_INNER + SSD_CONV_CH + 2 * SSD_HEADS]
    wz, wxbc, wdt = w[:, o[0]:o[1]], w[:, o[1]:o[2]], w[:, o[2]:o[3]]
    wq, wk, wv = (w[:, o[3] + i * ATT_QK:o[3] + (i + 1) * ATT_QK] for i in range(3))
    pad = jnp.zeros((D_MODEL, _PROJ_COLS - _COL_DT - 2 * SSD_HEADS), f32)
    w_tok = jnp.concatenate([wz, wxbc, wdt, pad], axis=1).astype(bf)
    w_qv = jnp.concatenate([wq * (ATT_HEAD_DIM ** -0.5 * LOG2E), wv], axis=1).T.astype(bf)
    lane_pad = lambda v: jnp.pad(v.astype(f32), (0, LANES - v.shape[0]))
    par = jnp.stack([lane_pad(jnp.concatenate([dt_bias_f[0], dt_bias_b[0]])),
                     lane_pad(jnp.concatenate([a_log_f[0], a_log_b[0]]))] + [jnp.zeros((LANES,), f32)] * 6)
    lam4 = jnp.stack([lane_pad(lam_q1[0]), lane_pad(lam_k1[0]), lane_pad(lam_q2[0]), lane_pad(lam_k2[0])]
                     + [jnp.zeros((LANES,), f32)] * 4)
    return dict(
        meta=meta, w_tok=w_tok, w_k=wk.astype(bf), w_qv=w_qv, conv_w=conv_w[0, :, 0, :], conv_b=conv_b[0][None], par=par,
        dsk=jnp.repeat(d_skip[0], SSD_HEAD_DIM)[None], nw=ssd_norm_w[0][None], lam4=lam4,
        subw=jnp.broadcast_to(subln_w[0][:, None], (ATT_V_DIM, SEQ_ALIGN)), rel_bias=rel_bias,
        w_out=w_out[0].astype(bf), pool_w=pool_w[0].astype(bf), pool_b=pool_b[0][None], pool_scale=pool_scale[0][None],
        ln1_g=ln1_g, ln1_b=ln1_b, ln2_g=ln2_g, ln2_b=ln2_b,
        w_r=jnp.pad(w_router, ((0, 0), (0, 0), (0, LANES - N_EXPERTS))).astype(bf),
        w1=w1.astype(bf), w3=w3.astype(bf), w2=w2.astype(bf))


def _trunk(x, p, bias5):
    b, s, _ = x.shape
    seq_len = s + N_META
    lp = -(-seq_len // SEQ_ALIGN) * SEQ_ALIGN
    n = b * lp
    n_real = b * seq_len
    meta = jnp.broadcast_to(p["meta"][None].astype(x.dtype), (b, N_META, D_MODEL))
    h = jnp.concatenate([meta, x, jnp.zeros((b, lp - seq_len, D_MODEL), x.dtype)], axis=1).reshape(n, D_MODEL)
    hb = h.astype(jnp.bfloat16)
    lambda_init = 0.8 - 0.6 * math.exp(-0.3 * 0)
    tm = next(c for c in (ROW_TILE, ROW_TILE // 2, SEQ_ALIGN) if n % c == 0)

    proj = mm_nn(hb, p["w_tok"], jnp.float32, tm, 512)
    k3 = mm_nn(hb, p["w_k"], jnp.bfloat16, tm, 512).reshape(b, lp, ATT_QK)
    qv = mm_nt_blocked(p["w_qv"], hb, jnp.bfloat16, tm, 512)
    proj3 = proj.reshape(b, lp, _PROJ_COLS)
    xbc = conv_silu(proj3, p["conv_w"], p["conv_b"]).reshape(n, SSD_CONV_CH)
    yf = ssd_scan(xbc, proj, p["par"], b, lp, seq_len, False)
    y_ssd = ssd_scan(xbc, proj, p["par"], b, lp, seq_len, True, gate=(yf, p["dsk"], p["nw"]))
    y_att = diff_attention(qv, k3, bias5, p["lam4"], p["subw"], b, lp, seq_len, lambda_init)
    h, hb, aff, afft = out_proj_norm_route(y_ssd, y_att, p["w_out"], h, p["ln1_g"][0][None], p["ln1_b"][0][None],
                                           p["w_r"][0], lp, seq_len)
    h, hb = moe_block(h, hb, aff, afft, p["w1"], p["w3"], p["w2"], 0, p["ln2_g"][0][None], p["ln2_b"][0][None],
                      n_real, lp, seq_len)
    d = pool_diff(h.reshape(b, lp, D_MODEL), seq_len).reshape(n, D_MODEL)
    h, hb, aff, afft = pool_proj_norm_route(d, p["pool_w"], p["pool_b"], p["pool_scale"], h, p["ln1_g"][1][None],
                                            p["ln1_b"][1][None], p["w_r"][1], lp, seq_len)
    return moe_block(h, hb, aff, afft, p["w1"], p["w3"], p["w2"], 1, p["ln2_g"][1][None], p["ln2_b"][1][None],
                     n_real, lp, seq_len, final=True)


def kernel(x_prompt, x_sample, meta, w_in, conv_w, conv_b, a_log_f, a_log_b, dt_bias_f, dt_bias_b, d_skip, ssd_norm_w, lam_q1, lam_k1, lam_q2, lam_k2, subln_w, rel_bias, w_out, pool_w, pool_b, pool_scale, ln1_g, ln1_b, ln2_g, ln2_b, w_router, w1, w3, w2):
    p = _prep_weights(meta, w_in, conv_w, conv_b, a_log_f, a_log_b, dt_bias_f, dt_bias_b, d_skip, ssd_norm_w,
                      lam_q1, lam_k1, lam_q2, lam_k2, subln_w, rel_bias, w_out, pool_w, pool_b, pool_scale,
                      ln1_g, ln1_b, ln2_g, ln2_b, w_router, w1, w3, w2)
    bias5 = bias_tiles(rel_bias)
    return (_trunk(x_prompt, p, bias5), _trunk(x_sample, p, bias5))
```

```python
import functools
import math

import jax
import jax.numpy as jnp
from jax import lax
from jax.experimental import pallas as pl
from jax.experimental.pallas import tpu as pltpu

D_MODEL = 2048
DEPTH = 2
N_META = 16
SSD_HEADS = 16
SSD_HEAD_DIM = 64
SSD_INNER = SSD_HEADS * SSD_HEAD_DIM
SSD_GROUPS = 4
SSD_STATE = 128
SSD_CONV = 5
SSD_CHUNK = 128
SSD_CONV_CH = SSD_INNER + 2 * SSD_GROUPS * SSD_STATE
ATT_HEADS = 8
ATT_HEAD_DIM = 64
ATT_V_DIM = 2 * ATT_HEAD_DIM
ATT_QK = ATT_HEADS * 2 * ATT_HEAD_DIM
ATT_INNER = ATT_HEADS * ATT_V_DIM
REL_BUCKETS = 32
POOL_WINDOWS = (2, 4, 8, 16)
POOL_GROUP = D_MODEL // len(POOL_WINDOWS)
N_EXPERTS = 16
EXPERT_FF = 2048
CAPACITY_FACTOR = 2
ALPHA = (2 * DEPTH) ** 0.25
LN_EPS = 1e-5

LANES = 128
BF16_ROWS = 16
SEQ_ALIGN = 256
TOK_TILE = 256
ROW_TILE = 1024
DMA_ROWS = 32
SLAB = 256
OUT_ROWS = 16
FFN_ROWS = 704
FFN_COLS = 512
VMEM_LIMIT = 56 * 1024 * 1024

_BUCKET_STARTS = (1, 2, 3, 4, 5, 6, 7, 8, 12, 16, 23, 32, 46, 64, 91)
_COL_Z, _COL_XBC, _COL_DT, _PROJ_COLS = 0, 1024, 3072, 3584
ATT_CHUNK = 4
ATT_HEADS_PER_STEP = 2
LOG2E = math.log2(math.e)


def _cparams(*sem):
    return pltpu.CompilerParams(dimension_semantics=sem, vmem_limit_bytes=VMEM_LIMIT)


def _sigmoid(x):
    return 1.0 / (1.0 + jnp.exp(-x))


def _softplus(x):
    return jnp.maximum(x, 0.0) + jnp.log(1.0 + jnp.exp(-jnp.abs(x)))


def _split3(x):
    p1 = x.astype(jnp.bfloat16)
    r1 = x - p1.astype(jnp.float32)
    p2 = r1.astype(jnp.bfloat16)
    p3 = (r1 - p2.astype(jnp.float32)).astype(jnp.bfloat16)
    return p1, p2, p3


def _mm_nn_kernel(x_ref, w_ref, o_ref):
    o_ref[...] = jnp.dot(x_ref[...], w_ref[...], preferred_element_type=jnp.float32).astype(o_ref.dtype)


def mm_nn(x, w, out_dtype, tm, tn):
    n, k = x.shape
    nn = w.shape[1]
    return pl.pallas_call(
        _mm_nn_kernel,
        grid=(n // tm, nn // tn),
        in_specs=[pl.BlockSpec((tm, k), lambda i, j: (i, 0)), pl.BlockSpec((k, tn), lambda i, j: (0, j))],
        out_specs=pl.BlockSpec((tm, tn), lambda i, j: (i, j)),
        out_shape=jax.ShapeDtypeStruct((n, nn), out_dtype),
        compiler_params=_cparams("parallel", "arbitrary"),
        name="mm_nn",
    )(x, w)


def _mm_nt_kernel(w_ref, x_ref, o_ref):
    r = lax.dot_general(w_ref[...], x_ref[...], (((1,), (1,)), ((), ())), preferred_element_type=jnp.float32)
    for s in range(o_ref.shape[0]):
        o_ref[s] = r[:, s * SEQ_ALIGN:(s + 1) * SEQ_ALIGN].astype(o_ref.dtype)


def mm_nt_blocked(wt, x, out_dtype, tm, tn):
    n, k = x.shape
    nn = wt.shape[0]
    per = tm // SEQ_ALIGN
    return pl.pallas_call(
        _mm_nt_kernel,
        grid=(n // tm, nn // tn),
        in_specs=[pl.BlockSpec((tn, k), lambda i, j: (j, 0)), pl.BlockSpec((tm, k), lambda i, j: (i, 0))],
        out_specs=pl.BlockSpec((per, tn, SEQ_ALIGN), lambda i, j: (i, j, 0)),
        out_shape=jax.ShapeDtypeStruct((n // SEQ_ALIGN, nn, SEQ_ALIGN), out_dtype),
        compiler_params=_cparams("parallel", "arbitrary"),
        name="mm_nt",
    )(wt, x)


def _conv_kernel(x_ref, w_ref, b_ref, o_ref, pad_ref, *, rows):
    lp = x_ref.shape[1]
    halo = 8
    pad_ref[0:halo, :] = jnp.zeros((halo, SLAB), jnp.float32)
    pad_ref[lp + halo:lp + 2 * halo, :] = jnp.zeros((halo, SLAB), jnp.float32)
    pad_ref[halo:lp + halo, :] = x_ref[0]
    for t0 in range(0, lp, rows):
        acc = jnp.broadcast_to(b_ref[...], (rows, SLAB))
        for k in range(SSD_CONV):
            off = t0 + halo - SSD_CONV // 2 + k
            acc = acc + pad_ref[off:off + rows, :] * w_ref[k:k + 1, :]
        o_ref[0, t0:t0 + rows, :] = acc * _sigmoid(acc)


def conv_silu(proj3, conv_w, conv_b):
    b, lp, _ = proj3.shape
    c0 = _COL_XBC // SLAB
    return pl.pallas_call(
        functools.partial(_conv_kernel, rows=SEQ_ALIGN),
        grid=(b, SSD_CONV_CH // SLAB),
        in_specs=[
            pl.BlockSpec((1, lp, SLAB), lambda i, j: (i, 0, c0 + j)),
            pl.BlockSpec((SSD_CONV, SLAB), lambda i, j: (0, j)),
            pl.BlockSpec((1, SLAB), lambda i, j: (0, j)),
        ],
        out_specs=pl.BlockSpec((1, lp, SLAB), lambda i, j: (i, 0, j)),
        out_shape=jax.ShapeDtypeStruct((b, lp, SSD_CONV_CH), jnp.float32),
        scratch_shapes=[pltpu.VMEM((lp + 16, SLAB), jnp.float32)],
        compiler_params=_cparams("parallel", "parallel"),
        name="conv_silu",
    )(proj3, conv_w, conv_b)


def _ssd_kernel(x_ref, b_ref, c_ref, dt_ref, par_ref, *rest, seq_len, reverse, gated):
    if gated:
        yo_ref, z_ref, dsk_ref, nw_ref, y_ref, h_ref = rest
    else:
        y_ref, h_ref = rest
    cs = SSD_CHUNK
    step = pl.program_id(1)
    chunk = (pl.num_programs(1) - 1 - step) if reverse else step
    lane0 = SSD_HEADS if reverse else 0

    @pl.when(step == 0)
    def _():
        h_ref[...] = jnp.zeros(h_ref.shape, jnp.float32)

    sub = lax.broadcasted_iota(jnp.int32, (cs, cs), 0)
    lan = lax.broadcasted_iota(jnp.int32, (cs, cs), 1)
    valid = (chunk * cs + sub[:, 0:1]) < seq_len
    dt = jnp.where(valid, _softplus(dt_ref[...] + par_ref[0:1, :]), 0.0)
    a = dt * -jnp.exp(par_ref[1:2, :])
    a_t = a.T[0:2 * SSD_HEADS, :]
    if reverse:
        incl, rest = (sub >= lan), (sub < lan)
    else:
        incl, rest = (sub <= lan), (sub > lan)
    tri = jnp.concatenate([incl.astype(jnp.bfloat16), rest.astype(jnp.bfloat16),
                           jnp.ones((cs, cs), jnp.bfloat16)], axis=1)
    pieces = jnp.concatenate(_split3(a_t), axis=0)
    sums = jnp.dot(pieces, tri, preferred_element_type=jnp.float32)
    nh2 = 2 * SSD_HEADS
    sums = sums[0:nh2] + sums[nh2:2 * nh2] + sums[2 * nh2:3 * nh2]
    cum_row, rest_row, tot_row = sums[:, 0:cs], sums[:, cs:2 * cs], sums[:, 2 * cs:3 * cs]
    cum_col = jnp.concatenate([cum_row, jnp.zeros((cs - nh2, cs), jnp.float32)], axis=0).T
    mask = (sub <= lan) if reverse else (sub >= lan)
    left = lan[0:1, :] < SSD_HEAD_DIM
    gated_pairs = []

    for g in range(SSD_GROUPS):
        bg = b_ref[:, g * SSD_STATE:(g + 1) * SSD_STATE]
        cg = c_ref[:, g * SSD_STATE:(g + 1) * SSD_STATE]
        cb = lax.dot_general(cg.astype(jnp.bfloat16), bg.astype(jnp.bfloat16), (((1,), (1,)), ((), ())),
                             preferred_element_type=jnp.float32)
        bg_t = bg.T
        for pp in range(2):
            p = 2 * g + pp
            r0, r1 = lane0 + 2 * p, lane0 + 2 * p + 1
            dtp = jnp.where(left, dt[:, r0:r0 + 1], dt[:, r1:r1 + 1])
            xs = (x_ref[:, p * LANES:(p + 1) * LANES] * dtp).astype(jnp.bfloat16)
            h_t = h_ref[p]
            rhs = jnp.concatenate([xs, h_t.astype(jnp.bfloat16)], axis=0)
            ys, ss = [], []
            for r in (r0, r1):
                cc = cum_col[:, r:r + 1]
                dec = jnp.exp(jnp.where(mask, cc - cum_row[r:r + 1, :], -jnp.inf))
                lhs = jnp.concatenate([(cb * dec).astype(jnp.bfloat16),
                                       (cg * jnp.exp(cc)).astype(jnp.bfloat16)], axis=1)
                ys.append(jnp.dot(lhs, rhs, preferred_element_type=jnp.float32))
                bw = (bg_t * jnp.exp(rest_row[r:r + 1, :])).astype(jnp.bfloat16)
                ss.append(jnp.dot(bw, xs, preferred_element_type=jnp.float32))
            lanes = slice(p * LANES, (p + 1) * LANES)
            y_pair = jnp.where(left, ys[0], ys[1])
            if gated:
                z = z_ref[:, lanes]
                gated_pairs.append((yo_ref[:, lanes] + y_pair + dsk_ref[:, lanes] * x_ref[:, lanes]) * (z * _sigmoid(z)))
            else:
                y_ref[:, lanes] = y_pair
            decay = jnp.where(left, jnp.exp(tot_row[r0:r0 + 1, :]), jnp.exp(tot_row[r1:r1 + 1, :]))
            h_ref[p] = decay * h_t + jnp.where(left, ss[0], ss[1])
        if gated:
            seg = jnp.concatenate(gated_pairs, axis=1)
            gated_pairs = []
            ms = jnp.mean(seg * seg, axis=-1, keepdims=True)
            cols = slice(2 * g * LANES, 2 * (g + 1) * LANES)
            y_ref[:, cols] = (seg * lax.rsqrt(ms + LN_EPS) * nw_ref[:, cols]).astype(y_ref.dtype)


def ssd_scan(xbc, proj, par, b, lp, seq_len, reverse, gate=None):
    n = xbc.shape[0]
    nc = lp // SSD_CHUNK
    cdt = _COL_DT // LANES

    def row(i, c):
        return i * nc + ((nc - 1 - c) if reverse else c)

    tok = lambda col: (lambda i, c: (row(i, c), col))
    fix = lambda i, c: (0, 0)
    in_specs = [
        pl.BlockSpec((SSD_CHUNK, SSD_INNER), tok(0)),
        pl.BlockSpec((SSD_CHUNK, SSD_GROUPS * SSD_STATE), tok(2)),
        pl.BlockSpec((SSD_CHUNK, SSD_GROUPS * SSD_STATE), tok(3)),
        pl.BlockSpec((SSD_CHUNK, LANES), tok(cdt)),
        pl.BlockSpec((8, LANES), fix),
    ]
    args = [xbc, xbc, xbc, proj, par]
    if gate is not None:
        y_other, dsk, nw = gate
        in_specs += [pl.BlockSpec((SSD_CHUNK, SSD_INNER), tok(0)), pl.BlockSpec((SSD_CHUNK, SSD_INNER), tok(0)),
                     pl.BlockSpec((1, SSD_INNER), fix), pl.BlockSpec((1, SSD_INNER), fix)]
        args += [y_other, proj, dsk, nw]
    return pl.pallas_call(
        functools.partial(_ssd_kernel, seq_len=seq_len, reverse=reverse, gated=gate is not None),
        grid=(b, nc),
        in_specs=in_specs,
        out_specs=pl.BlockSpec((SSD_CHUNK, SSD_INNER), tok(0)),
        out_shape=jax.ShapeDtypeStruct((n, SSD_INNER), jnp.float32 if gate is None else jnp.bfloat16),
        scratch_shapes=[pltpu.VMEM((SSD_HEADS // 2, SSD_STATE, LANES), jnp.float32)],
        compiler_params=_cparams("parallel", "arbitrary"),
        name="ssd_bwd" if reverse else "ssd_fwd",
    )(*args)


def _bias_kernel(rb_ref, o_ref):
    h = pl.program_id(0)
    d = pl.program_id(1) - 2
    t = SEQ_ALIGN
    kk = lax.broadcasted_iota(jnp.int32, (t, t), 0)
    qq = lax.broadcasted_iota(jnp.int32, (t, t), 1)
    rel = d * t + kk - qq
    n = jnp.abs(rel)
    half = REL_BUCKETS // 2
    vneg = jnp.full((t, t), rb_ref[0, h], jnp.float32)
    vpos = jnp.full((t, t), rb_ref[half, h], jnp.float32)
    for j, start in enumerate(_BUCKET_STARTS, start=1):
        ge = n >= start
        vneg = jnp.where(ge, rb_ref[j, h], vneg)
        vpos = jnp.where(ge, rb_ref[half + j, h], vpos)
    o_ref[0, 0] = jnp.where(rel > 0, vpos, vneg) * LOG2E


def bias_tiles(rel_bias):
    return pl.pallas_call(
        _bias_kernel,
        grid=(ATT_HEADS, 5),
        in_specs=[pl.BlockSpec(memory_space=pltpu.SMEM)],
        out_specs=pl.BlockSpec((1, 1, SEQ_ALIGN, SEQ_ALIGN), lambda h, d: (h, d, 0, 0)),
        out_shape=jax.ShapeDtypeStruct((ATT_HEADS, 5, SEQ_ALIGN, SEQ_ALIGN), jnp.float32),
        compiler_params=_cparams("parallel", "parallel"),
        name="bias_tiles",
    )(rel_bias)


def _attn_kernel(q_ref, v_ref, k_ref, bias_ref, lam_ref, sub_ref, o_ref, s_ref, *, seq_len, lambda_init):
    t = SEQ_ALIGN
    group = ATT_CHUNK
    vd = ATT_V_DIM
    qb = pl.program_id(2)
    sub = lax.broadcasted_iota(jnp.int32, (vd, t), 0)
    qs = []
    for hh in range(ATT_HEADS_PER_STEP):
        q_t = q_ref[0, hh * vd:(hh + 1) * vd, :]
        zero = jnp.zeros_like(q_t)
        qs += [jnp.where(sub < ATT_HEAD_DIM, q_t, zero), jnp.where(sub >= ATT_HEAD_DIM, q_t, zero)]
    n_maps = len(qs)
    n_full = seq_len // t
    tail = seq_len - n_full * t
    tail_rows = -(-tail // LANES) * LANES

    def fold_max(x):
        return jnp.max(x.reshape(x.shape[0] // 8, 8, t), axis=0)

    def score_block(kb, rows, mx):
        r0 = kb * t
        if not isinstance(kb, int):
            r0 = pl.multiple_of(r0, t)
        tile = jnp.clip(kb - qb, -2, 2) + 2
        out = []
        for mi in range(n_maps):
            hh = mi // 2
            kblk = k_ref[0, pl.ds(r0, rows), hh * vd:(hh + 1) * vd]
            s = jnp.dot(kblk, qs[mi], preferred_element_type=jnp.float32) + bias_ref[hh, tile, 0:rows, :]
            if rows != t:
                s = jnp.where(lax.broadcasted_iota(jnp.int32, (rows, 1), 0) < tail, s, -jnp.inf)
            s_ref[mi, pl.ds(r0, rows), :] = s
            out.append(jnp.maximum(mx[mi], fold_max(s)))
        return tuple(out)

    neg = jnp.full((8, t), -jnp.inf, jnp.float32)

    mx = (neg,) * n_maps
    for kb in range(n_full):
        mx = score_block(kb, t, mx)
    if tail:
        mx = score_block(n_full, tail_rows, mx)
    ms_ = [jnp.max(m, axis=0, keepdims=True) for m in mx]

    def prob_blocks(kbs, rows, accs):
        accs = list(accs)
        ones = jnp.ones((BF16_ROWS, rows * len(kbs)), jnp.bfloat16)
        for hh in range(ATT_HEADS_PER_STEP):
            vcat = jnp.concatenate([v_ref[kb][hh * vd:(hh + 1) * vd, 0:rows] for kb in kbs] , axis=1)
            vaug = jnp.concatenate([vcat, ones], axis=0)
            for mi in (2 * hh, 2 * hh + 1):
                ps = []
                for kb in kbs:
                    r0 = kb * t
                    if not isinstance(kb, int):
                        r0 = pl.multiple_of(r0, t)
                    ps.append(jnp.exp2((s_ref[mi, pl.ds(r0, rows), :] - ms_[mi]).astype(jnp.bfloat16)))
                pcat = jnp.concatenate(ps, axis=0) if len(ps) > 1 else ps[0]
                accs[mi] = accs[mi] + jnp.dot(vaug, pcat, preferred_element_type=jnp.float32)
        return tuple(accs)

    zacc = jnp.zeros((vd + BF16_ROWS, t), jnp.float32)
    accs = (zacc,) * n_maps
    for k0 in range(0, n_full, group):
        accs = prob_blocks(list(range(k0, min(k0 + group, n_full))), t, accs)
    if tail:
        accs = prob_blocks([n_full], tail_rows, accs)
    lv = lam_ref[...]
    lam = (jnp.exp(jnp.sum(lv[0:1] * lv[1:2], axis=-1, keepdims=True))
           - jnp.exp(jnp.sum(lv[2:3] * lv[3:4], axis=-1, keepdims=True)) + lambda_init)
    for hh in range(ATT_HEADS_PER_STEP):
        a1, a2 = accs[2 * hh], accs[2 * hh + 1]
        o = a1[0:vd] * (1.0 / a1[vd:vd + 1]) - lam * (a2[0:vd] * (1.0 / a2[vd:vd + 1]))
        ms = jnp.mean(o * o, axis=0, keepdims=True)
        o = o * lax.rsqrt(ms + LN_EPS) * sub_ref[...] * (1.0 - lambda_init)
        o_ref[:, hh * vd:(hh + 1) * vd] = o.T.astype(o_ref.dtype)


def diff_attention(qv, k3, bias5, lam4, subw, b, lp, seq_len, lambda_init):
    nq = lp // SEQ_ALIGN
    hs = ATT_HEADS_PER_STEP
    return pl.pallas_call(
        functools.partial(_attn_kernel, seq_len=seq_len, lambda_init=lambda_init),
        grid=(b, ATT_HEADS // hs, nq),
        in_specs=[
            pl.BlockSpec((1, hs * ATT_V_DIM, SEQ_ALIGN), lambda i, h, q: (i * nq + q, h, 0)),
            pl.BlockSpec((nq, hs * ATT_V_DIM, SEQ_ALIGN), lambda i, h, q: (i, ATT_HEADS // hs + h, 0)),
            pl.BlockSpec((1, lp, hs * ATT_V_DIM), lambda i, h, q: (i, 0, h)),
            pl.BlockSpec((hs, 5, SEQ_ALIGN, SEQ_ALIGN), lambda i, h, q: (h, 0, 0, 0)),
            pl.BlockSpec((8, LANES), lambda i, h, q: (0, 0)),
            pl.BlockSpec((ATT_V_DIM, SEQ_ALIGN), lambda i, h, q: (0, 0)),
        ],
        out_specs=pl.BlockSpec((SEQ_ALIGN, hs * ATT_V_DIM), lambda i, h, q: (i * nq + q, h)),
        out_shape=jax.ShapeDtypeStruct((b * lp, ATT_INNER), jnp.bfloat16),
        scratch_shapes=[pltpu.VMEM((2 * hs, lp, SEQ_ALIGN), jnp.float32)],
        compiler_params=_cparams("parallel", "parallel", "arbitrary"),
        name="diff_attn",
    )(qv, qv, k3, bias5, lam4, subw)


def _layer_norm_rows(t, g_ref, b_ref, tile_row0, lp, seq_len):
    rows = t.shape[0]
    mu = jnp.mean(t, axis=-1, keepdims=True)
    tc = t - mu
    var = jnp.mean(tc * tc, axis=-1, keepdims=True)
    y = tc * lax.rsqrt(var + LN_EPS) * g_ref[...] + b_ref[...]
    pos = tile_row0 % lp + lax.broadcasted_iota(jnp.int32, (rows, 1), 0)
    valid = pos < seq_len
    return jnp.where(valid, y, 0.0), valid


def _norm_route(t, g_ref, b_ref, wr_ref, tile_row0, lp, seq_len, h_ref, hb_ref, aff_ref, afft_ref, rs=slice(None)):
    y, valid = _layer_norm_rows(t, g_ref, b_ref, tile_row0, lp, seq_len)
    h_ref[rs, :] = y
    yb = y.astype(jnp.bfloat16)
    hb_ref[rs, :] = yb
    if aff_ref is not None:
        logits = jnp.dot(yb, wr_ref[...], preferred_element_type=jnp.float32)
        lane = lax.broadcasted_iota(jnp.int32, logits.shape, 1)
        is_e = lane < N_EXPERTS
        logits = jnp.where(is_e, logits, -jnp.inf)
        e = jnp.exp(logits - jnp.max(logits, axis=-1, keepdims=True))
        aff = e / jnp.sum(e, axis=-1, keepdims=True)
        aff = jnp.where(valid, aff, jnp.where(is_e, -1.0, 0.0))
        aff_ref[rs, :] = aff
        afft_ref[0, :, rs] = aff.T[0:N_EXPERTS, :]


def _half_tiles(rows):
    return [slice(i * (rows // 2), (i + 1) * (rows // 2)) for i in range(2)]


def _outproj_kernel(ys_ref, ya_ref, w_ref, h_ref, g_ref, b_ref, wr_ref, ho_ref, hb_ref, aff_ref, afft_ref, *, lp, seq_len):
    rows = h_ref.shape[0]
    for rs in _half_tiles(rows):
        mix = (jnp.dot(ys_ref[rs, :], w_ref[0:SSD_INNER, :], preferred_element_type=jnp.float32)
               + jnp.dot(ya_ref[rs, :], w_ref[SSD_INNER:, :], preferred_element_type=jnp.float32))
        t = ALPHA * h_ref[rs, :] + mix
        _norm_route(t, g_ref, b_ref, wr_ref, pl.program_id(0) * rows + rs.start, lp, seq_len, ho_ref, hb_ref, aff_ref,
                    afft_ref, rs)


def _poolproj_kernel(d_ref, w_ref, pb_ref, ps_ref, h_ref, g_ref, b_ref, wr_ref, ho_ref, hb_ref, aff_ref, afft_ref, *, lp, seq_len):
    rows = h_ref.shape[0]
    for rs in _half_tiles(rows):
        outs = []
        for gi in range(len(POOL_WINDOWS)):
            outs.append(jnp.dot(d_ref[rs, gi * POOL_GROUP:(gi + 1) * POOL_GROUP], w_ref[gi], preferred_element_type=jnp.float32))
        mix = (jnp.concatenate(outs, axis=-1) + pb_ref[...]) * ps_ref[...]
        t = ALPHA * h_ref[rs, :] + mix
        _norm_route(t, g_ref, b_ref, wr_ref, pl.program_id(0) * rows + rs.start, lp, seq_len, ho_ref, hb_ref, aff_ref,
                    afft_ref, rs)


def _mix_outputs(n, tm):
    row = lambda i: (i, 0)
    specs = [pl.BlockSpec((tm, D_MODEL), row), pl.BlockSpec((tm, D_MODEL), row), pl.BlockSpec((tm, LANES), row),
             pl.BlockSpec((1, N_EXPERTS, tm), lambda i: (i, 0, 0))]
    shapes = [jax.ShapeDtypeStruct((n, D_MODEL), jnp.float32), jax.ShapeDtypeStruct((n, D_MODEL), jnp.bfloat16),
              jax.ShapeDtypeStruct((n, LANES), jnp.float32), jax.ShapeDtypeStruct((n // tm, N_EXPERTS, tm), jnp.float32)]
    return specs, shapes


def out_proj_norm_route(y_ssd, y_att, w_out, h, g, bb, w_r, lp, seq_len):
    n = h.shape[0]
    tm = TOK_TILE
    row = lambda i: (i, 0)
    fix = lambda i: (0, 0)
    specs, shapes = _mix_outputs(n, tm)
    return pl.pallas_call(
        functools.partial(_outproj_kernel, lp=lp, seq_len=seq_len),
        grid=(n // tm,),
        in_specs=[pl.BlockSpec((tm, SSD_INNER), row), pl.BlockSpec((tm, ATT_INNER), row),
                  pl.BlockSpec((D_MODEL, D_MODEL), fix), pl.BlockSpec((tm, D_MODEL), row),
                  pl.BlockSpec((1, D_MODEL), fix), pl.BlockSpec((1, D_MODEL), fix), pl.BlockSpec((D_MODEL, LANES), fix)],
        out_specs=specs, out_shape=shapes,
        compiler_params=_cparams("parallel"),
        name="outproj_ln_route",
    )(y_ssd, y_att, w_out, h, g, bb, w_r)


def pool_proj_norm_route(d, pool_w, pool_b, pool_scale, h, g, bb, w_r, lp, seq_len):
    n = h.shape[0]
    tm = TOK_TILE
    row = lambda i: (i, 0)
    fix = lambda i: (0, 0)
    specs, shapes = _mix_outputs(n, tm)
    return pl.pallas_call(
        functools.partial(_poolproj_kernel, lp=lp, seq_len=seq_len),
        grid=(n // tm,),
        in_specs=[pl.BlockSpec((tm, D_MODEL), row),
                  pl.BlockSpec((len(POOL_WINDOWS), POOL_GROUP, POOL_GROUP), lambda i: (0, 0, 0)),
                  pl.BlockSpec((1, D_MODEL), fix), pl.BlockSpec((1, D_MODEL), fix), pl.BlockSpec((tm, D_MODEL), row),
                  pl.BlockSpec((1, D_MODEL), fix), pl.BlockSpec((1, D_MODEL), fix), pl.BlockSpec((D_MODEL, LANES), fix)],
        out_specs=specs, out_shape=shapes,
        compiler_params=_cparams("parallel"),
        name="poolproj_ln_route",
    )(d, pool_w, pool_b, pool_scale, h, g, bb, w_r)


def _pool_kernel(x_ref, o_ref, s0_ref, s1_ref, *, seq_len):
    lp = x_ref.shape[1]
    halo = 16
    gi = pl.program_id(1) // (POOL_GROUP // SLAB)
    zeros = jnp.zeros((halo, SLAB), jnp.float32)
    for ref in (s0_ref, s1_ref):
        ref[0:halo, :] = zeros
        ref[lp + halo:lp + 2 * halo, :] = zeros
    s0_ref[halo:lp + halo, :] = x_ref[0]
    t = lax.broadcasted_iota(jnp.int32, (lp, 1), 0)
    for wi, w in enumerate(POOL_WINDOWS):
        @pl.when(gi == wi)
        def _(w=w):
            src, dst = s0_ref, s1_ref
            width = 1
            while width < w:
                dst[halo:lp + halo, :] = src[halo:lp + halo, :] + src[halo + width:lp + halo + width, :]
                dst[0:halo, :] = src[0:halo, :] + src[width:halo + width, :]
                src, dst = dst, src
                width *= 2
            win = src[halo - w // 2:lp + halo - w // 2, :]
            cnt = jnp.minimum(t + w // 2 - 1, seq_len - 1) - jnp.maximum(t - w // 2, 0) + 1
            d = win / cnt.astype(jnp.float32) - x_ref[0]
            o_ref[0] = jnp.where(t < seq_len, d, 0.0).astype(o_ref.dtype)


def pool_diff(h3, seq_len):
    b, lp, _ = h3.shape
    return pl.pallas_call(
        functools.partial(_pool_kernel, seq_len=seq_len),
        grid=(b, D_MODEL // SLAB),
        in_specs=[pl.BlockSpec((1, lp, SLAB), lambda i, j: (i, 0, j))],
        out_specs=pl.BlockSpec((1, lp, SLAB), lambda i, j: (i, 0, j)),
        out_shape=jax.ShapeDtypeStruct((b, lp, D_MODEL), jnp.bfloat16),
        scratch_shapes=[pltpu.VMEM((lp + 32, SLAB), jnp.float32)] * 2,
        compiler_params=_cparams("parallel", "parallel"),
        name="pool_diff",
    )(h3)


def _route_kernel(aff_ref, pos_ref, start_ref, *, cap):
    nt = aff_ref.shape[0]
    t = TOK_TILE

    def bits_of(v):
        return lax.bitcast_convert_type(v, jnp.int32)

    def count(mask):
        return jnp.sum(jnp.sum(jnp.where(mask, 1.0, 0.0), axis=0), axis=1, keepdims=True)

    def search(i, thr):
        cand = thr | jnp.left_shift(jnp.int32(1), 30 - i)
        cnt = count(bits_of(aff_ref[...]) >= cand[None])
        return jnp.where(cnt >= cap, cand, thr)

    thr = lax.fori_loop(0, 31, search, jnp.zeros((N_EXPERTS, 1), jnp.int32))
    need = cap - count(bits_of(aff_ref[...]) > thr[None])

    before = (lax.broadcasted_iota(jnp.int32, (t, t), 0) < lax.broadcasted_iota(jnp.int32, (t, t), 1)).astype(jnp.bfloat16)
    ones8 = jnp.ones((8, t), jnp.bfloat16)
    zpad = jnp.zeros((LANES - N_EXPERTS, t), jnp.float32)

    def tile(ti, carry):
        c_tie, c_sel, c_lane = carry
        b = bits_of(aff_ref[ti])
        tie = jnp.where(b == thr, 1.0, 0.0)
        rank = jnp.dot(tie.astype(jnp.bfloat16), before, preferred_element_type=jnp.float32) + c_tie
        sel = jnp.where(b > thr, 1.0, jnp.where(rank < need, tie, 0.0))
        slot = jnp.dot(sel.astype(jnp.bfloat16), before, preferred_element_type=jnp.float32) + c_sel
        pos_ref[ti] = jnp.where(sel > 0.0, slot, -1.0).astype(jnp.int32)
        start_ref[ti] = c_lane.astype(jnp.int32)
        selp = jnp.concatenate([sel, zpad], axis=0).astype(jnp.bfloat16)
        cnt_lane = lax.dot_general(ones8, selp, (((1,), (1,)), ((), ())), preferred_element_type=jnp.float32)
        return (c_tie + jnp.sum(tie, axis=1, keepdims=True), c_sel + jnp.sum(sel, axis=1, keepdims=True),
                c_lane + cnt_lane[0:1])

    zc = jnp.zeros((N_EXPERTS, 1), jnp.float32)
    _, _, c_lane = lax.fori_loop(0, nt, tile, (zc, zc, jnp.zeros((1, LANES), jnp.float32)))
    start_ref[nt] = c_lane.astype(jnp.int32)


def route(afft, cap):
    nt = afft.shape[0]
    return pl.pallas_call(
        functools.partial(_route_kernel, cap=cap),
        out_shape=[jax.ShapeDtypeStruct((nt, N_EXPERTS, TOK_TILE), jnp.int32),
                   jax.ShapeDtypeStruct((nt + 1, 1, LANES), jnp.int32)],
        compiler_params=pltpu.CompilerParams(vmem_limit_bytes=VMEM_LIMIT),
        name="route",
    )(afft)


HALF_EXPERTS = N_EXPERTS // 2
_REGION_MAX = -(-(TOK_TILE + BF16_ROWS - 1) // DMA_ROWS) * DMA_ROWS
HALF_ROWS = -(-(HALF_EXPERTS * _REGION_MAX + BF16_ROWS) // (2 * SEQ_ALIGN)) * (2 * SEQ_ALIGN)
XE_COLS = D_MODEL + LANES


def _tile_plan(st_ref, ti, half):
    plan = []
    off = jnp.int32(0)
    for e in range(half * HALF_EXPERTS, (half + 1) * HALF_EXPERTS):
        s = st_ref[ti * N_EXPERTS + e]
        end = st_ref[(ti + 1) * N_EXPERTS + e]
        s16 = s - (s & (BF16_ROWS - 1))
        nch = jnp.where(end > s, lax.shift_right_logical(end - s16 + DMA_ROWS - 1, DMA_ROWS.bit_length() - 1), 0)
        plan.append((e, s16, nch, off, end))
        off = off + nch * DMA_ROWS
    return plan, off


def _stage_rows(pos_half, plan):
    sub = lax.broadcasted_iota(jnp.int32, (HALF_EXPERTS, 1), 0)
    delta = jnp.zeros((HALF_EXPERTS, 1), jnp.int32)
    for i, (e, s16, nch, off, end) in enumerate(plan):
        delta = jnp.where(sub == i, off - s16, delta)
    return jnp.where(pos_half >= 0, pos_half + delta, -1)


def _onehot_rows(srow, k0):
    k = k0 + lax.broadcasted_iota(jnp.int32, (SEQ_ALIGN, 1), 0)
    onehot = jnp.zeros((SEQ_ALIGN, srow.shape[1]), jnp.float32)
    for i in range(srow.shape[0]):
        onehot = jnp.where(srow[i:i + 1, :] == k, 1.0, onehot)
    return onehot.astype(jnp.bfloat16)


def _n_block_pairs(rows):
    return lax.shift_right_logical(rows + 2 * SEQ_ALIGN - 1, SEQ_ALIGN.bit_length())


def _dispatch_kernel(st_ref, x_ref, pos_ref, aff_ref, xe_ref, sx_ref, cx_ref, n_ref, sem_ref, *, cap, cap_pad):
    ti = pl.program_id(0)
    nt = pl.num_programs(0)

    def chunk_copy(half, e, s16, off, j):
        src = pl.multiple_of(off + j * DMA_ROWS, DMA_ROWS)
        dst = pl.multiple_of(s16 + j * DMA_ROWS, BF16_ROWS)
        return pltpu.make_async_copy(sx_ref.at[half, pl.ds(src, DMA_ROWS)], xe_ref.at[e, pl.ds(dst, DMA_ROWS)],
                                     sem_ref.at[half])

    def wait_chunks(half, count):
        def body(_, c):
            chunk_copy(half, 0, 0, 0, 0).wait()
            return c
        lax.fori_loop(0, count, body, 0)

    tail0 = (cap // DMA_ROWS) * DMA_ROWS
    n_tail = (cap_pad - tail0) // DMA_ROWS

    @pl.when(ti == 0)
    def _():
        cx_ref[...] = jnp.zeros(cx_ref.shape, cx_ref.dtype)
        sx_ref[0, 0:DMA_ROWS, :] = jnp.zeros((DMA_ROWS, XE_COLS), sx_ref.dtype)
        for e in range(N_EXPERTS):
            for j in range(n_tail):
                pltpu.make_async_copy(sx_ref.at[0, pl.ds(0, DMA_ROWS)],
                                      xe_ref.at[e, pl.ds(tail0 + j * DMA_ROWS, DMA_ROWS)], sem_ref.at[0]).start()
        wait_chunks(0, N_EXPERTS * n_tail)
        n_ref[0] = 0
        n_ref[1] = 0

    aff = aff_ref[...]
    a1, a2, a3 = _split3(aff)
    gate3 = (a1.astype(jnp.float32) + pltpu.roll(a2.astype(jnp.float32), N_EXPERTS, 1)
             + pltpu.roll(a3.astype(jnp.float32), 2 * N_EXPERTS, 1)).astype(jnp.bfloat16)
    xt = jnp.concatenate([x_ref[...], gate3], axis=1)

    for half in range(2):
        plan, k_tot = _tile_plan(st_ref, ti, half)
        srow = _stage_rows(pos_ref[0, half * HALF_EXPERTS:(half + 1) * HALF_EXPERTS, :], plan)
        wait_chunks(half, n_ref[half])

        def permute(kp, c, half=half, srow=srow):
            for u in range(2):
                k0 = pl.multiple_of((2 * kp + u) * SEQ_ALIGN, SEQ_ALIGN)
                sx_ref[half, pl.ds(k0, SEQ_ALIGN), :] = jnp.dot(_onehot_rows(srow, k0), xt,
                                                                preferred_element_type=jnp.float32).astype(sx_ref.dtype)
            return c

        lax.fori_loop(0, _n_block_pairs(k_tot), permute, 0)

        total = jnp.int32(0)
        for e, s16, nch, off, end in plan:
            @pl.when(nch > 0)
            def _(half=half, e=e, s16=s16, off=off, end=end):
                head = pl.ds(pl.multiple_of(off, BF16_ROWS), BF16_ROWS)
                sx_ref[half, head, :] = sx_ref[half, head, :] + cx_ref[e]
                part = end & (BF16_ROWS - 1)
                last = pl.ds(pl.multiple_of(off + (end - part) - s16, BF16_ROWS), BF16_ROWS)
                cx_ref[e] = jnp.where(part > 0, sx_ref[half, last, :], jnp.zeros((BF16_ROWS, XE_COLS), cx_ref.dtype))

            def issue(j, c, half=half, e=e, s16=s16, off=off):
                chunk_copy(half, e, s16, off, j).start()
                return c
            lax.fori_loop(0, nch, issue, 0)
            total = total + nch
        n_ref[half] = total

    @pl.when(ti == nt - 1)
    def _():
        wait_chunks(0, n_ref[0])
        wait_chunks(1, n_ref[1])


def dispatch(starts, hb, pos, aff, cap, cap_pad):
    n = hb.shape[0]
    nt = n // TOK_TILE
    grid_spec = pltpu.PrefetchScalarGridSpec(
        num_scalar_prefetch=1,
        grid=(nt,),
        in_specs=[pl.BlockSpec((TOK_TILE, D_MODEL), lambda i, st: (i, 0)),
                  pl.BlockSpec((1, N_EXPERTS, TOK_TILE), lambda i, st: (i, 0, 0)),
                  pl.BlockSpec((TOK_TILE, LANES), lambda i, st: (i, 0))],
        out_specs=pl.BlockSpec(memory_space=pl.ANY),
        scratch_shapes=[pltpu.VMEM((2, HALF_ROWS, XE_COLS), jnp.bfloat16),
                        pltpu.VMEM((N_EXPERTS, BF16_ROWS, XE_COLS), jnp.bfloat16),
                        pltpu.SMEM((2,), jnp.int32), pltpu.SemaphoreType.DMA((2,))],
    )
    return pl.pallas_call(
        functools.partial(_dispatch_kernel, cap=cap, cap_pad=cap_pad),
        grid_spec=grid_spec,
        out_shape=jax.ShapeDtypeStruct((N_EXPERTS, cap_pad, XE_COLS), jnp.bfloat16),
        compiler_params=_cparams("arbitrary"),
        name="moe_dispatch",
    )(starts, hb, pos, aff)


def _ffn_kernel(x_ref, w1_ref, w3_ref, w2_ref, o_ref, acc_ref, *, cap):
    e = pl.program_id(0)
    f = pl.program_id(2)
    @pl.when(f == 0)
    def _():
        acc_ref[...] = jnp.zeros(acc_ref.shape, jnp.float32)

    x = x_ref[0, :, 0:D_MODEL]
    a = jnp.dot(x, w1_ref[0, 0], preferred_element_type=jnp.float32)
    b = jnp.dot(x, w3_ref[0, 0], preferred_element_type=jnp.float32)
    hid = (a * _sigmoid(a) * b).astype(jnp.bfloat16)
    acc_ref[...] = acc_ref[...] + jnp.dot(hid, w2_ref[0, 0], preferred_element_type=jnp.float32)

    @pl.when(f == pl.num_programs(2) - 1)
    def _():
        rows = o_ref.shape[1]
        gl = x_ref[0, :, D_MODEL:XE_COLS].astype(jnp.float32)
        lane = lax.broadcasted_iota(jnp.int32, gl.shape, 1)
        mine = ((lane & (N_EXPERTS - 1)) == e) & (lane < 3 * N_EXPERTS)
        gate = jnp.sum(jnp.where(mine, gl, 0.0), axis=-1, keepdims=True)
        slot = pl.program_id(1) * rows + lax.broadcasted_iota(jnp.int32, (rows, 1), 0)
        o_ref[0] = jnp.where(slot < cap, acc_ref[...] * gate, 0.0).astype(o_ref.dtype)


def expert_ffn(xe, w1, w3, w2, layer, cap):
    e, cap_pad, _ = xe.shape
    return pl.pallas_call(
        functools.partial(_ffn_kernel, cap=cap),
        grid=(e, cap_pad // FFN_ROWS, EXPERT_FF // FFN_COLS),
        in_specs=[pl.BlockSpec((1, FFN_ROWS, XE_COLS), lambda i, m, f: (i, m, 0)),
                  pl.BlockSpec((1, 1, D_MODEL, FFN_COLS), lambda i, m, f: (layer, i, 0, f)),
                  pl.BlockSpec((1, 1, D_MODEL, FFN_COLS), lambda i, m, f: (layer, i, 0, f)),
                  pl.BlockSpec((1, 1, FFN_COLS, D_MODEL), lambda i, m, f: (layer, i, f, 0))],
        out_specs=pl.BlockSpec((1, FFN_ROWS, D_MODEL), lambda i, m, f: (i, m, 0)),
        out_shape=jax.ShapeDtypeStruct((e, cap_pad, D_MODEL), jnp.bfloat16),
        scratch_shapes=[pltpu.VMEM((FFN_ROWS, D_MODEL), jnp.float32)],
        compiler_params=_cparams("parallel", "parallel", "arbitrary"),
        name="expert_ffn",
    )(xe, w1, w3, w2)


def _combine_kernel(st_ref, ye_ref, pos_ref, h_ref, g_ref, b_ref, *rest, lp, seq_len, final):
    if final:
        out_ref, sy_ref, acc_ref, n_ref, sem_ref, ybuf_ref, osem_ref = rest
    else:
        ho_ref, hb_ref, sy_ref, acc_ref, n_ref, sem_ref = rest
    ti = pl.program_id(0)
    nt = pl.num_programs(0)
    t = TOK_TILE

    def chunk_copy(half, e, s16, off, j):
        src = pl.multiple_of(s16 + j * DMA_ROWS, BF16_ROWS)
        dst = pl.multiple_of(off + j * DMA_ROWS, DMA_ROWS)
        return pltpu.make_async_copy(ye_ref.at[e, pl.ds(src, DMA_ROWS)], sy_ref.at[half, pl.ds(dst, DMA_ROWS)],
                                     sem_ref.at[half])

    def fetch(half, tile):
        plan, _ = _tile_plan(st_ref, tile, half)
        total = jnp.int32(0)
        for e, s16, nch, off, end in plan:
            def issue(j, c, e=e, s16=s16, off=off):
                chunk_copy(half, e, s16, off, j).start()
                return c
            lax.fori_loop(0, nch, issue, 0)
            total = total + nch
        n_ref[half] = total

    @pl.when(ti == 0)
    def _():
        sy_ref[...] = jnp.zeros(sy_ref.shape, sy_ref.dtype)
        fetch(0, 0)
        fetch(1, 0)

    acc_ref[...] = jnp.zeros(acc_ref.shape, jnp.float32)
    for half in range(2):
        plan, k_tot = _tile_plan(st_ref, ti, half)
        srow = _stage_rows(pos_ref[0, half * HALF_EXPERTS:(half + 1) * HALF_EXPERTS, :], plan)

        def wait_one(_, c, half=half):
            chunk_copy(half, 0, 0, 0, 0).wait()
            return c
        lax.fori_loop(0, n_ref[half], wait_one, 0)

        def gather(kp, c, half=half, srow=srow):
            k0 = pl.multiple_of(kp * (2 * SEQ_ALIGN), 2 * SEQ_ALIGN)
            onehot = jnp.concatenate([_onehot_rows(srow, k0), _onehot_rows(srow, k0 + SEQ_ALIGN)], axis=0)
            acc_ref[...] = acc_ref[...] + lax.dot_general(
                onehot, sy_ref[half, pl.ds(k0, 2 * SEQ_ALIGN), :], (((0,), (0,)), ((), ())),
                preferred_element_type=jnp.float32)
            return c

        lax.fori_loop(0, _n_block_pairs(k_tot), gather, 0)

        @pl.when(ti + 1 < nt)
        def _(half=half):
            fetch(half, ti + 1)

    tsum = ALPHA * h_ref[...] + acc_ref[...]
    if not final:
        _norm_route(tsum, g_ref, b_ref, None, ti * t, lp, seq_len, ho_ref, hb_ref, None, None)
        return

    nts = lp // t

    def out_rows(tile):
        tok0 = lax.rem(tile, nts) * t
        first = jnp.where(tok0 == 0, N_META, 0)
        return first, jnp.clip(seq_len - tok0, 0, t) - first

    def out_copy(slot, bi, src, dst):
        return pltpu.make_async_copy(ybuf_ref.at[slot, pl.ds(src, OUT_ROWS)], out_ref.at[bi, pl.ds(dst, OUT_ROWS)],
                                     osem_ref.at[slot])

    def wait_out(slot, tile):
        def body(_, c):
            out_copy(slot, 0, 0, 0).wait()
            return c
        lax.fori_loop(0, out_rows(tile)[1] // OUT_ROWS, body, 0)

    slot = ti & 1

    @pl.when(ti >= 2)
    def _():
        wait_out(slot, ti - 2)

    y, _ = _layer_norm_rows(tsum, g_ref, b_ref, ti * t, lp, seq_len)
    ybuf_ref[slot] = y
    first, rows = out_rows(ti)
    bi = ti // nts
    tok0 = lax.rem(ti, nts) * t

    def send(j, c):
        src = pl.multiple_of(first + j * OUT_ROWS, OUT_ROWS)
        out_copy(slot, bi, src, pl.multiple_of(tok0 - N_META + src, OUT_ROWS)).start()
        return c
    lax.fori_loop(0, rows // OUT_ROWS, send, 0)

    @pl.when(ti == nt - 1)
    def _():
        wait_out(slot, ti)

        @pl.when(nt > 1)
        def _():
            wait_out(1 - slot, ti - 1)


def combine(starts, ye, pos, h, g, bb, lp, seq_len, final):
    n = h.shape[0]
    nt = n // TOK_TILE
    row = lambda i, st: (i, 0)
    fix = lambda i, st: (0, 0)
    scratch = [pltpu.VMEM((2, HALF_ROWS, D_MODEL), jnp.bfloat16), pltpu.VMEM((TOK_TILE, D_MODEL), jnp.float32),
               pltpu.SMEM((2,), jnp.int32), pltpu.SemaphoreType.DMA((2,))]
    if final:
        assert seq_len % OUT_ROWS == 0 and N_META % OUT_ROWS == 0
        out_specs = pl.BlockSpec(memory_space=pl.ANY)
        out_shape = jax.ShapeDtypeStruct((n // lp, seq_len - N_META, D_MODEL), jnp.float32)
        scratch += [pltpu.VMEM((2, TOK_TILE, D_MODEL), jnp.float32), pltpu.SemaphoreType.DMA((2,))]
    else:
        out_specs = [pl.BlockSpec((TOK_TILE, D_MODEL), row), pl.BlockSpec((TOK_TILE, D_MODEL), row)]
        out_shape = [jax.ShapeDtypeStruct((n, D_MODEL), jnp.float32), jax.ShapeDtypeStruct((n, D_MODEL), jnp.bfloat16)]
    grid_spec = pltpu.PrefetchScalarGridSpec(
        num_scalar_prefetch=1,
        grid=(nt,),
        in_specs=[pl.BlockSpec(memory_space=pl.ANY), pl.BlockSpec((1, N_EXPERTS, TOK_TILE), lambda i, st: (i, 0, 0)),
                  pl.BlockSpec((TOK_TILE, D_MODEL), row), pl.BlockSpec((1, D_MODEL), fix), pl.BlockSpec((1, D_MODEL), fix)],
        out_specs=out_specs,
        scratch_shapes=scratch,
    )
    return pl.pallas_call(
        functools.partial(_combine_kernel, lp=lp, seq_len=seq_len, final=final),
        grid_spec=grid_spec,
        out_shape=out_shape,
        compiler_params=_cparams("arbitrary"),
        name="moe_combine_out" if final else "moe_combine",
    )(starts, ye, pos, h, g, bb)


def moe_block(h, hb, aff, afft, w1, w3, w2, layer, g, bb, n_real, lp, seq_len, final=False):
    cap = max(1, CAPACITY_FACTOR * n_real // N_EXPERTS)
    cap_pad = -(-(cap + DMA_ROWS) // FFN_ROWS) * FFN_ROWS
    pos, start3 = route(afft, cap)
    starts = start3[:, 0, :N_EXPERTS].reshape(-1)
    xe = dispatch(starts, hb, pos, aff, cap, cap_pad)
    ye = expert_ffn(xe, w1, w3, w2, layer, cap)
    return combine(starts, ye, pos, h, g, bb, lp, seq_len, final)


def _prep_weights(meta, w_in, conv_w, conv_b, a_log_f, a_log_b, dt_bias_f, dt_bias_b, d_skip, ssd_norm_w,
                  lam_q1, lam_k1, lam_q2, lam_k2, subln_w, rel_bias, w_out, pool_w, pool_b, pool_scale,
                  ln1_g, ln1_b, ln2_g, ln2_b, w_router, w1, w3, w2):
    bf = jnp.bfloat16
    f32 = jnp.float32
    w = w_in[0]
    o = [0, SSD_INNER, SSD_INNER + SSD_CONV_CH, SSD_INNER + SSD_CONV_CH + 2 * SSD_HEADS]
    wz, wxbc, wdt = w[:, o[0]:o[1]], w[:, o[1]:o[2]], w[:, o[2]:o[3]]
    wq, wk, wv = (w[:, o[3] + i * ATT_QK:o[3] + (i + 1) * ATT_QK] for i in range(3))
    pad = jnp.zeros((D_MODEL, _PROJ_COLS - _COL_DT - 2 * SSD_HEADS), f32)
    w_tok = jnp.concatenate([wz, wxbc, wdt, pad], axis=1).astype(bf)
    w_qv = jnp.concatenate([wq * (ATT_HEAD_DIM ** -0.5 * LOG2E), wv], axis=1).T.astype(bf)
    lane_pad = lambda v: jnp.pad(v.astype(f32), (0, LANES - v.shape[0]))
    par = jnp.stack([lane_pad(jnp.concatenate([dt_bias_f[0], dt_bias_b[0]])),
                     lane_pad(jnp.concatenate([a_log_f[0], a_log_b[0]]))] + [jnp.zeros((LANES,), f32)] * 6)
    lam4 = jnp.stack([lane_pad(lam_q1[0]), lane_pad(lam_k1[0]), lane_pad(lam_q2[0]), lane_pad(lam_k2[0])]
                     + [jnp.zeros((LANES,), f32)] * 4)
    return dict(
        meta=meta, w_tok=w_tok, w_k=wk.astype(bf), w_qv=w_qv, conv_w=conv_w[0, :, 0, :], conv_b=conv_b[0][None], par=par,
        dsk=jnp.repeat(d_skip[0], SSD_HEAD_DIM)[None], nw=ssd_norm_w[0][None], lam4=lam4,
        subw=jnp.broadcast_to(subln_w[0][:, None], (ATT_V_DIM, SEQ_ALIGN)), rel_bias=rel_bias,
        w_out=w_out[0].astype(bf), pool_w=pool_w[0].astype(bf), pool_b=pool_b[0][None], pool_scale=pool_scale[0][None],
        ln1_g=ln1_g, ln1_b=ln1_b, ln2_g=ln2_g, ln2_b=ln2_b,
        w_r=jnp.pad(w_router, ((0, 0), (0, 0), (0, LANES - N_EXPERTS))).astype(bf),
        w1=w1.astype(bf), w3=w3.astype(bf), w2=w2.astype(bf))


def _trunk(x, p, bias5):
    b, s, _ = x.shape
    seq_len = s + N_META
    lp = -(-seq_len // SEQ_ALIGN) * SEQ_ALIGN
    n = b * lp
    n_real = b * seq_len
    meta = jnp.broadcast_to(p["meta"][None].astype(x.dtype), (b, N_META, D_MODEL))
    h = jnp.concatenate([meta, x, jnp.zeros((b, lp - seq_len, D_MODEL), x.dtype)], axis=1).reshape(n, D_MODEL)
    hb = h.astype(jnp.bfloat16)
    lambda_init = 0.8 - 0.6 * math.exp(-0.3 * 0)
    tm = next(c for c in (ROW_TILE, ROW_TILE // 2, SEQ_ALIGN) if n % c == 0)

    proj = mm_nn(hb, p["w_tok"], jnp.float32, tm, 512)
    k3 = mm_nn(hb, p["w_k"], jnp.bfloat16, tm, 512).reshape(b, lp, ATT_QK)
    qv = mm_nt_blocked(p["w_qv"], hb, jnp.bfloat16, tm, 512)
    proj3 = proj.reshape(b, lp, _PROJ_COLS)
    xbc = conv_silu(proj3, p["conv_w"], p["conv_b"]).reshape(n, SSD_CONV_CH)
    yf = ssd_scan(xbc, proj, p["par"], b, lp, seq_len, False)
    y_ssd = ssd_scan(xbc, proj, p["par"], b, lp, seq_len, True, gate=(yf, p["dsk"], p["nw"]))
    y_att = diff_attention(qv, k3, bias5, p["lam4"], p["subw"], b, lp, seq_len, lambda_init)
    h, hb, aff, afft = out_proj_norm_route(y_ssd, y_att, p["w_out"], h, p["ln1_g"][0][None], p["ln1_b"][0][None],
                                           p["w_r"][0], lp, seq_len)
    h, hb = moe_block(h, hb, aff, afft, p["w1"], p["w3"], p["w2"], 0, p["ln2_g"][0][None], p["ln2_b"][0][None],
                      n_real, lp, seq_len)
    d = pool_diff(h.reshape(b, lp, D_MODEL), seq_len).reshape(n, D_MODEL)
    h, hb, aff, afft = pool_proj_norm_route(d, p["pool_w"], p["pool_b"], p["pool_scale"], h, p["ln1_g"][1][None],
                                            p["ln1_b"][1][None], p["w_r"][1], lp, seq_len)
    return moe_block(h, hb, aff, afft, p["w1"], p["w3"], p["w2"], 1, p["ln2_g"][1][None], p["ln2_b"][1][None],
                     n_real, lp, seq_len, final=True)


def kernel(x_prompt, x_sample, meta, w_in, conv_w, conv_b, a_log_f, a_log_b, dt_bias_f, dt_bias_b, d_skip, ssd_norm_w, lam_q1, lam_k1, lam_q2, lam_k2, subln_w, rel_bias, w_out, pool_w, pool_b, pool_scale, ln1_g, ln1_b, ln2_g, ln2_b, w_router, w1, w3, w2):
    p = _prep_weights(meta, w_in, conv_w, conv_b, a_log_f, a_log_b, dt_bias_f, dt_bias_b, d_skip, ssd_norm_w,
                      lam_q1, lam_k1, lam_q2, lam_k2, subln_w, rel_bias, w_out, pool_w, pool_b, pool_scale,
                      ln1_g, ln1_b, ln2_g, ln2_b, w_router, w1, w3, w2)
    bias5 = bias_tiles(rel_bias)
    return (_trunk(x_prompt, p, bias5), _trunk(x_sample, p, bias5))
```
